```python
import math
import jax, jax.numpy as jnp
from jax import lax
import numpy as np

D_MODEL = 4096
BATCH = 2
SEQ = 8192
DEPTH = 1

ATT_HEADS = 32
ATT_KV_HEADS = 4
ATT_GROUP = ATT_HEADS // ATT_KV_HEADS
ATT_HEAD_DIM = 64
WINDOW = 128
ATT_BLOCK = 128
RET_HEADS = 8
RET_KEY_DIM = 128
RET_VAL_DIM = 256
RET_CHUNK = 128
ATT_Q_WIDTH = ATT_HEADS * ATT_HEAD_DIM
ATT_KV_WIDTH = ATT_KV_HEADS * ATT_HEAD_DIM
RET_QK_WIDTH = RET_HEADS * RET_KEY_DIM
RET_V_WIDTH = RET_HEADS * RET_VAL_DIM
IN_SPLITS = [ATT_Q_WIDTH, ATT_KV_WIDTH, ATT_KV_WIDTH, RET_QK_WIDTH, RET_QK_WIDTH, RET_V_WIDTH, RET_V_WIDTH, D_MODEL, D_MODEL]
IN_OFFSETS = [int(v) for v in np.cumsum(IN_SPLITS)[:-1]]
IN_WIDTH = int(sum(IN_SPLITS))
N_EXPERTS = 32
TOP_K = 4
EXPERT_FF = 1536
EXPERT_BLOCK = 256
SWIGLU_LIMIT = 7.0
SWIGLU_ALPHA = 1.702
PLE_DIM = 256
LN_EPS = 1e-5
DEEPNORM_ALPHA = float((2 * DEPTH) ** 0.25)
DEEPNORM_BETA = float((8 * DEPTH) ** -0.25)

kernel_name = 'hybrid_swa_retention_moe_block'


def _layer_norm(x, g, b):
    xf = x.astype(jnp.float32)
    mu = jnp.mean(xf, axis=-1, keepdims=True)
    var = jnp.mean(jnp.square(xf - mu), axis=-1, keepdims=True)
    y = (xf - mu) * lax.rsqrt(var + LN_EPS) * g.astype(jnp.float32) + b.astype(jnp.float32)
    return y.astype(x.dtype)


def _alibi_slopes():
    return jnp.asarray(2.0 ** (-8.0 * (np.arange(ATT_HEADS) + 1) / ATT_HEADS), dtype=jnp.float32)


def _sliding_window_attention(q, k, v, sinks):
    B, S = q.shape[0], q.shape[1]
    nb = S // ATT_BLOCK
    qb = q.reshape(B, nb, ATT_BLOCK, ATT_KV_HEADS, ATT_GROUP, ATT_HEAD_DIM)

    def band(t):
        tp = jnp.pad(t, ((0, 0), (ATT_BLOCK, 0), (0, 0), (0, 0)))
        tp = tp.reshape(B, nb + 1, ATT_BLOCK, ATT_KV_HEADS, ATT_HEAD_DIM)
        return jnp.concatenate([tp[:, :-1], tp[:, 1:]], axis=2)

    kb, vb = band(k), band(v)
    s = jnp.einsum('bnikgd,bnjkd->bnkgij', qb, kb, preferred_element_type=jnp.float32) * (ATT_HEAD_DIM ** -0.5)
    qi = jnp.arange(ATT_BLOCK)[:, None]
    kj = jnp.arange(2 * ATT_BLOCK)[None, :]
    dist = ATT_BLOCK + qi - kj
    key_pos = (jnp.arange(nb)[:, None, None] - 1) * ATT_BLOCK + kj[None]
    valid = (dist >= 0) & (dist < WINDOW) & (key_pos >= 0)
    slopes = _alibi_slopes().reshape(ATT_KV_HEADS, ATT_GROUP, 1, 1)
    s = s - slopes * dist.astype(jnp.float32)
    s = jnp.where(valid[None, :, None, None], s, -jnp.inf)
    sink = sinks.astype(jnp.float32).reshape(1, 1, ATT_KV_HEADS, ATT_GROUP, 1, 1)
    m = jnp.maximum(jnp.max(s, axis=-1, keepdims=True), sink)
    e = jnp.exp(s - m)
    probs = e / (jnp.sum(e, axis=-1, keepdims=True) + jnp.exp(sink - m))
    o = jnp.einsum('bnkgij,bnjkd->bnikgd', probs.astype(v.dtype), vb)
    return o.reshape(B, S, ATT_Q_WIDTH)


def _retention(q, k, v):
    B, S = q.shape[0], q.shape[1]
    nc = S // RET_CHUNK
    log_gamma = jnp.log(1.0 - 2.0 ** (-5.0 - jnp.arange(RET_HEADS, dtype=jnp.float32)))
    pos = jnp.arange(RET_CHUNK, dtype=jnp.float32)
    diff = pos[:, None] - pos[None, :]
    inner_decay = jnp.where(diff >= 0, jnp.exp(jnp.maximum(diff, 0.0) * log_gamma[:, None, None]), 0.0)
    q_decay = jnp.exp((pos + 1.0) * log_gamma[:, None])[None, :, :, None]
    k_decay = jnp.exp((RET_CHUNK - 1.0 - pos) * log_gamma[:, None])[None, :, :, None]
    chunk_decay = jnp.exp(RET_CHUNK * log_gamma)[None, :, None, None]

    def to_chunks(t):
        return t.astype(jnp.float32).reshape(B, nc, RET_CHUNK, RET_HEADS, t.shape[-1]).transpose(1, 0, 3, 2, 4)

    qc, kc, vc = to_chunks(q), to_chunks(k) * (RET_KEY_DIM ** -0.5), to_chunks(v)

    def step(state, chunk):
        qi, ki, vi = chunk
        inner = jnp.einsum('bhij,bhje->bhie', jnp.einsum('bhid,bhjd->bhij', qi, ki) * inner_decay, vi)
        cross = jnp.einsum('bhid,bhde->bhie', qi, state) * q_decay
        state = state * chunk_decay + jnp.einsum('bhjd,bhje->bhde', ki * k_decay, vi)
        return state, inner + cross

    state0 = jnp.zeros((B, RET_HEADS, RET_KEY_DIM, RET_VAL_DIM), jnp.float32)
    _, out = lax.scan(step, state0, (qc, kc, vc))
    return out.transpose(1, 0, 3, 2, 4).reshape(B, S, RET_HEADS, RET_VAL_DIM)


def _retention_branch(rq, rk, rv, rg, norm_g):
    B, S = rq.shape[0], rq.shape[1]
    o = _retention(rq.reshape(B, S, RET_HEADS, RET_KEY_DIM), rk.reshape(B, S, RET_HEADS, RET_KEY_DIM),
                   rv.reshape(B, S, RET_HEADS, RET_VAL_DIM))
    mu = jnp.mean(o, axis=-1, keepdims=True)
    var = jnp.mean(jnp.square(o - mu), axis=-1, keepdims=True)
    o = ((o - mu) * lax.rsqrt(var + LN_EPS)).reshape(B, S, RET_V_WIDTH) * norm_g.astype(jnp.float32)
    return (jax.nn.silu(rg.astype(jnp.float32)) * o).astype(rq.dtype)


def _clamped_swiglu(gu):
    glu, lin = gu[..., :EXPERT_FF], gu[..., EXPERT_FF:]
    glu = jnp.minimum(glu, SWIGLU_LIMIT)
    lin = jnp.clip(lin, -SWIGLU_LIMIT, SWIGLU_LIMIT)
    return glu * jax.nn.sigmoid(SWIGLU_ALPHA * glu) * (lin + 1.0)


def _moe(h, w_router, b_router, w_gate_up, b_gate_up, w_down, b_down):
    T = h.shape[0]
    TK = T * TOP_K
    nb = -(-TK // EXPERT_BLOCK) + N_EXPERTS
    logits = jnp.dot(h, w_router, preferred_element_type=jnp.float32) + b_router.astype(jnp.float32)
    top_vals, top_idx = lax.top_k(logits, TOP_K)
    gates = jax.nn.softmax(top_vals, axis=-1).reshape(TK)
    flat_e = top_idx.reshape(TK).astype(jnp.int32)
    order = jnp.argsort(flat_e)
    sorted_e = flat_e[order]
    counts = jnp.bincount(flat_e, length=N_EXPERTS).astype(jnp.int32)
    padded = (counts + EXPERT_BLOCK - 1) // EXPERT_BLOCK * EXPERT_BLOCK
    group_start = jnp.cumsum(counts) - counts
    padded_end = jnp.cumsum(padded)
    padded_start = padded_end - padded
    dest = padded_start[sorted_e] + jnp.arange(TK, dtype=jnp.int32) - group_start[sorted_e]
    n_rows = nb * EXPERT_BLOCK
    row_token = jnp.zeros((n_rows,), jnp.int32).at[dest].set((order // TOP_K).astype(jnp.int32))
    row_gate = jnp.zeros((n_rows,), jnp.float32).at[dest].set(gates[order])
    block_start = jnp.arange(nb, dtype=jnp.int32) * EXPERT_BLOCK
    block_expert = jnp.minimum(jnp.searchsorted(padded_end, block_start, side='right'), N_EXPERTS - 1).astype(jnp.int32)

    def block_step(acc, blk):
        e, rows, g = blk
        xb = h[rows]
        gu = xb @ w_gate_up[e] + b_gate_up[e]
        y = _clamped_swiglu(gu) @ w_down[e] + b_down[e]
        return acc.at[rows].add(y * g[:, None].astype(y.dtype)), None

    out, _ = lax.scan(block_step, jnp.zeros_like(h),
                      (block_expert, row_token.reshape(nb, EXPERT_BLOCK), row_gate.reshape(nb, EXPERT_BLOCK)))
    return out


def setup_inputs(seed: int = 0) -> dict:
    key = jax.random.key(seed)
    ks = jax.random.split(key, 22)
    f32 = jnp.float32
    L = DEPTH
    beta = DEEPNORM_BETA

    def nrm(k, shape, scale):
        return jax.random.normal(k, shape, f32) * scale

    return {
        'x': nrm(ks[0], (BATCH, SEQ, D_MODEL), 1.0),
        'p': nrm(ks[1], (DEPTH, BATCH, SEQ, PLE_DIM), 1.0),
        'w_in': nrm(ks[2], (L, D_MODEL, IN_WIDTH), D_MODEL ** -0.5),
        'attn_sinks': nrm(ks[3], (L, ATT_HEADS), 1.0),
        'ret_norm_g': 1.0 + nrm(ks[4], (L, RET_V_WIDTH), 0.02),
        'w_att_out': nrm(ks[5], (L, ATT_Q_WIDTH, D_MODEL), ATT_Q_WIDTH ** -0.5),
        'w_ret_out': nrm(ks[6], (L, RET_V_WIDTH, D_MODEL), RET_V_WIDTH ** -0.5),
        'w_out': nrm(ks[7], (L, D_MODEL, D_MODEL), D_MODEL ** -0.5 * beta),
        'ln1_g': 1.0 + nrm(ks[8], (L, D_MODEL), 0.02),
        'ln1_b': nrm(ks[9], (L, D_MODEL), 0.02),
        'w_router': nrm(ks[10], (L, D_MODEL, N_EXPERTS), D_MODEL ** -0.5),
        'b_router': nrm(ks[11], (L, N_EXPERTS), 0.01),
        'w_gate_up': nrm(ks[12], (L, N_EXPERTS, D_MODEL, 2 * EXPERT_FF), D_MODEL ** -0.5),
        'b_gate_up': nrm(ks[13], (L, N_EXPERTS, 2 * EXPERT_FF), 0.01),
        'w_down': nrm(ks[14], (L, N_EXPERTS, EXPERT_FF, D_MODEL), EXPERT_FF ** -0.5 * beta),
        'b_down': nrm(ks[15], (L, N_EXPERTS, D_MODEL), 0.01),
        'ln2_g': 1.0 + nrm(ks[16], (L, D_MODEL), 0.02),
        'ln2_b': nrm(ks[17], (L, D_MODEL), 0.02),
        'w_ple': nrm(ks[18], (L, PLE_DIM, D_MODEL), PLE_DIM ** -0.5 * beta),
        'w_ple_gate': nrm(ks[19], (L, D_MODEL, D_MODEL), D_MODEL ** -0.5),
        'ln3_g': 1.0 + nrm(ks[20], (L, D_MODEL), 0.02),
        'ln3_b': nrm(ks[21], (L, D_MODEL), 0.02),
    }


def reference(x, p, w_in, attn_sinks, ret_norm_g, w_att_out, w_ret_out, w_out, ln1_g, ln1_b,
              w_router, b_router, w_gate_up, b_gate_up, w_down, b_down, ln2_g, ln2_b,
              w_ple, w_ple_gate, ln3_g, ln3_b):
    B, S, D = x.shape
    for i in range(DEPTH):
        proj = x @ w_in[i]
        aq, ak, av, rq, rk, rv, rg, ga, gr = jnp.split(proj, IN_OFFSETS, axis=-1)
        y_att = _sliding_window_attention(aq.reshape(B, S, ATT_HEADS, ATT_HEAD_DIM),
                                          ak.reshape(B, S, ATT_KV_HEADS, ATT_HEAD_DIM),
                                          av.reshape(B, S, ATT_KV_HEADS, ATT_HEAD_DIM), attn_sinks[i])
        y_ret = _retention_branch(rq, rk, rv, rg, ret_norm_g[i])
        merged = jax.nn.sigmoid(ga) * (y_att @ w_att_out[i]) + jax.nn.sigmoid(gr) * (y_ret @ w_ret_out[i])
        x = _layer_norm(DEEPNORM_ALPHA * x + merged @ w_out[i], ln1_g[i], ln1_b[i])
        moe_out = _moe(x.reshape(B * S, D), w_router[i], b_router[i], w_gate_up[i], b_gate_up[i],
                       w_down[i], b_down[i]).reshape(B, S, D)
        x = _layer_norm(DEEPNORM_ALPHA * x + moe_out, ln2_g[i], ln2_b[i])
        ple = (p[i] @ w_ple[i]) * jax.nn.sigmoid(x @ w_ple_gate[i])
        x = _layer_norm(DEEPNORM_ALPHA * x + ple, ln3_g[i], ln3_b[i])
    return x
```

```python
import functools
import math

import jax
import jax.numpy as jnp
import numpy as np
from jax import lax
from jax.experimental import pallas as pl
from jax.experimental.pallas import tpu as pltpu

F32 = jnp.float32
BF16 = jnp.bfloat16

ATT_HEADS = 32
ATT_KV_HEADS = 4
ATT_GROUP = ATT_HEADS // ATT_KV_HEADS
ATT_HEAD_DIM = 64
ATT_BLOCK = 128
RET_HEADS = 8
RET_KEY_DIM = 128
RET_VAL_DIM = 256
RET_CHUNK = 128
N_EXPERTS = 32
TOP_K = 4
EXPERT_FF = 1536
SWIGLU_LIMIT = 7.0
SWIGLU_ALPHA = 1.702
LN_EPS = 1e-5
DEPTH = 1
DEEPNORM_ALPHA = float((2 * DEPTH) ** 0.25)

ATT_Q_WIDTH = ATT_HEADS * ATT_HEAD_DIM
ATT_KV_WIDTH = ATT_KV_HEADS * ATT_HEAD_DIM
RET_QK_WIDTH = RET_HEADS * RET_KEY_DIM
RET_V_WIDTH = RET_HEADS * RET_VAL_DIM

LANES = 128
VMEM_LIMIT_BYTES = 56 * 1024 * 1024
MM_TM = 512
MM_TN = 512
LN_TM = 256
COMBINE_TM = 128
MOE_BM = 512
MOE_TF = 256
MOE_DC = 512
ROUTER_PAD = LANES
NEG_BIG = -1e30


def _params(*sem):
    return pltpu.CompilerParams(dimension_semantics=sem, vmem_limit_bytes=VMEM_LIMIT_BYTES)


def _sigmoid(x):
    return 1.0 / (1.0 + jnp.exp(-x))


def _proj_kernel(x_ref, w_ref, o_ref):
    o_ref[...] = jnp.dot(x_ref[...], w_ref[...], preferred_element_type=F32).astype(o_ref.dtype)


def _proj(xb, wb, col_off, ncols, name):
    m, k = xb.shape
    tm, tn = min(MM_TM, m), MM_TN
    assert m % tm == 0 and ncols % tn == 0 and col_off % tn == 0
    off = col_off // tn
    return pl.pallas_call(
        _proj_kernel,
        grid=(ncols // tn, m // tm),
        in_specs=[pl.BlockSpec((tm, k), lambda j, i: (i, 0)),
                  pl.BlockSpec((k, tn), lambda j, i: (0, j + off))],
        out_specs=pl.BlockSpec((tm, tn), lambda j, i: (i, j)),
        out_shape=jax.ShapeDtypeStruct((m, ncols), BF16),
        compiler_params=_params("arbitrary", "arbitrary"),
        name=name,
    )(xb, wb)


def _merge_kernel(ya_ref, yr_ref, wa_ref, wr_ref, ga_ref, gr_ref, o_ref):
    a = jnp.dot(ya_ref[...], wa_ref[...], preferred_element_type=F32)
    r = jnp.dot(yr_ref[...], wr_ref[...], preferred_element_type=F32)
    ga = _sigmoid(ga_ref[...].astype(F32))
    gr = _sigmoid(gr_ref[...].astype(F32))
    o_ref[...] = (ga * a + gr * r).astype(o_ref.dtype)


def _merge(ya, yr, wa, wr, ga, gr):
    m, ka = ya.shape
    kr = yr.shape[1]
    n = wa.shape[1]
    tm, tn = min(MM_TM, m), MM_TN
    return pl.pallas_call(
        _merge_kernel,
        grid=(n // tn, m // tm),
        in_specs=[pl.BlockSpec((tm, ka), lambda j, i: (i, 0)),
                  pl.BlockSpec((tm, kr), lambda j, i: (i, 0)),
                  pl.BlockSpec((ka, tn), lambda j, i: (0, j)),
                  pl.BlockSpec((kr, tn), lambda j, i: (0, j)),
                  pl.BlockSpec((tm, tn), lambda j, i: (i, j)),
                  pl.BlockSpec((tm, tn), lambda j, i: (i, j))],
        out_specs=pl.BlockSpec((tm, tn), lambda j, i: (i, j)),
        out_shape=jax.ShapeDtypeStruct((m, n), BF16),
        compiler_params=_params("arbitrary", "arbitrary"),
        name="merge",
    )(ya, yr, wa, wr, ga, gr)


def _resid_mm_kernel(a_ref, w_ref, x_ref, o_ref):
    acc = jnp.dot(a_ref[...], w_ref[...], preferred_element_type=F32)
    o_ref[...] = DEEPNORM_ALPHA * x_ref[...] + acc


def _resid_mm(a, w, x):
    m, k = a.shape
    n = w.shape[1]
    tm, tn = min(MM_TM, m), MM_TN
    return pl.pallas_call(
        _resid_mm_kernel,
        grid=(n // tn, m // tm),
        in_specs=[pl.BlockSpec((tm, k), lambda j, i: (i, 0)),
                  pl.BlockSpec((k, tn), lambda j, i: (0, j)),
                  pl.BlockSpec((tm, tn), lambda j, i: (i, j))],
        out_specs=pl.BlockSpec((tm, tn), lambda j, i: (i, j)),
        out_shape=jax.ShapeDtypeStruct((m, n), F32),
        compiler_params=_params("arbitrary", "arbitrary"),
        name="resid_mm",
    )(a, w, x)


def _ple_kernel(xb_ref, wg_ref, p_ref, wp_ref, x_ref, o_ref):
    gate = jnp.dot(xb_ref[...], wg_ref[...], preferred_element_type=F32)
    emb = jnp.dot(p_ref[...].astype(BF16), wp_ref[...], preferred_element_type=F32)
    o_ref[...] = DEEPNORM_ALPHA * x_ref[...] + emb * _sigmoid(gate)


def _ple(xb, wg, p, wp, x):
    m, k = xb.shape
    n = wg.shape[1]
    kp = p.shape[1]
    tm, tn = min(MM_TM, m), MM_TN
    return pl.pallas_call(
        _ple_kernel,
        grid=(n // tn, m // tm),
        in_specs=[pl.BlockSpec((tm, k), lambda j, i: (i, 0)),
                  pl.BlockSpec((k, tn), lambda j, i: (0, j)),
                  pl.BlockSpec((tm, kp), lambda j, i: (i, 0)),
                  pl.BlockSpec((kp, tn), lambda j, i: (0, j)),
                  pl.BlockSpec((tm, tn), lambda j, i: (i, j))],
        out_specs=pl.BlockSpec((tm, tn), lambda j, i: (i, j)),
        out_shape=jax.ShapeDtypeStruct((m, n), F32),
        compiler_params=_params("arbitrary", "arbitrary"),
        name="ple",
    )(xb, wg, p, wp, x)


def _alibi_slope(h):
    return float(2.0 ** (-8.0 * (h + 1) / ATT_HEADS))


def _attn_kernel(sink_ref, q_ref, kvc_ref, kvp_ref, o_ref, *, nblk):
    blk = ATT_BLOCK
    n = lax.rem(pl.program_id(0), nblk)
    has_prev = n > 0
    row = lax.broadcasted_iota(jnp.int32, (blk, 2 * blk), 0)
    col = lax.broadcasted_iota(jnp.int32, (blk, 2 * blk), 1)
    dist = blk + row - col
    valid = (dist >= 0) & (dist < blk) & ((col >= blk) | has_prev)
    distf = dist.astype(F32)
    lane = lax.broadcasted_iota(jnp.int32, (2 * blk, LANES), 1)
    lo = lane < ATT_HEAD_DIM
    scale = ATT_HEAD_DIM ** -0.5

    def halves(t, half):
        swapped = jnp.concatenate([t[:, ATT_HEAD_DIM:], t[:, :ATT_HEAD_DIM]], axis=1)
        zero = jnp.zeros_like(t)
        if half == 0:
            return jnp.where(lo, t, zero), jnp.where(lo, zero, swapped)
        return jnp.where(lo, swapped, zero), jnp.where(lo, zero, t)

    for kh in range(ATT_KV_HEADS):
        tile, half = kh // 2, kh % 2
        ks = slice(tile * LANES, (tile + 1) * LANES)
        vs = slice(ATT_KV_WIDTH + tile * LANES, ATT_KV_WIDTH + (tile + 1) * LANES)
        kcat = jnp.concatenate([kvp_ref[:, ks], kvc_ref[:, ks]], axis=0)
        vcat = jnp.concatenate([kvp_ref[:, vs], kvc_ref[:, vs]], axis=0)
        k_lo, k_hi = halves(kcat, half)
        v_lo, v_hi = halves(vcat, half)
        for jp in range(ATT_GROUP // 2):
            h0 = kh * ATT_GROUP + 2 * jp
            qs = slice(h0 * ATT_HEAD_DIM, h0 * ATT_HEAD_DIM + LANES)
            qp = q_ref[:, qs]
            acc = jnp.zeros((blk, LANES), F32)
            for par, (kx, vx) in enumerate(((k_lo, v_lo), (k_hi, v_hi))):
                h = h0 + par
                s = lax.dot_general(qp, kx, (((1,), (1,)), ((), ())), preferred_element_type=F32)
                s = s * scale - _alibi_slope(h) * distf
                s = jnp.where(valid, s, -jnp.inf)
                sink = sink_ref[h]
                m = jnp.maximum(jnp.max(s, axis=-1, keepdims=True), sink)
                e = jnp.exp(s - m)
                denom = jnp.sum(e, axis=-1, keepdims=True) + jnp.exp(sink - m)
                pv = jnp.dot(e.astype(BF16), vx, preferred_element_type=F32)
                acc = acc + pv * (1.0 / denom)
            o_ref[:, qs] = acc.astype(o_ref.dtype)


def _attention(aq, akv, sinks, nblk):
    t = aq.shape[0]
    blk = ATT_BLOCK
    return pl.pallas_call(
        functools.partial(_attn_kernel, nblk=nblk),
        grid=(t // blk,),
        in_specs=[pl.BlockSpec(memory_space=pltpu.SMEM),
                  pl.BlockSpec((blk, ATT_Q_WIDTH), lambda i: (i, 0)),
                  pl.BlockSpec((blk, 2 * ATT_KV_WIDTH), lambda i: (i, 0)),
                  pl.BlockSpec((blk, 2 * ATT_KV_WIDTH), lambda i: (jnp.maximum(i - 1, 0), 0))],
        out_specs=pl.BlockSpec((blk, ATT_Q_WIDTH), lambda i: (i, 0)),
        out_shape=jax.ShapeDtypeStruct((t, ATT_Q_WIDTH), BF16),
        compiler_params=_params("arbitrary"),
        name="swa_attention",
    )(sinks, aq, akv, akv)


def _ret_constants():
    c = RET_CHUNK
    gam = 1.0 - 2.0 ** (-5.0 - np.arange(RET_HEADS, dtype=np.float64))
    lg = np.log(gam)
    pos = np.arange(c, dtype=np.float64)
    diff = pos[:, None] - pos[None, :]
    kscale = RET_KEY_DIM ** -0.5
    inner = np.where(diff >= 0, np.exp(np.maximum(diff, 0.0) * lg[:, None, None]), 0.0) * kscale
    qdec = np.exp((pos + 1.0) * lg[:, None])
    kdec = np.exp((c - 1.0 - pos) * lg[:, None]) * kscale
    cdec = np.exp(c * lg)
    qdec_b = np.broadcast_to(qdec[:, :, None], (RET_HEADS, c, RET_VAL_DIM))
    kdec_b = np.broadcast_to(kdec[:, :, None], (RET_HEADS, c, RET_KEY_DIM))
    return (jnp.asarray(inner, F32), jnp.asarray(qdec_b, F32), jnp.asarray(kdec_b, F32),
            [float(v) for v in cdec])


def _ret_kernel(qk_ref, v_ref, g_ref, ng_ref, inner_ref, qdec_ref, kdec_ref, o_ref, state_ref, *, cdec):
    @pl.when(pl.program_id(1) == 0)
    def _():
        state_ref[...] = jnp.zeros_like(state_ref)

    dk, dv = RET_KEY_DIM, RET_VAL_DIM
    for h in range(RET_HEADS):
        q = qk_ref[:, h * dk:(h + 1) * dk]
        k = qk_ref[:, RET_QK_WIDTH + h * dk:RET_QK_WIDTH + (h + 1) * dk]
        v = v_ref[:, h * dv:(h + 1) * dv]
        st = state_ref[h]
        qk = lax.dot_general(q, k, (((1,), (1,)), ((), ())), preferred_element_type=F32)
        a = (qk * inner_ref[h]).astype(BF16)
        inner = jnp.dot(a, v, preferred_element_type=F32)
        cross = jnp.dot(q, st.astype(BF16), preferred_element_type=F32) * qdec_ref[h]
        kd = (k.astype(F32) * kdec_ref[h]).astype(BF16)
        upd = lax.dot_general(kd, v, (((0,), (0,)), ((), ())), preferred_element_type=F32)
        state_ref[h] = st * cdec[h] + upd
        o = inner + cross
        mu = jnp.mean(o, axis=-1, keepdims=True)
        d = o - mu
        var = jnp.mean(d * d, axis=-1, keepdims=True)
        y = d * lax.rsqrt(var + LN_EPS) * ng_ref[:, h * dv:(h + 1) * dv]
        g = g_ref[:, h * dv:(h + 1) * dv].astype(F32)
        o_ref[:, h * dv:(h + 1) * dv] = (g * _sigmoid(g) * y).astype(o_ref.dtype)


def _retention(rqk, rv, rg, norm_g, batch, nchunk):
    t = rqk.shape[0]
    c = RET_CHUNK
    inner, qdec, kdec, cdec = _ret_constants()
    row = lambda b, n: (b * nchunk + n, 0)
    const3 = lambda b, n: (0, 0, 0)
    return pl.pallas_call(
        functools.partial(_ret_kernel, cdec=cdec),
        grid=(batch, nchunk),
        in_specs=[pl.BlockSpec((c, 2 * RET_QK_WIDTH), row),
                  pl.BlockSpec((c, RET_V_WIDTH), row),
                  pl.BlockSpec((c, RET_V_WIDTH), row),
                  pl.BlockSpec((1, RET_V_WIDTH), lambda b, n: (0, 0)),
                  pl.BlockSpec((RET_HEADS, c, c), const3),
                  pl.BlockSpec((RET_HEADS, c, RET_VAL_DIM), const3),
                  pl.BlockSpec((RET_HEADS, c, RET_KEY_DIM), const3)],
        out_specs=pl.BlockSpec((c, RET_V_WIDTH), row),
        out_shape=jax.ShapeDtypeStruct((t, RET_V_WIDTH), BF16),
        scratch_shapes=[pltpu.VMEM((RET_HEADS, RET_KEY_DIM, RET_VAL_DIM), F32)],
        compiler_params=_params("arbitrary", "arbitrary"),
        name="retention",
    )(rqk, rv, rg, norm_g.reshape(1, RET_V_WIDTH), inner, qdec, kdec)


def _ln_rows(z, g, b):
    mu = jnp.mean(z, axis=-1, keepdims=True)
    d = z - mu
    var = jnp.mean(d * d, axis=-1, keepdims=True)
    return d * lax.rsqrt(var + LN_EPS) * g + b


def _ln_router_kernel(z_ref, g_ref, b_ref, wr_ref, br_ref, x_ref, gate_ref, idx_ref):
    x = _ln_rows(z_ref[...], g_ref[...], b_ref[...])
    x_ref[...] = x
    logits = jnp.dot(x, wr_ref[...], preferred_element_type=F32,
                     precision=lax.Precision.HIGHEST) + br_ref[...]
    lane = lax.broadcasted_iota(jnp.int32, logits.shape, 1)
    lane_f = lane.astype(F32)
    vals, idxs = [], []
    cur = logits
    for _ in range(TOP_K):
        m = jnp.max(cur, axis=-1, keepdims=True)
        idx_f = jnp.min(jnp.where(cur == m, lane_f, float(ROUTER_PAD)), axis=-1, keepdims=True)
        idx = idx_f.astype(jnp.int32)
        vals.append(m)
        idxs.append(idx)
        cur = jnp.where(lane == idx, -jnp.inf, cur)
    es = [jnp.exp(v - vals[0]) for v in vals]
    inv = 1.0 / (es[0] + es[1] + es[2] + es[3])
    gates = jnp.zeros(logits.shape, F32)
    ids = jnp.zeros(logits.shape, jnp.int32)
    for k in range(TOP_K):
        gates = jnp.where(lane == k, es[k] * inv, gates)
        ids = jnp.where(lane == k, idxs[k], ids)
    gate_ref[...] = gates
    idx_ref[...] = ids


def _ln_router(z, g, b, w_router, b_router):
    t, d = z.shape
    tm = min(LN_TM, t)
    wr = jnp.zeros((d, ROUTER_PAD), F32).at[:, :N_EXPERTS].set(w_router.astype(F32))
    br = jnp.full((1, ROUTER_PAD), NEG_BIG, F32).at[0, :N_EXPERTS].set(b_router.astype(F32))
    row = lambda i: (i, 0)
    const = lambda i: (0, 0)
    return pl.pallas_call(
        _ln_router_kernel,
        grid=(t // tm,),
        in_specs=[pl.BlockSpec((tm, d), row),
                  pl.BlockSpec((1, d), const),
                  pl.BlockSpec((1, d), const),
                  pl.BlockSpec((d, ROUTER_PAD), const),
                  pl.BlockSpec((1, ROUTER_PAD), const)],
        out_specs=[pl.BlockSpec((tm, d), row),
                   pl.BlockSpec((tm, ROUTER_PAD), row),
                   pl.BlockSpec((tm, ROUTER_PAD), row)],
        out_shape=[jax.ShapeDtypeStruct((t, d), F32),
                   jax.ShapeDtypeStruct((t, ROUTER_PAD), F32),
                   jax.ShapeDtypeStruct((t, ROUTER_PAD), jnp.int32)],
        compiler_params=_params("arbitrary"),
        name="ln_router",
    )(z, g.reshape(1, d), b.reshape(1, d), wr, br)


def _combine_ln_kernel(x_ref, y0_ref, y1_ref, y2_ref, y3_ref, gate_ref, g_ref, b_ref, o_ref, ob_ref):
    gt = gate_ref[...]
    z = DEEPNORM_ALPHA * x_ref[...]
    for k, y_ref in enumerate((y0_ref, y1_ref, y2_ref, y3_ref)):
        z = z + gt[:, k:k + 1] * y_ref[...]
    out = _ln_rows(z, g_ref[...], b_ref[...])
    o_ref[...] = out
    ob_ref[...] = out.astype(ob_ref.dtype)


def _combine_ln(x1, yp, gates, g, b):
    t, d = x1.shape
    tm = min(COMBINE_TM, t)
    nrb = t // tm
    row = lambda i: (i, 0)
    const = lambda i: (0, 0)
    y_specs = [pl.BlockSpec((tm, d), functools.partial(lambda i, k: (k * nrb + i, 0), k=k))
               for k in range(TOP_K)]
    return pl.pallas_call(
        _combine_ln_kernel,
        grid=(nrb,),
        in_specs=[pl.BlockSpec((tm, d), row)] + y_specs +
                 [pl.BlockSpec((tm, ROUTER_PAD), row),
                  pl.BlockSpec((1, d), const),
                  pl.BlockSpec((1, d), const)],
        out_specs=[pl.BlockSpec((tm, d), row), pl.BlockSpec((tm, d), row)],
        out_shape=[jax.ShapeDtypeStruct((t, d), F32), jax.ShapeDtypeStruct((t, d), BF16)],
        compiler_params=_params("arbitrary"),
        name="combine_ln",
    )(x1, yp, yp, yp, yp, gates, g.reshape(1, d), b.reshape(1, d))


def _ln_kernel(z_ref, g_ref, b_ref, o_ref):
    o_ref[...] = _ln_rows(z_ref[...], g_ref[...], b_ref[...])


def _ln(z, g, b):
    t, d = z.shape
    tm = min(LN_TM, t)
    return pl.pallas_call(
        _ln_kernel,
        grid=(t // tm,),
        in_specs=[pl.BlockSpec((tm, d), lambda i: (i, 0)),
                  pl.BlockSpec((1, d), lambda i: (0, 0)),
                  pl.BlockSpec((1, d), lambda i: (0, 0))],
        out_specs=pl.BlockSpec((tm, d), lambda i: (i, 0)),
        out_shape=jax.ShapeDtypeStruct((t, d), F32),
        compiler_params=_params("arbitrary"),
        name="layer_norm",
    )(z, g.reshape(1, d), b.reshape(1, d))


def _moe_kernel(be_ref, nu_ref, idx_hbm, x_hbm, wg_ref, wu_ref, bg_ref, bu_ref, wd_ref, bd_ref,
                y_hbm, idx_smem, xf, xb, acc, ybuf, sem_idx, sem_g, sem_s, *, bm, nj, ntok):
    del be_ref
    i = pl.program_id(0)
    j = pl.program_id(1)
    nu = nu_ref[0]
    active = i < nu
    d = xf.shape[1]

    def load_idx(item, slot):
        cp = pltpu.make_async_copy(idx_hbm.at[item], idx_smem.at[slot], sem_idx)
        cp.start()
        cp.wait()

    def row_dst(slot, r):
        return idx_smem[slot, lax.div(r, LANES), lax.rem(r, LANES)]

    def start_gather(slot):
        def body(r, c):
            tok = lax.rem(row_dst(slot, r), ntok)
            pltpu.make_async_copy(x_hbm.at[pl.ds(tok, 1), :], xf.at[pl.ds(r, 1), :], sem_g).start()
            return c
        lax.fori_loop(0, bm, body, 0)

    def wait_gather():
        pltpu.make_async_copy(x_hbm.at[pl.ds(0, bm), :], xf, sem_g).wait()

    def start_scatter(slot):
        def body(r, c):
            dst = row_dst(slot, r)
            pltpu.make_async_copy(ybuf.at[pl.ds(r, 1), :], y_hbm.at[pl.ds(dst, 1), :], sem_s).start()
            return c
        lax.fori_loop(0, bm, body, 0)

    def wait_scatter():
        pltpu.make_async_copy(ybuf, y_hbm.at[pl.ds(0, bm), :], sem_s).wait()

    @pl.when(active & (j == 0))
    def _():
        @pl.when(i == 0)
        def _():
            load_idx(0, 0)
            start_gather(0)
            ybuf[...] = jnp.zeros_like(ybuf)
            pad = pltpu.make_async_copy(ybuf, y_hbm.at[pl.ds(TOP_K * ntok, bm), :], sem_s)
            pad.start()
            pad.wait()
        wait_gather()
        xb[...] = xf[...].astype(BF16)

        @pl.when(i + 1 < nu)
        def _():
            nxt = lax.rem(i + 1, 2)
            load_idx(i + 1, nxt)
            start_gather(nxt)

    @pl.when(active)
    def _():
        x = xb[...]
        g = jnp.dot(x, wg_ref[0], preferred_element_type=F32) + bg_ref[0]
        u = jnp.dot(x, wu_ref[0], preferred_element_type=F32) + bu_ref[0]
        g = jnp.minimum(g, SWIGLU_LIMIT)
        u = jnp.clip(u, -SWIGLU_LIMIT, SWIGLU_LIMIT)
        a = (g * _sigmoid(SWIGLU_ALPHA * g) * (u + 1.0)).astype(BF16)
        for c in range(d // MOE_DC):
            cs = slice(c * MOE_DC, (c + 1) * MOE_DC)
            part = jnp.dot(a, wd_ref[0, :, cs], preferred_element_type=F32)

            @pl.when(j == 0)
            def _():
                acc[:, cs] = part

            @pl.when(j > 0)
            def _():
                acc[:, cs] = acc[:, cs] + part

    @pl.when(active & (j == nj - 1))
    def _():
        @pl.when(i > 0)
        def _():
            wait_scatter()
        ybuf[...] = acc[...] + bd_ref[0]
        start_scatter(lax.rem(i, 2))

        @pl.when(i == nu - 1)
        def _():
            wait_scatter()


def _moe(x1, row_dst, block_expert, n_used, wgu, bgu, wd, bd):
    t, d = x1.shape
    bm, tf = MOE_BM, MOE_TF
    n_items = row_dst.shape[0] // bm
    nj = EXPERT_FF // tf
    idx3 = row_dst.reshape(n_items, bm // LANES, LANES)
    bgu3 = bgu.reshape(N_EXPERTS, 1, 2 * EXPERT_FF)
    bd3 = bd.reshape(N_EXPERTS, 1, d)
    last = nj - 1

    def chunk(i, j, nu):
        return jnp.where(i < nu[0], j, last)

    grid_spec = pltpu.PrefetchScalarGridSpec(
        num_scalar_prefetch=2,
        grid=(n_items, nj),
        in_specs=[
            pl.BlockSpec(memory_space=pl.ANY),
            pl.BlockSpec(memory_space=pl.ANY),
            pl.BlockSpec((1, d, tf), lambda i, j, be, nu: (be[i], 0, chunk(i, j, nu))),
            pl.BlockSpec((1, d, tf), lambda i, j, be, nu: (be[i], 0, chunk(i, j, nu) + nj)),
            pl.BlockSpec((1, 1, tf), lambda i, j, be, nu: (be[i], 0, chunk(i, j, nu))),
            pl.BlockSpec((1, 1, tf), lambda i, j, be, nu: (be[i], 0, chunk(i, j, nu) + nj)),
            pl.BlockSpec((1, tf, d), lambda i, j, be, nu: (be[i], chunk(i, j, nu), 0)),
            pl.BlockSpec((1, 1, d), lambda i, j, be, nu: (be[i], 0, 0)),
        ],
        out_specs=pl.BlockSpec(memory_space=pl.ANY),
        scratch_shapes=[
            pltpu.SMEM((2, bm // LANES, LANES), jnp.int32),
            pltpu.VMEM((bm, d), F32),
            pltpu.VMEM((bm, d), BF16),
            pltpu.VMEM((bm, d), F32),
            pltpu.VMEM((bm, d), F32),
            pltpu.SemaphoreType.DMA(()),
            pltpu.SemaphoreType.DMA(()),
            pltpu.SemaphoreType.DMA(()),
        ],
    )
    return pl.pallas_call(
        functools.partial(_moe_kernel, bm=bm, nj=nj, ntok=t),
        grid_spec=grid_spec,
        out_shape=jax.ShapeDtypeStruct((TOP_K * t + bm, d), F32),
        compiler_params=_params("arbitrary", "arbitrary"),
        name="moe_experts",
    )(block_expert, n_used, idx3, x1, wgu, wgu, bgu3, bgu3, wd, bd3)


def _routing_tables(top_idx, t):
    bm = MOE_BM
    tk = t * TOP_K
    n_items = tk // bm + N_EXPERTS
    n_rows = n_items * bm
    flat_e = top_idx.T.reshape(tk)
    order = jnp.argsort(flat_e, stable=True).astype(jnp.int32)
    sorted_e = flat_e[order]
    counts = jnp.zeros((N_EXPERTS,), jnp.int32).at[flat_e].add(1)
    padded = (counts + bm - 1) // bm * bm
    group_start = jnp.cumsum(counts) - counts
    padded_end = jnp.cumsum(padded)
    padded_start = padded_end - padded
    dest = padded_start[sorted_e] + jnp.arange(tk, dtype=jnp.int32) - group_start[sorted_e]
    pad_rows = tk + (jnp.arange(n_rows, dtype=jnp.int32) % bm)
    row_dst = pad_rows.at[dest].set(order)
    n_used = (padded_end[-1] // bm).astype(jnp.int32)
    item_start = jnp.arange(n_items, dtype=jnp.int32) * bm
    item_start = jnp.minimum(item_start, (n_used - 1) * bm)
    block_expert = jnp.minimum(jnp.searchsorted(padded_end, item_start, side="right"),
                               N_EXPERTS - 1).astype(jnp.int32)
    return row_dst, block_expert, n_used.reshape(1)


def kernel(x, p, w_in, attn_sinks, ret_norm_g, w_att_out, w_ret_out, w_out, ln1_g, ln1_b,
           w_router, b_router, w_gate_up, b_gate_up, w_down, b_down, ln2_g, ln2_b,
           w_ple, w_ple_gate, ln3_g, ln3_b):
    bsz, seq, d = x.shape
    t = bsz * seq
    assert seq % ATT_BLOCK == 0 and seq % RET_CHUNK == 0
    xf = x.reshape(t, d)
    for i in range(DEPTH):
        xb = xf.astype(BF16)
        w_in_b = w_in[i].astype(BF16)
        off = 0
        segs = {}
        for name, width in (("aq", ATT_Q_WIDTH), ("akv", 2 * ATT_KV_WIDTH), ("rqk", 2 * RET_QK_WIDTH),
                            ("rv", RET_V_WIDTH), ("rg", RET_V_WIDTH), ("ga", d), ("gr", d)):
            segs[name] = _proj(xb, w_in_b, off, width, "proj_" + name)
            off += width
        y_att = _attention(segs["aq"], segs["akv"], attn_sinks[i].astype(F32), seq // ATT_BLOCK)
        y_ret = _retention(segs["rqk"], segs["rv"], segs["rg"], ret_norm_g[i].astype(F32),
                           bsz, seq // RET_CHUNK)
        merged = _merge(y_att, y_ret, w_att_out[i].astype(BF16), w_ret_out[i].astype(BF16),
                        segs["ga"], segs["gr"])
        z1 = _resid_mm(merged, w_out[i].astype(BF16), xf)
        x1, gates, top_idx = _ln_router(z1, ln1_g[i], ln1_b[i], w_router[i], b_router[i])
        row_dst, block_expert, n_used = _routing_tables(top_idx[:, :TOP_K], t)
        yp = _moe(x1, row_dst, block_expert, n_used, w_gate_up[i].astype(BF16),
                  b_gate_up[i].astype(F32), w_down[i].astype(BF16), b_down[i].astype(F32))
        x2, x2b = _combine_ln(x1, yp, gates, ln2_g[i], ln2_b[i])
        z3 = _ple(x2b, w_ple_gate[i].astype(BF16), p[i].reshape(t, -1), w_ple[i].astype(BF16), x2)
        xf = _ln(z3, ln3_g[i], ln3_b[i])
    return xf.reshape(bsz, seq, d)
```

```python
import functools
import math

import jax
import jax.numpy as jnp
import numpy as np
from jax import lax
from jax.experimental import pallas as pl
from jax.experimental.pallas import tpu as pltpu

F32 = jnp.float32
BF16 = jnp.bfloat16

ATT_HEADS = 32
ATT_KV_HEADS = 4
ATT_GROUP = ATT_HEADS // ATT_KV_HEADS
ATT_HEAD_DIM = 64
ATT_BLOCK = 128
RET_HEADS = 8
RET_KEY_DIM = 128
RET_VAL_DIM = 256
RET_CHUNK = 128
N_EXPERTS = 32
TOP_K = 4
EXPERT_FF = 1536
SWIGLU_LIMIT = 7.0
SWIGLU_ALPHA = 1.702
LN_EPS = 1e-5
DEPTH = 1
DEEPNORM_ALPHA = float((2 * DEPTH) ** 0.25)

ATT_Q_WIDTH = ATT_HEADS * ATT_HEAD_DIM
ATT_KV_WIDTH = ATT_KV_HEADS * ATT_HEAD_DIM
RET_QK_WIDTH = RET_HEADS * RET_KEY_DIM
RET_V_WIDTH = RET_HEADS * RET_VAL_DIM

LANES = 128
VMEM_LIMIT_BYTES = 56 * 1024 * 1024
MM_TM = 512
MM_TN = 512
LN_TM = 256
COMBINE_TM = 128
MOE_BM = 512
MOE_TF = 512
MOE_TD = 2048
MOE_DC = 512
ROUTER_PAD = LANES
NEG_BIG = -1e30
SLAB_PITCH = 24
HI_MASK = 0xFFFF0000


def _params(*sem):
    return pltpu.CompilerParams(dimension_semantics=sem, vmem_limit_bytes=VMEM_LIMIT_BYTES)


def _sigmoid(x):
    return 1.0 / (1.0 + jnp.exp(-x))


def _bf16_bits(x):
    return lax.bitcast_convert_type(x.astype(BF16).astype(F32), jnp.uint32)


def _pack_words(lo, hi):
    return (_bf16_bits(lo) >> 16) | _bf16_bits(hi)


def _unpack_lo(w):
    return lax.bitcast_convert_type(w << 16, F32)


def _unpack_hi(w):
    return lax.bitcast_convert_type(w & jnp.uint32(HI_MASK), F32)


def _proj_kernel(x_ref, w_ref, o_ref):
    o_ref[...] = jnp.dot(x_ref[...], w_ref[...], preferred_element_type=F32).astype(o_ref.dtype)


def _proj(xb, wb, col_off, ncols, name):
    m, k = xb.shape
    tm, tn = min(MM_TM, m), MM_TN
    assert m % tm == 0 and ncols % tn == 0 and col_off % tn == 0
    off = col_off // tn
    return pl.pallas_call(
        _proj_kernel,
        grid=(ncols // tn, m // tm),
        in_specs=[pl.BlockSpec((tm, k), lambda j, i: (i, 0)),
                  pl.BlockSpec((k, tn), lambda j, i: (0, j + off))],
        out_specs=pl.BlockSpec((tm, tn), lambda j, i: (i, j)),
        out_shape=jax.ShapeDtypeStruct((m, ncols), BF16),
        compiler_params=_params("arbitrary", "arbitrary"),
        name=name,
    )(xb, wb)


def _merge_kernel(ya_ref, yr_ref, wa_ref, wr_ref, ga_ref, gr_ref, o_ref):
    a = jnp.dot(ya_ref[...], wa_ref[...], preferred_element_type=F32)
    r = jnp.dot(yr_ref[...], wr_ref[...], preferred_element_type=F32)
    ga = _sigmoid(ga_ref[...].astype(F32))
    gr = _sigmoid(gr_ref[...].astype(F32))
    o_ref[...] = (ga * a + gr * r).astype(o_ref.dtype)


def _merge(ya, yr, wa, wr, ga, gr):
    m, ka = ya.shape
    kr = yr.shape[1]
    n = wa.shape[1]
    tm, tn = min(MM_TM, m), MM_TN
    return pl.pallas_call(
        _merge_kernel,
        grid=(n // tn, m // tm),
        in_specs=[pl.BlockSpec((tm, ka), lambda j, i: (i, 0)),
                  pl.BlockSpec((tm, kr), lambda j, i: (i, 0)),
                  pl.BlockSpec((ka, tn), lambda j, i: (0, j)),
                  pl.BlockSpec((kr, tn), lambda j, i: (0, j)),
                  pl.BlockSpec((tm, tn), lambda j, i: (i, j)),
                  pl.BlockSpec((tm, tn), lambda j, i: (i, j))],
        out_specs=pl.BlockSpec((tm, tn), lambda j, i: (i, j)),
        out_shape=jax.ShapeDtypeStruct((m, n), BF16),
        compiler_params=_params("arbitrary", "arbitrary"),
        name="merge",
    )(ya, yr, wa, wr, ga, gr)


def _resid_mm_kernel(a_ref, w_ref, x_ref, o_ref):
    acc = jnp.dot(a_ref[...], w_ref[...], preferred_element_type=F32)
    o_ref[...] = DEEPNORM_ALPHA * x_ref[...] + acc


def _resid_mm(a, w, x):
    m, k = a.shape
    n = w.shape[1]
    tm, tn = min(MM_TM, m), MM_TN
    return pl.pallas_call(
        _resid_mm_kernel,
        grid=(n // tn, m // tm),
        in_specs=[pl.BlockSpec((tm, k), lambda j, i: (i, 0)),
                  pl.BlockSpec((k, tn), lambda j, i: (0, j)),
                  pl.BlockSpec((tm, tn), lambda j, i: (i, j))],
        out_specs=pl.BlockSpec((tm, tn), lambda j, i: (i, j)),
        out_shape=jax.ShapeDtypeStruct((m, n), F32),
        compiler_params=_params("arbitrary", "arbitrary"),
        name="resid_mm",
    )(a, w, x)


def _ple_kernel(xb_ref, wg_ref, p_ref, wp_ref, x_ref, o_ref):
    gate = jnp.dot(xb_ref[...], wg_ref[...], preferred_element_type=F32)
    emb = jnp.dot(p_ref[...].astype(BF16), wp_ref[...], preferred_element_type=F32)
    o_ref[...] = DEEPNORM_ALPHA * x_ref[...] + emb * _sigmoid(gate)


def _ple(xb, wg, p, wp, x):
    m, k = xb.shape
    n = wg.shape[1]
    kp = p.shape[1]
    tm, tn = min(MM_TM, m), MM_TN
    return pl.pallas_call(
        _ple_kernel,
        grid=(n // tn, m // tm),
        in_specs=[pl.BlockSpec((tm, k), lambda j, i: (i, 0)),
                  pl.BlockSpec((k, tn), lambda j, i: (0, j)),
                  pl.BlockSpec((tm, kp), lambda j, i: (i, 0)),
                  pl.BlockSpec((kp, tn), lambda j, i: (0, j)),
                  pl.BlockSpec((tm, tn), lambda j, i: (i, j))],
        out_specs=pl.BlockSpec((tm, tn), lambda j, i: (i, j)),
        out_shape=jax.ShapeDtypeStruct((m, n), F32),
        compiler_params=_params("arbitrary", "arbitrary"),
        name="ple",
    )(xb, wg, p, wp, x)


def _alibi_slope(h):
    return float(2.0 ** (-8.0 * (h + 1) / ATT_HEADS))


def _attn_kernel(sink_ref, q_ref, kvc_ref, kvp_ref, o_ref, *, nblk):
    blk = ATT_BLOCK
    n = lax.rem(pl.program_id(0), nblk)
    has_prev = n > 0
    row = lax.broadcasted_iota(jnp.int32, (blk, 2 * blk), 0)
    col = lax.broadcasted_iota(jnp.int32, (blk, 2 * blk), 1)
    dist = blk + row - col
    valid = (dist >= 0) & (dist < blk) & ((col >= blk) | has_prev)
    distf = dist.astype(F32)
    lane = lax.broadcasted_iota(jnp.int32, (2 * blk, LANES), 1)
    lo = lane < ATT_HEAD_DIM
    scale = ATT_HEAD_DIM ** -0.5

    def halves(t, half):
        swapped = jnp.concatenate([t[:, ATT_HEAD_DIM:], t[:, :ATT_HEAD_DIM]], axis=1)
        zero = jnp.zeros_like(t)
        if half == 0:
            return jnp.where(lo, t, zero), jnp.where(lo, zero, swapped)
        return jnp.where(lo, swapped, zero), jnp.where(lo, zero, t)

    for kh in range(ATT_KV_HEADS):
        tile, half = kh // 2, kh % 2
        ks = slice(tile * LANES, (tile + 1) * LANES)
        vs = slice(ATT_KV_WIDTH + tile * LANES, ATT_KV_WIDTH + (tile + 1) * LANES)
        kcat = jnp.concatenate([kvp_ref[:, ks], kvc_ref[:, ks]], axis=0)
        vcat = jnp.concatenate([kvp_ref[:, vs], kvc_ref[:, vs]], axis=0)
        k_lo, k_hi = halves(kcat, half)
        v_lo, v_hi = halves(vcat, half)
        for jp in range(ATT_GROUP // 2):
            h0 = kh * ATT_GROUP + 2 * jp
            qs = slice(h0 * ATT_HEAD_DIM, h0 * ATT_HEAD_DIM + LANES)
            qp = q_ref[:, qs]
            acc = jnp.zeros((blk, LANES), F32)
            for par, (kx, vx) in enumerate(((k_lo, v_lo), (k_hi, v_hi))):
                h = h0 + par
                s = lax.dot_general(qp, kx, (((1,), (1,)), ((), ())), preferred_element_type=F32)
                s = s * scale - _alibi_slope(h) * distf
                s = jnp.where(valid, s, -jnp.inf)
                sink = sink_ref[h]
                m = jnp.maximum(jnp.max(s, axis=-1, keepdims=True), sink)
                e = jnp.exp(s - m)
                denom = jnp.sum(e, axis=-1, keepdims=True) + jnp.exp(sink - m)
                pv = jnp.dot(e.astype(BF16), vx, preferred_element_type=F32)
                acc = acc + pv * (1.0 / denom)
            o_ref[:, qs] = acc.astype(o_ref.dtype)


def _attention(aq, akv, sinks, nblk):
    t = aq.shape[0]
    blk = ATT_BLOCK
    return pl.pallas_call(
        functools.partial(_attn_kernel, nblk=nblk),
        grid=(t // blk,),
        in_specs=[pl.BlockSpec(memory_space=pltpu.SMEM),
                  pl.BlockSpec((blk, ATT_Q_WIDTH), lambda i: (i, 0)),
                  pl.BlockSpec((blk, 2 * ATT_KV_WIDTH), lambda i: (i, 0)),
                  pl.BlockSpec((blk, 2 * ATT_KV_WIDTH), lambda i: (jnp.maximum(i - 1, 0), 0))],
        out_specs=pl.BlockSpec((blk, ATT_Q_WIDTH), lambda i: (i, 0)),
        out_shape=jax.ShapeDtypeStruct((t, ATT_Q_WIDTH), BF16),
        compiler_params=_params("arbitrary"),
        name="swa_attention",
    )(sinks, aq, akv, akv)


def _ret_constants():
    c = RET_CHUNK
    gam = 1.0 - 2.0 ** (-5.0 - np.arange(RET_HEADS, dtype=np.float64))
    lg = np.log(gam)
    pos = np.arange(c, dtype=np.float64)
    diff = pos[:, None] - pos[None, :]
    kscale = RET_KEY_DIM ** -0.5
    inner = np.where(diff >= 0, np.exp(np.maximum(diff, 0.0) * lg[:, None, None]), 0.0) * kscale
    qdec = np.exp((pos + 1.0) * lg[:, None])
    kdec = np.exp((c - 1.0 - pos) * lg[:, None]) * kscale
    cdec = np.exp(c * lg)
    qdec_b = np.broadcast_to(qdec[:, :, None], (RET_HEADS, c, RET_VAL_DIM))
    kdec_b = np.broadcast_to(kdec[:, :, None], (RET_HEADS, c, RET_KEY_DIM))
    return (jnp.asarray(inner, F32), jnp.asarray(qdec_b, F32), jnp.asarray(kdec_b, F32),
            [float(v) for v in cdec])


def _ret_kernel(qk_ref, v_ref, g_ref, ng_ref, inner_ref, qdec_ref, kdec_ref, o_ref, state_ref, *, cdec):
    @pl.when(pl.program_id(1) == 0)
    def _():
        state_ref[...] = jnp.zeros_like(state_ref)

    dk, dv = RET_KEY_DIM, RET_VAL_DIM
    for h in range(RET_HEADS):
        q = qk_ref[:, h * dk:(h + 1) * dk]
        k = qk_ref[:, RET_QK_WIDTH + h * dk:RET_QK_WIDTH + (h + 1) * dk]
        v = v_ref[:, h * dv:(h + 1) * dv]
        st = state_ref[h]
        qk = lax.dot_general(q, k, (((1,), (1,)), ((), ())), preferred_element_type=F32)
        a = (qk * inner_ref[h]).astype(BF16)
        inner = jnp.dot(a, v, preferred_element_type=F32)
        cross = jnp.dot(q, st.astype(BF16), preferred_element_type=F32) * qdec_ref[h]
        kd = (k.astype(F32) * kdec_ref[h]).astype(BF16)
        upd = lax.dot_general(kd, v, (((0,), (0,)), ((), ())), preferred_element_type=F32)
        state_ref[h] = st * cdec[h] + upd
        o = inner + cross
        mu = jnp.mean(o, axis=-1, keepdims=True)
        d = o - mu
        var = jnp.mean(d * d, axis=-1, keepdims=True)
        y = d * lax.rsqrt(var + LN_EPS) * ng_ref[:, h * dv:(h + 1) * dv]
        g = g_ref[:, h * dv:(h + 1) * dv].astype(F32)
        o_ref[:, h * dv:(h + 1) * dv] = (g * _sigmoid(g) * y).astype(o_ref.dtype)


def _retention(rqk, rv, rg, norm_g, batch, nchunk):
    t = rqk.shape[0]
    c = RET_CHUNK
    inner, qdec, kdec, cdec = _ret_constants()
    row = lambda b, n: (b * nchunk + n, 0)
    const3 = lambda b, n: (0, 0, 0)
    return pl.pallas_call(
        functools.partial(_ret_kernel, cdec=cdec),
        grid=(batch, nchunk),
        in_specs=[pl.BlockSpec((c, 2 * RET_QK_WIDTH), row),
                  pl.BlockSpec((c, RET_V_WIDTH), row),
                  pl.BlockSpec((c, RET_V_WIDTH), row),
                  pl.BlockSpec((1, RET_V_WIDTH), lambda b, n: (0, 0)),
                  pl.BlockSpec((RET_HEADS, c, c), const3),
                  pl.BlockSpec((RET_HEADS, c, RET_VAL_DIM), const3),
                  pl.BlockSpec((RET_HEADS, c, RET_KEY_DIM), const3)],
        out_specs=pl.BlockSpec((c, RET_V_WIDTH), row),
        out_shape=jax.ShapeDtypeStruct((t, RET_V_WIDTH), BF16),
        scratch_shapes=[pltpu.VMEM((RET_HEADS, RET_KEY_DIM, RET_VAL_DIM), F32)],
        compiler_params=_params("arbitrary", "arbitrary"),
        name="retention",
    )(rqk, rv, rg, norm_g.reshape(1, RET_V_WIDTH), inner, qdec, kdec)


def _ln_rows(z, g, b):
    mu = jnp.mean(z, axis=-1, keepdims=True)
    d = z - mu
    var = jnp.mean(d * d, axis=-1, keepdims=True)
    return d * lax.rsqrt(var + LN_EPS) * g + b


def _ln_router_kernel(z_ref, g_ref, b_ref, wr_ref, br_ref, x_ref, xp_ref, gate_ref, idx_ref):
    x = _ln_rows(z_ref[...], g_ref[...], b_ref[...])
    x_ref[...] = x
    tm, d = x.shape
    half = d // 2
    nsl = half // LANES
    for s in range(nsl):
        cs = slice(s * LANES, (s + 1) * LANES)
        hs = slice(half + s * LANES, half + (s + 1) * LANES)
        xp_ref[pl.ds(s, tm, stride=nsl), :] = _pack_words(x[:, cs], x[:, hs])
    logits = jnp.dot(x, wr_ref[...], preferred_element_type=F32,
                     precision=lax.Precision.HIGHEST) + br_ref[...]
    lane = lax.broadcasted_iota(jnp.int32, logits.shape, 1)
    lane_f = lane.astype(F32)
    vals, idxs = [], []
    cur = logits
    for _ in range(TOP_K):
        m = jnp.max(cur, axis=-1, keepdims=True)
        idx_f = jnp.min(jnp.where(cur == m, lane_f, float(ROUTER_PAD)), axis=-1, keepdims=True)
        idx = idx_f.astype(jnp.int32)
        vals.append(m)
        idxs.append(idx)
        cur = jnp.where(lane == idx, -jnp.inf, cur)
    es = [jnp.exp(v - vals[0]) for v in vals]
    inv = 1.0 / (es[0] + es[1] + es[2] + es[3])
    gates = jnp.zeros(logits.shape, F32)
    ids = jnp.zeros(logits.shape, jnp.int32)
    for k in range(TOP_K):
        gates = jnp.where(lane == k, es[k] * inv, gates)
        ids = jnp.where(lane == k, idxs[k], ids)
    gate_ref[...] = gates
    idx_ref[...] = ids


def _ln_router(z, g, b, w_router, b_router):
    t, d = z.shape
    tm = min(LN_TM, t)
    wr = jnp.zeros((d, ROUTER_PAD), F32).at[:, :N_EXPERTS].set(w_router.astype(F32))
    br = jnp.full((1, ROUTER_PAD), NEG_BIG, F32).at[0, :N_EXPERTS].set(b_router.astype(F32))
    row = lambda i: (i, 0)
    const = lambda i: (0, 0)
    nsl = d // 2 // LANES
    return pl.pallas_call(
        _ln_router_kernel,
        grid=(t // tm,),
        in_specs=[pl.BlockSpec((tm, d), row),
                  pl.BlockSpec((1, d), const),
                  pl.BlockSpec((1, d), const),
                  pl.BlockSpec((d, ROUTER_PAD), const),
                  pl.BlockSpec((1, ROUTER_PAD), const)],
        out_specs=[pl.BlockSpec((tm, d), row),
                   pl.BlockSpec((tm * nsl, LANES), row),
                   pl.BlockSpec((tm, ROUTER_PAD), row),
                   pl.BlockSpec((tm, ROUTER_PAD), row)],
        out_shape=[jax.ShapeDtypeStruct((t, d), F32),
                   jax.ShapeDtypeStruct((t * nsl, LANES), jnp.uint32),
                   jax.ShapeDtypeStruct((t, ROUTER_PAD), F32),
                   jax.ShapeDtypeStruct((t, ROUTER_PAD), jnp.int32)],
        compiler_params=_params("arbitrary"),
        name="ln_router",
    )(z, g.reshape(1, d), b.reshape(1, d), wr, br)


def _combine_ln_kernel(x_ref, y0_ref, y1_ref, y2_ref, y3_ref, gate_ref, g_ref, b_ref, o_ref, ob_ref):
    tm, d = x_ref.shape
    nsl = d // 2 // LANES
    gt = gate_ref[...]
    gk = [jnp.broadcast_to(gt[:, k:k + 1], (tm, LANES)) for k in range(TOP_K)]
    los, his = [], []
    for s in range(nsl):
        lo = hi = None
        for k, y_ref in enumerate((y0_ref, y1_ref, y2_ref, y3_ref)):
            w = y_ref[pl.ds(s, tm, stride=nsl), :]
            tl, th = gk[k] * _unpack_lo(w), gk[k] * _unpack_hi(w)
            lo = tl if lo is None else lo + tl
            hi = th if hi is None else hi + th
        los.append(lo)
        his.append(hi)
    z = DEEPNORM_ALPHA * x_ref[...] + jnp.concatenate(los + his, axis=1)
    out = _ln_rows(z, g_ref[...], b_ref[...])
    o_ref[...] = out
    ob_ref[...] = out.astype(ob_ref.dtype)


def _combine_ln(x1, yp, gates, g, b):
    t, d = x1.shape
    tm = min(COMBINE_TM, t)
    nrb = t // tm
    nsl = d // 2 // LANES
    row = lambda i: (i, 0)
    const = lambda i: (0, 0)
    y_specs = [pl.BlockSpec((tm * nsl, LANES), functools.partial(lambda i, k: (k * nrb + i, 0), k=k))
               for k in range(TOP_K)]
    return pl.pallas_call(
        _combine_ln_kernel,
        grid=(nrb,),
        in_specs=[pl.BlockSpec((tm, d), row)] + y_specs +
                 [pl.BlockSpec((tm, ROUTER_PAD), row),
                  pl.BlockSpec((1, d), const),
                  pl.BlockSpec((1, d), const)],
        out_specs=[pl.BlockSpec((tm, d), row), pl.BlockSpec((tm, d), row)],
        out_shape=[jax.ShapeDtypeStruct((t, d), F32), jax.ShapeDtypeStruct((t, d), BF16)],
        compiler_params=_params("arbitrary"),
        name="combine_ln",
    )(x1, yp, yp, yp, yp, gates, g.reshape(1, d), b.reshape(1, d))


def _ln_kernel(z_ref, g_ref, b_ref, o_ref):
    o_ref[...] = _ln_rows(z_ref[...], g_ref[...], b_ref[...])


def _ln(z, g, b):
    t, d = z.shape
    tm = min(LN_TM, t)
    return pl.pallas_call(
        _ln_kernel,
        grid=(t // tm,),
        in_specs=[pl.BlockSpec((tm, d), lambda i: (i, 0)),
                  pl.BlockSpec((1, d), lambda i: (0, 0)),
                  pl.BlockSpec((1, d), lambda i: (0, 0))],
        out_specs=pl.BlockSpec((tm, d), lambda i: (i, 0)),
        out_shape=jax.ShapeDtypeStruct((t, d), F32),
        compiler_params=_params("arbitrary"),
        name="layer_norm",
    )(z, g.reshape(1, d), b.reshape(1, d))


def _moe_kernel(be_ref, nu_ref, idx_hbm, x_hbm, wg_ref, wu_ref, bg_ref, bu_ref, wd_ref, bd_ref,
                y_hbm, idx_smem, xg, xb, act, ybuf, sem_idx, sem_g, sem_s, *, bm, ngu, ntok):
    del be_ref
    i = pl.program_id(0)
    j = pl.program_id(1)
    nu = nu_ref[0]
    active = i < nu
    d = xb.shape[1]
    half = d // 2
    nsl = half // LANES
    td = wd_ref.shape[2]

    def load_idx(item, slot):
        cp = pltpu.make_async_copy(idx_hbm.at[item], idx_smem.at[slot], sem_idx)
        cp.start()
        cp.wait()

    def for_each_row(slot, fn):
        for q in range(bm // LANES):
            def body(rr, c, q=q):
                fn(idx_smem[slot, q, rr], q * LANES + rr)
                return c
            lax.fori_loop(0, LANES, body, 0, unroll=8)

    def vmem_slab(buf, r):
        return buf.at[pl.ds(pl.multiple_of(r * SLAB_PITCH, 8), nsl), :]

    def hbm_slab(ref, row):
        return ref.at[pl.ds(pl.multiple_of(row * nsl, 8), nsl), :]

    def start_gather(slot):
        def one(dst, r):
            pltpu.make_async_copy(hbm_slab(x_hbm, lax.rem(dst, ntok)), vmem_slab(xg, r), sem_g).start()
        for_each_row(slot, one)

    def wait_gather():
        pltpu.make_async_copy(x_hbm.at[pl.ds(0, bm * nsl), :], xg.at[pl.ds(0, bm * nsl), :], sem_g).wait()

    def start_scatter(slot):
        def one(dst, r):
            pltpu.make_async_copy(vmem_slab(ybuf, r), hbm_slab(y_hbm, dst), sem_s).start()
        for_each_row(slot, one)

    def wait_scatter():
        pltpu.make_async_copy(ybuf.at[pl.ds(0, bm * nsl), :], y_hbm.at[pl.ds(0, bm * nsl), :], sem_s).wait()

    @pl.when(active & (j == 0))
    def _():
        @pl.when(i == 0)
        def _():
            load_idx(0, 0)
            start_gather(0)
            ybuf[...] = jnp.zeros_like(ybuf)
            pad = pltpu.make_async_copy(ybuf.at[pl.ds(0, bm * nsl), :],
                                        y_hbm.at[pl.ds(TOP_K * ntok * nsl, bm * nsl), :], sem_s)
            pad.start()
            pad.wait()
        wait_gather()
        for s in range(nsl):
            w = xg[pl.ds(s, bm, stride=SLAB_PITCH), :]
            xb[:, s * LANES:(s + 1) * LANES] = _unpack_lo(w).astype(BF16)
            xb[:, half + s * LANES:half + (s + 1) * LANES] = _unpack_hi(w).astype(BF16)

        @pl.when(i + 1 < nu)
        def _():
            nxt = lax.rem(i + 1, 2)
            load_idx(i + 1, nxt)
            start_gather(nxt)

    @pl.when(active & (j < ngu))
    def _():
        x = xb[...]
        g = jnp.dot(x, wg_ref[0], preferred_element_type=F32) + bg_ref[0]
        u = jnp.dot(x, wu_ref[0], preferred_element_type=F32) + bu_ref[0]
        g = jnp.minimum(g, SWIGLU_LIMIT)
        u = jnp.clip(u, -SWIGLU_LIMIT, SWIGLU_LIMIT)
        act[j] = (g * _sigmoid(SWIGLU_ALPHA * g) * (u + 1.0)).astype(BF16)

    def down(is_hi):
        a = jnp.concatenate([act[q] for q in range(ngu)], axis=1)
        for c in range(td // MOE_DC):
            cs = slice(c * MOE_DC, (c + 1) * MOE_DC)
            y = jnp.dot(a, wd_ref[0, :, cs], preferred_element_type=F32) + bd_ref[0, :, cs]
            bits = _bf16_bits(y)
            for q in range(MOE_DC // LANES):
                s = c * (MOE_DC // LANES) + q
                piece = bits[:, q * LANES:(q + 1) * LANES]
                rows = pl.ds(s, bm, stride=SLAB_PITCH)
                if is_hi:
                    ybuf[rows, :] = ybuf[rows, :] | piece
                else:
                    ybuf[rows, :] = piece >> 16

    @pl.when(active & (j == ngu))
    def _():
        @pl.when(i > 0)
        def _():
            wait_scatter()
        down(False)

    @pl.when(active & (j == ngu + 1))
    def _():
        down(True)
        start_scatter(lax.rem(i, 2))

        @pl.when(i == nu - 1)
        def _():
            wait_scatter()


def _moe(xp, row_dst, block_expert, n_used, wgu, bgu, wd, bd, t, d):
    bm, tf, td = MOE_BM, MOE_TF, MOE_TD
    half = d // 2
    nsl = half // LANES
    assert td == half and td % MOE_DC == 0 and EXPERT_FF % tf == 0
    assert nsl % 8 == 0 and SLAB_PITCH % 8 == 0 and SLAB_PITCH >= nsl
    n_items = row_dst.shape[0] // bm
    ngu = EXPERT_FF // tf
    nj = ngu + d // td
    idx3 = row_dst.reshape(n_items, bm // LANES, LANES)
    bgu3 = bgu.reshape(N_EXPERTS, 1, 2 * EXPERT_FF)
    bd3 = bd.reshape(N_EXPERTS, 1, d)

    def step(i, j, nu):
        return jnp.where(i < nu[0], j, nj - 1)

    def gu(i, j, nu):
        return jnp.minimum(step(i, j, nu), ngu - 1)

    def dn(i, j, nu):
        return jnp.maximum(step(i, j, nu) - ngu, 0)

    grid_spec = pltpu.PrefetchScalarGridSpec(
        num_scalar_prefetch=2,
        grid=(n_items, nj),
        in_specs=[
            pl.BlockSpec(memory_space=pl.ANY),
            pl.BlockSpec(memory_space=pl.ANY),
            pl.BlockSpec((1, d, tf), lambda i, j, be, nu: (be[i], 0, gu(i, j, nu))),
            pl.BlockSpec((1, d, tf), lambda i, j, be, nu: (be[i], 0, gu(i, j, nu) + ngu)),
            pl.BlockSpec((1, 1, tf), lambda i, j, be, nu: (be[i], 0, gu(i, j, nu))),
            pl.BlockSpec((1, 1, tf), lambda i, j, be, nu: (be[i], 0, gu(i, j, nu) + ngu)),
            pl.BlockSpec((1, EXPERT_FF, td), lambda i, j, be, nu: (be[i], 0, dn(i, j, nu))),
            pl.BlockSpec((1, 1, td), lambda i, j, be, nu: (be[i], 0, dn(i, j, nu))),
        ],
        out_specs=pl.BlockSpec(memory_space=pl.ANY),
        scratch_shapes=[
            pltpu.SMEM((2, bm // LANES, LANES), jnp.int32),
            pltpu.VMEM((bm * SLAB_PITCH, LANES), jnp.uint32),
            pltpu.VMEM((bm, d), BF16),
            pltpu.VMEM((ngu, bm, tf), BF16),
            pltpu.VMEM((bm * SLAB_PITCH, LANES), jnp.uint32),
            pltpu.SemaphoreType.DMA(()),
            pltpu.SemaphoreType.DMA(()),
            pltpu.SemaphoreType.DMA(()),
        ],
    )
    return pl.pallas_call(
        functools.partial(_moe_kernel, bm=bm, ngu=ngu, ntok=t),
        grid_spec=grid_spec,
        out_shape=jax.ShapeDtypeStruct(((TOP_K * t + bm) * nsl, LANES), jnp.uint32),
        compiler_params=_params("arbitrary", "arbitrary"),
        name="moe_experts",
    )(block_expert, n_used, idx3, xp, wgu, wgu, bgu3, bgu3, wd, bd3)


def _routing_tables(top_idx, t):
    bm = MOE_BM
    tk = t * TOP_K
    n_items = tk // bm + N_EXPERTS
    n_rows = n_items * bm
    flat_e = top_idx.T.reshape(tk)
    order = jnp.argsort(flat_e, stable=True).astype(jnp.int32)
    sorted_e = flat_e[order]
    counts = jnp.zeros((N_EXPERTS,), jnp.int32).at[flat_e].add(1)
    padded = (counts + bm - 1) // bm * bm
    group_start = jnp.cumsum(counts) - counts
    padded_end = jnp.cumsum(padded)
    padded_start = padded_end - padded
    dest = padded_start[sorted_e] + jnp.arange(tk, dtype=jnp.int32) - group_start[sorted_e]
    pad_rows = tk + (jnp.arange(n_rows, dtype=jnp.int32) % bm)
    row_dst = pad_rows.at[dest].set(order)
    n_used = (padded_end[-1] // bm).astype(jnp.int32)
    item_start = jnp.arange(n_items, dtype=jnp.int32) * bm
    item_start = jnp.minimum(item_start, (n_used - 1) * bm)
    block_expert = jnp.minimum(jnp.searchsorted(padded_end, item_start, side="right"),
                               N_EXPERTS - 1).astype(jnp.int32)
    return row_dst, block_expert, n_used.reshape(1)


def kernel(x, p, w_in, attn_sinks, ret_norm_g, w_att_out, w_ret_out, w_out, ln1_g, ln1_b,
           w_router, b_router, w_gate_up, b_gate_up, w_down, b_down, ln2_g, ln2_b,
           w_ple, w_ple_gate, ln3_g, ln3_b):
    bsz, seq, d = x.shape
    t = bsz * seq
    assert seq % ATT_BLOCK == 0 and seq % RET_CHUNK == 0
    xf = x.reshape(t, d)
    for i in range(DEPTH):
        xb = xf.astype(BF16)
        w_in_b = w_in[i].astype(BF16)
        off = 0
        segs = {}
        for name, width in (("aq", ATT_Q_WIDTH), ("akv", 2 * ATT_KV_WIDTH), ("rqk", 2 * RET_QK_WIDTH),
                            ("rv", RET_V_WIDTH), ("rg", RET_V_WIDTH), ("ga", d), ("gr", d)):
            segs[name] = _proj(xb, w_in_b, off, width, "proj_" + name)
            off += width
        y_att = _attention(segs["aq"], segs["akv"], attn_sinks[i].astype(F32), seq // ATT_BLOCK)
        y_ret = _retention(segs["rqk"], segs["rv"], segs["rg"], ret_norm_g[i].astype(F32),
                           bsz, seq // RET_CHUNK)
        merged = _merge(y_att, y_ret, w_att_out[i].astype(BF16), w_ret_out[i].astype(BF16),
                        segs["ga"], segs["gr"])
        z1 = _resid_mm(merged, w_out[i].astype(BF16), xf)
        x1, x1p, gates, top_idx = _ln_router(z1, ln1_g[i], ln1_b[i], w_router[i], b_router[i])
        row_dst, block_expert, n_used = _routing_tables(top_idx[:, :TOP_K], t)
        yp = _moe(x1p, row_dst, block_expert, n_used, w_gate_up[i].astype(BF16),
                  b_gate_up[i].astype(F32), w_down[i].astype(BF16), b_down[i].astype(F32), t, d)
        x2, x2b = _combine_ln(x1, yp, gates, ln2_g[i], ln2_b[i])
        z3 = _ple(x2b, w_ple_gate[i].astype(BF16), p[i].reshape(t, -1), w_ple[i].astype(BF16), x2)
        xf = _ln(z3, ln3_g[i], ln3_b[i])
    return xf.reshape(bsz, seq, d)
```

```python
import functools
import math

import jax
import jax.numpy as jnp
import numpy as np
from jax import lax
from jax.experimental import pallas as pl
from jax.experimental.pallas import tpu as pltpu

F32 = jnp.float32
BF16 = jnp.bfloat16

ATT_HEADS = 32
ATT_KV_HEADS = 4
ATT_GROUP = ATT_HEADS // ATT_KV_HEADS
ATT_HEAD_DIM = 64
ATT_BLOCK = 128
RET_HEADS = 8
RET_KEY_DIM = 128
RET_VAL_DIM = 256
RET_CHUNK = 128
N_EXPERTS = 32
TOP_K = 4
EXPERT_FF = 1536
SWIGLU_LIMIT = 7.0
SWIGLU_ALPHA = 1.702
LN_EPS = 1e-5
DEPTH = 1
DEEPNORM_ALPHA = float((2 * DEPTH) ** 0.25)

ATT_Q_WIDTH = ATT_HEADS * ATT_HEAD_DIM
ATT_KV_WIDTH = ATT_KV_HEADS * ATT_HEAD_DIM
RET_QK_WIDTH = RET_HEADS * RET_KEY_DIM
RET_V_WIDTH = RET_HEADS * RET_VAL_DIM

LANES = 128
VMEM_LIMIT_BYTES = 56 * 1024 * 1024
MM_TM = 1024
MM_TN = 512
LN_TM = 256
COMBINE_TM = 128
MOE_BM = 512
MOE_TF = 512
MOE_TD = 2048
MOE_DC = 512
ROUTER_PAD = LANES
NEG_BIG = -1e30
SLAB_PITCH = 24
HI_MASK = 0xFFFF0000


def _params(*sem):
    return pltpu.CompilerParams(dimension_semantics=sem, vmem_limit_bytes=VMEM_LIMIT_BYTES)


def _sigmoid(x):
    return 1.0 / (1.0 + jnp.exp(-x))


def _bf16_bits(x):
    return lax.bitcast_convert_type(x.astype(BF16).astype(F32), jnp.uint32)


def _pack_words(lo, hi):
    return (_bf16_bits(lo) >> 16) | _bf16_bits(hi)


def _unpack_lo(w):
    return lax.bitcast_convert_type(w << 16, F32)


def _unpack_hi(w):
    return lax.bitcast_convert_type(w & jnp.uint32(HI_MASK), F32)


def _proj_kernel(x_ref, w_ref, o_ref):
    o_ref[...] = jnp.dot(x_ref[...], w_ref[...], preferred_element_type=F32).astype(o_ref.dtype)


def _proj(xb, wb, col_off, ncols, name):
    m, k = xb.shape
    tm, tn = min(MM_TM, m), MM_TN
    assert m % tm == 0 and ncols % tn == 0 and col_off % tn == 0
    off = col_off // tn
    return pl.pallas_call(
        _proj_kernel,
        grid=(ncols // tn, m // tm),
        in_specs=[pl.BlockSpec((tm, k), lambda j, i: (i, 0)),
                  pl.BlockSpec((k, tn), lambda j, i: (0, j + off))],
        out_specs=pl.BlockSpec((tm, tn), lambda j, i: (i, j)),
        out_shape=jax.ShapeDtypeStruct((m, ncols), BF16),
        compiler_params=_params("arbitrary", "arbitrary"),
        name=name,
    )(xb, wb)


def _merge_kernel(ya_ref, yr_ref, wa_ref, wr_ref, ga_ref, gr_ref, o_ref):
    a = jnp.dot(ya_ref[...], wa_ref[...], preferred_element_type=F32)
    r = jnp.dot(yr_ref[...], wr_ref[...], preferred_element_type=F32)
    ga = _sigmoid(ga_ref[...].astype(F32))
    gr = _sigmoid(gr_ref[...].astype(F32))
    o_ref[...] = (ga * a + gr * r).astype(o_ref.dtype)


def _merge(ya, yr, wa, wr, ga, gr):
    m, ka = ya.shape
    kr = yr.shape[1]
    n = wa.shape[1]
    tm, tn = min(MM_TM, m), MM_TN
    return pl.pallas_call(
        _merge_kernel,
        grid=(n // tn, m // tm),
        in_specs=[pl.BlockSpec((tm, ka), lambda j, i: (i, 0)),
                  pl.BlockSpec((tm, kr), lambda j, i: (i, 0)),
                  pl.BlockSpec((ka, tn), lambda j, i: (0, j)),
                  pl.BlockSpec((kr, tn), lambda j, i: (0, j)),
                  pl.BlockSpec((tm, tn), lambda j, i: (i, j)),
                  pl.BlockSpec((tm, tn), lambda j, i: (i, j))],
        out_specs=pl.BlockSpec((tm, tn), lambda j, i: (i, j)),
        out_shape=jax.ShapeDtypeStruct((m, n), BF16),
        compiler_params=_params("arbitrary", "arbitrary"),
        name="merge",
    )(ya, yr, wa, wr, ga, gr)


def _resid_mm_kernel(a_ref, w_ref, x_ref, o_ref):
    acc = jnp.dot(a_ref[...], w_ref[...], preferred_element_type=F32)
    o_ref[...] = DEEPNORM_ALPHA * x_ref[...] + acc


def _resid_mm(a, w, x):
    m, k = a.shape
    n = w.shape[1]
    tm, tn = min(MM_TM, m), MM_TN
    return pl.pallas_call(
        _resid_mm_kernel,
        grid=(n // tn, m // tm),
        in_specs=[pl.BlockSpec((tm, k), lambda j, i: (i, 0)),
                  pl.BlockSpec((k, tn), lambda j, i: (0, j)),
                  pl.BlockSpec((tm, tn), lambda j, i: (i, j))],
        out_specs=pl.BlockSpec((tm, tn), lambda j, i: (i, j)),
        out_shape=jax.ShapeDtypeStruct((m, n), F32),
        compiler_params=_params("arbitrary", "arbitrary"),
        name="resid_mm",
    )(a, w, x)


def _ple_kernel(xb_ref, wg_ref, p_ref, wp_ref, x_ref, o_ref):
    gate = jnp.dot(xb_ref[...], wg_ref[...], preferred_element_type=F32)
    emb = jnp.dot(p_ref[...].astype(BF16), wp_ref[...], preferred_element_type=F32)
    o_ref[...] = DEEPNORM_ALPHA * x_ref[...] + emb * _sigmoid(gate)


def _ple(xb, wg, p, wp, x):
    m, k = xb.shape
    n = wg.shape[1]
    kp = p.shape[1]
    tm, tn = min(MM_TM, m), MM_TN
    return pl.pallas_call(
        _ple_kernel,
        grid=(n // tn, m // tm),
        in_specs=[pl.BlockSpec((tm, k), lambda j, i: (i, 0)),
                  pl.BlockSpec((k, tn), lambda j, i: (0, j)),
                  pl.BlockSpec((tm, kp), lambda j, i: (i, 0)),
                  pl.BlockSpec((kp, tn), lambda j, i: (0, j)),
                  pl.BlockSpec((tm, tn), lambda j, i: (i, j))],
        out_specs=pl.BlockSpec((tm, tn), lambda j, i: (i, j)),
        out_shape=jax.ShapeDtypeStruct((m, n), F32),
        compiler_params=_params("arbitrary", "arbitrary"),
        name="ple",
    )(xb, wg, p, wp, x)


def _alibi_slope(h):
    return float(2.0 ** (-8.0 * (h + 1) / ATT_HEADS))


def _attn_kernel(sink_ref, q_ref, kvc_ref, kvp_ref, o_ref, *, nblk):
    blk = ATT_BLOCK
    n = lax.rem(pl.program_id(0), nblk)
    has_prev = n > 0
    row = lax.broadcasted_iota(jnp.int32, (blk, 2 * blk), 0)
    col = lax.broadcasted_iota(jnp.int32, (blk, 2 * blk), 1)
    dist = blk + row - col
    valid = (dist >= 0) & (dist < blk) & ((col >= blk) | has_prev)
    distf = dist.astype(F32)
    lane = lax.broadcasted_iota(jnp.int32, (2 * blk, LANES), 1)
    lo = lane < ATT_HEAD_DIM
    scale = ATT_HEAD_DIM ** -0.5

    def halves(t, half):
        swapped = jnp.concatenate([t[:, ATT_HEAD_DIM:], t[:, :ATT_HEAD_DIM]], axis=1)
        zero = jnp.zeros_like(t)
        if half == 0:
            return jnp.where(lo, t, zero), jnp.where(lo, zero, swapped)
        return jnp.where(lo, swapped, zero), jnp.where(lo, zero, t)

    for kh in range(ATT_KV_HEADS):
        tile, half = kh // 2, kh % 2
        ks = slice(tile * LANES, (tile + 1) * LANES)
        vs = slice(ATT_KV_WIDTH + tile * LANES, ATT_KV_WIDTH + (tile + 1) * LANES)
        kcat = jnp.concatenate([kvp_ref[:, ks], kvc_ref[:, ks]], axis=0)
        vcat = jnp.concatenate([kvp_ref[:, vs], kvc_ref[:, vs]], axis=0)
        k_lo, k_hi = halves(kcat, half)
        v_lo, v_hi = halves(vcat, half)
        for jp in range(ATT_GROUP // 2):
            h0 = kh * ATT_GROUP + 2 * jp
            qs = slice(h0 * ATT_HEAD_DIM, h0 * ATT_HEAD_DIM + LANES)
            qp = q_ref[:, qs]
            acc = jnp.zeros((blk, LANES), F32)
            for par, (kx, vx) in enumerate(((k_lo, v_lo), (k_hi, v_hi))):
                h = h0 + par
                s = lax.dot_general(qp, kx, (((1,), (1,)), ((), ())), preferred_element_type=F32)
                s = s * scale - _alibi_slope(h) * distf
                s = jnp.where(valid, s, -jnp.inf)
                sink = sink_ref[h]
                m = jnp.maximum(jnp.max(s, axis=-1, keepdims=True), sink)
                e = jnp.exp(s - m)
                denom = jnp.sum(e, axis=-1, keepdims=True) + jnp.exp(sink - m)
                pv = jnp.dot(e.astype(BF16), vx, preferred_element_type=F32)
                acc = acc + pv * (1.0 / denom)
            o_ref[:, qs] = acc.astype(o_ref.dtype)


def _attention(aq, akv, sinks, nblk):
    t = aq.shape[0]
    blk = ATT_BLOCK
    return pl.pallas_call(
        functools.partial(_attn_kernel, nblk=nblk),
        grid=(t // blk,),
        in_specs=[pl.BlockSpec(memory_space=pltpu.SMEM),
                  pl.BlockSpec((blk, ATT_Q_WIDTH), lambda i: (i, 0)),
                  pl.BlockSpec((blk, 2 * ATT_KV_WIDTH), lambda i: (i, 0)),
                  pl.BlockSpec((blk, 2 * ATT_KV_WIDTH), lambda i: (jnp.maximum(i - 1, 0), 0))],
        out_specs=pl.BlockSpec((blk, ATT_Q_WIDTH), lambda i: (i, 0)),
        out_shape=jax.ShapeDtypeStruct((t, ATT_Q_WIDTH), BF16),
        compiler_params=_params("arbitrary"),
        name="swa_attention",
    )(sinks, aq, akv, akv)


def _ret_constants():
    c = RET_CHUNK
    gam = 1.0 - 2.0 ** (-5.0 - np.arange(RET_HEADS, dtype=np.float64))
    lg = np.log(gam)
    pos = np.arange(c, dtype=np.float64)
    diff = pos[:, None] - pos[None, :]
    kscale = RET_KEY_DIM ** -0.5
    inner = np.where(diff >= 0, np.exp(np.maximum(diff, 0.0) * lg[:, None, None]), 0.0) * kscale
    qdec = np.exp((pos + 1.0) * lg[:, None])
    kdec = np.exp((c - 1.0 - pos) * lg[:, None]) * kscale
    cdec = np.exp(c * lg)
    qdec_b = np.broadcast_to(qdec[:, :, None], (RET_HEADS, c, RET_VAL_DIM))
    kdec_b = np.broadcast_to(kdec[:, :, None], (RET_HEADS, c, RET_KEY_DIM))
    return (jnp.asarray(inner, F32), jnp.asarray(qdec_b, F32), jnp.asarray(kdec_b, F32),
            [float(v) for v in cdec])


def _ret_kernel(qk_ref, v_ref, g_ref, ng_ref, inner_ref, qdec_ref, kdec_ref, o_ref, state_ref, *, cdec):
    @pl.when(pl.program_id(1) == 0)
    def _():
        state_ref[...] = jnp.zeros_like(state_ref)

    dk, dv = RET_KEY_DIM, RET_VAL_DIM
    for h in range(RET_HEADS):
        q = qk_ref[:, h * dk:(h + 1) * dk]
        k = qk_ref[:, RET_QK_WIDTH + h * dk:RET_QK_WIDTH + (h + 1) * dk]
        v = v_ref[:, h * dv:(h + 1) * dv]
        st = state_ref[h]
        qk = lax.dot_general(q, k, (((1,), (1,)), ((), ())), preferred_element_type=F32)
        a = (qk * inner_ref[h]).astype(BF16)
        inner = jnp.dot(a, v, preferred_element_type=F32)
        cross = jnp.dot(q, st.astype(BF16), preferred_element_type=F32) * qdec_ref[h]
        kd = (k.astype(F32) * kdec_ref[h]).astype(BF16)
        upd = lax.dot_general(kd, v, (((0,), (0,)), ((), ())), preferred_element_type=F32)
        state_ref[h] = st * cdec[h] + upd
        o = inner + cross
        mu = jnp.mean(o, axis=-1, keepdims=True)
        d = o - mu
        var = jnp.mean(d * d, axis=-1, keepdims=True)
        y = d * lax.rsqrt(var + LN_EPS) * ng_ref[:, h * dv:(h + 1) * dv]
        g = g_ref[:, h * dv:(h + 1) * dv].astype(F32)
        o_ref[:, h * dv:(h + 1) * dv] = (g * _sigmoid(g) * y).astype(o_ref.dtype)


def _retention(rqk, rv, rg, norm_g, batch, nchunk):
    t = rqk.shape[0]
    c = RET_CHUNK
    inner, qdec, kdec, cdec = _ret_constants()
    row = lambda b, n: (b * nchunk + n, 0)
    const3 = lambda b, n: (0, 0, 0)
    return pl.pallas_call(
        functools.partial(_ret_kernel, cdec=cdec),
        grid=(batch, nchunk),
        in_specs=[pl.BlockSpec((c, 2 * RET_QK_WIDTH), row),
                  pl.BlockSpec((c, RET_V_WIDTH), row),
                  pl.BlockSpec((c, RET_V_WIDTH), row),
                  pl.BlockSpec((1, RET_V_WIDTH), lambda b, n: (0, 0)),
                  pl.BlockSpec((RET_HEADS, c, c), const3),
                  pl.BlockSpec((RET_HEADS, c, RET_VAL_DIM), const3),
                  pl.BlockSpec((RET_HEADS, c, RET_KEY_DIM), const3)],
        out_specs=pl.BlockSpec((c, RET_V_WIDTH), row),
        out_shape=jax.ShapeDtypeStruct((t, RET_V_WIDTH), BF16),
        scratch_shapes=[pltpu.VMEM((RET_HEADS, RET_KEY_DIM, RET_VAL_DIM), F32)],
        compiler_params=_params("arbitrary", "arbitrary"),
        name="retention",
    )(rqk, rv, rg, norm_g.reshape(1, RET_V_WIDTH), inner, qdec, kdec)


def _ln_rows(z, g, b):
    mu = jnp.mean(z, axis=-1, keepdims=True)
    d = z - mu
    var = jnp.mean(d * d, axis=-1, keepdims=True)
    return d * lax.rsqrt(var + LN_EPS) * g + b


def _ln_router_kernel(z_ref, g_ref, b_ref, wr_ref, br_ref, x_ref, xp_ref, gate_ref, idx_ref):
    x = _ln_rows(z_ref[...], g_ref[...], b_ref[...])
    x_ref[...] = x
    tm, d = x.shape
    half = d // 2
    nsl = half // LANES
    for s in range(nsl):
        cs = slice(s * LANES, (s + 1) * LANES)
        hs = slice(half + s * LANES, half + (s + 1) * LANES)
        xp_ref[pl.ds(s, tm, stride=nsl), :] = _pack_words(x[:, cs], x[:, hs])
    logits = jnp.dot(x, wr_ref[...], preferred_element_type=F32,
                     precision=lax.Precision.HIGHEST) + br_ref[...]
    lane = lax.broadcasted_iota(jnp.int32, logits.shape, 1)
    lane_f = lane.astype(F32)
    vals, idxs = [], []
    cur = logits
    for _ in range(TOP_K):
        m = jnp.max(cur, axis=-1, keepdims=True)
        idx_f = jnp.min(jnp.where(cur == m, lane_f, float(ROUTER_PAD)), axis=-1, keepdims=True)
        idx = idx_f.astype(jnp.int32)
        vals.append(m)
        idxs.append(idx)
        cur = jnp.where(lane == idx, -jnp.inf, cur)
    es = [jnp.exp(v - vals[0]) for v in vals]
    inv = 1.0 / (es[0] + es[1] + es[2] + es[3])
    gates = jnp.zeros(logits.shape, F32)
    ids = jnp.zeros(logits.shape, jnp.int32)
    for k in range(TOP_K):
        gates = jnp.where(lane == k, es[k] * inv, gates)
        ids = jnp.where(lane == k, idxs[k], ids)
    gate_ref[...] = gates
    idx_ref[...] = ids


def _ln_router(z, g, b, w_router, b_router):
    t, d = z.shape
    tm = min(LN_TM, t)
    wr = jnp.zeros((d, ROUTER_PAD), F32).at[:, :N_EXPERTS].set(w_router.astype(F32))
    br = jnp.full((1, ROUTER_PAD), NEG_BIG, F32).at[0, :N_EXPERTS].set(b_router.astype(F32))
    row = lambda i: (i, 0)
    const = lambda i: (0, 0)
    nsl = d // 2 // LANES
    return pl.pallas_call(
        _ln_router_kernel,
        grid=(t // tm,),
        in_specs=[pl.BlockSpec((tm, d), row),
                  pl.BlockSpec((1, d), const),
                  pl.BlockSpec((1, d), const),
                  pl.BlockSpec((d, ROUTER_PAD), const),
                  pl.BlockSpec((1, ROUTER_PAD), const)],
        out_specs=[pl.BlockSpec((tm, d), row),
                   pl.BlockSpec((tm * nsl, LANES), row),
                   pl.BlockSpec((tm, ROUTER_PAD), row),
                   pl.BlockSpec((tm, ROUTER_PAD), row)],
        out_shape=[jax.ShapeDtypeStruct((t, d), F32),
                   jax.ShapeDtypeStruct((t * nsl, LANES), jnp.uint32),
                   jax.ShapeDtypeStruct((t, ROUTER_PAD), F32),
                   jax.ShapeDtypeStruct((t, ROUTER_PAD), jnp.int32)],
        compiler_params=_params("arbitrary"),
        name="ln_router",
    )(z, g.reshape(1, d), b.reshape(1, d), wr, br)


def _combine_ln_kernel(x_ref, y0_ref, y1_ref, y2_ref, y3_ref, gate_ref, g_ref, b_ref, o_ref, ob_ref):
    tm, d = x_ref.shape
    nsl = d // 2 // LANES
    gt = gate_ref[...]
    gk = [jnp.broadcast_to(gt[:, k:k + 1], (tm, LANES)) for k in range(TOP_K)]
    los, his = [], []
    for s in range(nsl):
        lo = hi = None
        for k, y_ref in enumerate((y0_ref, y1_ref, y2_ref, y3_ref)):
            w = y_ref[pl.ds(s, tm, stride=nsl), :]
            tl, th = gk[k] * _unpack_lo(w), gk[k] * _unpack_hi(w)
            lo = tl if lo is None else lo + tl
            hi = th if hi is None else hi + th
        los.append(lo)
        his.append(hi)
    z = DEEPNORM_ALPHA * x_ref[...] + jnp.concatenate(los + his, axis=1)
    out = _ln_rows(z, g_ref[...], b_ref[...])
    o_ref[...] = out
    ob_ref[...] = out.astype(ob_ref.dtype)


def _combine_ln(x1, yp, gates, g, b):
    t, d = x1.shape
    tm = min(COMBINE_TM, t)
    nrb = t // tm
    nsl = d // 2 // LANES
    row = lambda i: (i, 0)
    const = lambda i: (0, 0)
    y_specs = [pl.BlockSpec((tm * nsl, LANES), functools.partial(lambda i, k: (k * nrb + i, 0), k=k))
               for k in range(TOP_K)]
    return pl.pallas_call(
        _combine_ln_kernel,
        grid=(nrb,),
        in_specs=[pl.BlockSpec((tm, d), row)] + y_specs +
                 [pl.BlockSpec((tm, ROUTER_PAD), row),
                  pl.BlockSpec((1, d), const),
                  pl.BlockSpec((1, d), const)],
        out_specs=[pl.BlockSpec((tm, d), row), pl.BlockSpec((tm, d), row)],
        out_shape=[jax.ShapeDtypeStruct((t, d), F32), jax.ShapeDtypeStruct((t, d), BF16)],
        compiler_params=_params("arbitrary"),
        name="combine_ln",
    )(x1, yp, yp, yp, yp, gates, g.reshape(1, d), b.reshape(1, d))


def _ln_kernel(z_ref, g_ref, b_ref, o_ref):
    o_ref[...] = _ln_rows(z_ref[...], g_ref[...], b_ref[...])


def _ln(z, g, b):
    t, d = z.shape
    tm = min(LN_TM, t)
    return pl.pallas_call(
        _ln_kernel,
        grid=(t // tm,),
        in_specs=[pl.BlockSpec((tm, d), lambda i: (i, 0)),
                  pl.BlockSpec((1, d), lambda i: (0, 0)),
                  pl.BlockSpec((1, d), lambda i: (0, 0))],
        out_specs=pl.BlockSpec((tm, d), lambda i: (i, 0)),
        out_shape=jax.ShapeDtypeStruct((t, d), F32),
        compiler_params=_params("arbitrary"),
        name="layer_norm",
    )(z, g.reshape(1, d), b.reshape(1, d))


def _moe_kernel(be_ref, nu_ref, idx_hbm, x_hbm, wg_ref, wu_ref, bg_ref, bu_ref, wd_ref, bd_ref,
                y_hbm, idx_smem, xg, xb, act, ybuf, sem_idx, sem_g, sem_s, *, bm, ngu, ntok):
    del be_ref
    i = pl.program_id(0)
    j = pl.program_id(1)
    nu = nu_ref[0]
    active = i < nu
    d = xb.shape[1]
    half = d // 2
    nsl = half // LANES
    td = wd_ref.shape[2]

    def load_idx(item, slot):
        src = idx_hbm.at[pl.ds(pl.multiple_of(item * (2 * bm), 2 * bm), 2 * bm)]
        dst = idx_smem.at[pl.ds(pl.multiple_of(slot * (2 * bm), 2 * bm), 2 * bm)]
        cp = pltpu.make_async_copy(src, dst, sem_idx)
        cp.start()
        cp.wait()

    def for_each_row(slot, table, fn):
        base = slot * (2 * bm) + table * bm

        def body(r, c):
            fn(idx_smem[base + r], r)
            return c
        lax.fori_loop(0, bm, body, 0, unroll=8)

    def vmem_slab(buf, r):
        return buf.at[pl.ds(pl.multiple_of(r * SLAB_PITCH, 8), nsl), :]

    def hbm_slab(ref, row):
        return ref.at[pl.ds(pl.multiple_of(row, 8), nsl), :]

    def start_gather(slot):
        def one(src, r):
            pltpu.make_async_copy(hbm_slab(x_hbm, src), vmem_slab(xg, r), sem_g).start()
        for_each_row(slot, 0, one)

    def wait_gather():
        pltpu.make_async_copy(x_hbm.at[pl.ds(0, bm * nsl), :], xg.at[pl.ds(0, bm * nsl), :], sem_g).wait()

    def start_scatter(slot):
        def one(dst, r):
            pltpu.make_async_copy(vmem_slab(ybuf, r), hbm_slab(y_hbm, dst), sem_s).start()
        for_each_row(slot, 1, one)

    def wait_scatter():
        pltpu.make_async_copy(ybuf.at[pl.ds(0, bm * nsl), :], y_hbm.at[pl.ds(0, bm * nsl), :], sem_s).wait()

    @pl.when(active & (j == 0))
    def _():
        @pl.when(i == 0)
        def _():
            load_idx(0, 0)
            start_gather(0)
            ybuf[...] = jnp.zeros_like(ybuf)
            pad = pltpu.make_async_copy(ybuf.at[pl.ds(0, bm * nsl), :],
                                        y_hbm.at[pl.ds(TOP_K * ntok * nsl, bm * nsl), :], sem_s)
            pad.start()
            pad.wait()
        wait_gather()
        for s in range(nsl):
            w = xg[pl.ds(s, bm, stride=SLAB_PITCH), :]
            xb[:, s * LANES:(s + 1) * LANES] = _unpack_lo(w).astype(BF16)
            xb[:, half + s * LANES:half + (s + 1) * LANES] = _unpack_hi(w).astype(BF16)

        @pl.when(i + 1 < nu)
        def _():
            nxt = lax.rem(i + 1, 2)
            load_idx(i + 1, nxt)
            start_gather(nxt)

    @pl.when(active & (j < ngu))
    def _():
        x = xb[...]
        g = jnp.dot(x, wg_ref[0], preferred_element_type=F32) + bg_ref[0]
        u = jnp.dot(x, wu_ref[0], preferred_element_type=F32) + bu_ref[0]
        g = jnp.minimum(g, SWIGLU_LIMIT)
        u = jnp.clip(u, -SWIGLU_LIMIT, SWIGLU_LIMIT)
        act[j] = (g * _sigmoid(SWIGLU_ALPHA * g) * (u + 1.0)).astype(BF16)

    def down(is_hi):
        a = jnp.concatenate([act[q] for q in range(ngu)], axis=1)
        for c in range(td // MOE_DC):
            cs = slice(c * MOE_DC, (c + 1) * MOE_DC)
            y = jnp.dot(a, wd_ref[0, :, cs], preferred_element_type=F32) + bd_ref[0, :, cs]
            bits = _bf16_bits(y)
            for q in range(MOE_DC // LANES):
                s = c * (MOE_DC // LANES) + q
                piece = bits[:, q * LANES:(q + 1) * LANES]
                rows = pl.ds(s, bm, stride=SLAB_PITCH)
                if is_hi:
                    ybuf[rows, :] = ybuf[rows, :] | piece
                else:
                    ybuf[rows, :] = piece >> 16

    @pl.when(active & (j == ngu))
    def _():
        @pl.when(i > 0)
        def _():
            wait_scatter()
        down(False)

    @pl.when(active & (j == ngu + 1))
    def _():
        down(True)
        start_scatter(lax.rem(i, 2))

        @pl.when(i == nu - 1)
        def _():
            wait_scatter()


def _moe(xp, row_dst, block_expert, n_used, wgu, bgu, wd, bd, t, d):
    bm, tf, td = MOE_BM, MOE_TF, MOE_TD
    half = d // 2
    nsl = half // LANES
    assert td == half and td % MOE_DC == 0 and EXPERT_FF % tf == 0
    assert nsl % 8 == 0 and SLAB_PITCH % 8 == 0 and SLAB_PITCH >= nsl
    n_items = row_dst.shape[0] // bm
    ngu = EXPERT_FF // tf
    nj = ngu + d // td
    idx3 = jnp.stack([((row_dst % t) * nsl).reshape(n_items, bm), (row_dst * nsl).reshape(n_items, bm)],
                     axis=1).reshape(n_items * 2 * bm)
    bgu3 = bgu.reshape(N_EXPERTS, 1, 2 * EXPERT_FF)
    bd3 = bd.reshape(N_EXPERTS, 1, d)

    def step(i, j, nu):
        return jnp.where(i < nu[0], j, nj - 1)

    def gu(i, j, nu):
        return jnp.minimum(step(i, j, nu), ngu - 1)

    def dn(i, j, nu):
        return jnp.maximum(step(i, j, nu) - ngu, 0)

    grid_spec = pltpu.PrefetchScalarGridSpec(
        num_scalar_prefetch=2,
        grid=(n_items, nj),
        in_specs=[
            pl.BlockSpec(memory_space=pl.ANY),
            pl.BlockSpec(memory_space=pl.ANY),
            pl.BlockSpec((1, d, tf), lambda i, j, be, nu: (be[i], 0, gu(i, j, nu))),
            pl.BlockSpec((1, d, tf), lambda i, j, be, nu: (be[i], 0, gu(i, j, nu) + ngu)),
            pl.BlockSpec((1, 1, tf), lambda i, j, be, nu: (be[i], 0, gu(i, j, nu))),
            pl.BlockSpec((1, 1, tf), lambda i, j, be, nu: (be[i], 0, gu(i, j, nu) + ngu)),
            pl.BlockSpec((1, EXPERT_FF, td), lambda i, j, be, nu: (be[i], 0, dn(i, j, nu))),
            pl.BlockSpec((1, 1, td), lambda i, j, be, nu: (be[i], 0, dn(i, j, nu))),
        ],
        out_specs=pl.BlockSpec(memory_space=pl.ANY),
        scratch_shapes=[
            pltpu.SMEM((2 * 2 * bm,), jnp.int32),
            pltpu.VMEM((bm * SLAB_PITCH, LANES), jnp.uint32),
            pltpu.VMEM((bm, d), BF16),
            pltpu.VMEM((ngu, bm, tf), BF16),
            pltpu.VMEM((bm * SLAB_PITCH, LANES), jnp.uint32),
            pltpu.SemaphoreType.DMA(()),
            pltpu.SemaphoreType.DMA(()),
            pltpu.SemaphoreType.DMA(()),
        ],
    )
    return pl.pallas_call(
        functools.partial(_moe_kernel, bm=bm, ngu=ngu, ntok=t),
        grid_spec=grid_spec,
        out_shape=jax.ShapeDtypeStruct(((TOP_K * t + bm) * nsl, LANES), jnp.uint32),
        compiler_params=_params("arbitrary", "arbitrary"),
        name="moe_experts",
    )(block_expert, n_used, idx3, xp, wgu, wgu, bgu3, bgu3, wd, bd3)


def _routing_tables(top_idx, t):
    bm = MOE_BM
    tk = t * TOP_K
    n_items = tk // bm + N_EXPERTS
    n_rows = n_items * bm
    flat_e = top_idx.T.reshape(tk)
    experts = jnp.arange(N_EXPERTS, dtype=jnp.int32)
    order = jnp.argsort(flat_e, stable=True).astype(jnp.int32)
    counts = jnp.sum((flat_e[:, None] == experts[None, :]).astype(jnp.int32), axis=0)
    padded = (counts + bm - 1) // bm * bm
    group_end = jnp.cumsum(counts)
    group_start = group_end - counts
    padded_end = jnp.cumsum(padded)
    padded_start = padded_end - padded

    def expert_of(rows):
        e = jnp.sum((padded_end[None, :] <= rows[:, None]).astype(jnp.int32), axis=1)
        return jnp.minimum(e, N_EXPERTS - 1)

    def lookup(table, e):
        return jnp.sum(jnp.where(e[:, None] == experts[None, :], table[None, :], 0), axis=1)

    rows = jnp.arange(n_rows, dtype=jnp.int32)
    row_e = expert_of(rows)
    pos = rows - lookup(padded_start - group_start, row_e)
    valid = pos < lookup(group_end, row_e)
    row_dst = jnp.where(valid, order[jnp.clip(pos, 0, tk - 1)], tk + rows % bm)
    n_used = (padded_end[-1] // bm).astype(jnp.int32)
    item_start = jnp.minimum(jnp.arange(n_items, dtype=jnp.int32), n_used - 1) * bm
    block_expert = expert_of(item_start)
    return row_dst, block_expert, n_used.reshape(1)


def kernel(x, p, w_in, attn_sinks, ret_norm_g, w_att_out, w_ret_out, w_out, ln1_g, ln1_b,
           w_router, b_router, w_gate_up, b_gate_up, w_down, b_down, ln2_g, ln2_b,
           w_ple, w_ple_gate, ln3_g, ln3_b):
    bsz, seq, d = x.shape
    t = bsz * seq
    assert seq % ATT_BLOCK == 0 and seq % RET_CHUNK == 0
    xf = x.reshape(t, d)
    for i in range(DEPTH):
        xb = xf.astype(BF16)
        w_in_b = w_in[i].astype(BF16)
        off = 0
        segs = {}
        for name, width in (("aq", ATT_Q_WIDTH), ("akv", 2 * ATT_KV_WIDTH), ("rqk", 2 * RET_QK_WIDTH),
                            ("rv", RET_V_WIDTH), ("rg", RET_V_WIDTH), ("ga", d), ("gr", d)):
            segs[name] = _proj(xb, w_in_b, off, width, "proj_" + name)
            off += width
        y_att = _attention(segs["aq"], segs["akv"], attn_sinks[i].astype(F32), seq // ATT_BLOCK)
        y_ret = _retention(segs["rqk"], segs["rv"], segs["rg"], ret_norm_g[i].astype(F32),
                           bsz, seq // RET_CHUNK)
        merged = _merge(y_att, y_ret, w_att_out[i].astype(BF16), w_ret_out[i].astype(BF16),
                        segs["ga"], segs["gr"])
        z1 = _resid_mm(merged, w_out[i].astype(BF16), xf)
        x1, x1p, gates, top_idx = _ln_router(z1, ln1_g[i], ln1_b[i], w_router[i], b_router[i])
        row_dst, block_expert, n_used = _routing_tables(top_idx[:, :TOP_K], t)
        yp = _moe(x1p, row_dst, block_expert, n_used, w_gate_up[i].astype(BF16),
                  b_gate_up[i].astype(F32), w_down[i].astype(BF16), b_down[i].astype(F32), t, d)
        x2, x2b = _combine_ln(x1, yp, gates, ln2_g[i], ln2_b[i])
        z3 = _ple(x2b, w_ple_gate[i].astype(BF16), p[i].reshape(t, -1), w_ple[i].astype(BF16), x2)
        xf = _ln(z3, ln3_g[i], ln3_b[i])
    return xf.reshape(bsz, seq, d)
```

```python
import functools
import math

import jax
import jax.numpy as jnp
import numpy as np
from jax import lax
from jax.experimental import pallas as pl
from jax.experimental.pallas import tpu as pltpu

F32 = jnp.float32
BF16 = jnp.bfloat16

ATT_HEADS = 32
ATT_KV_HEADS = 4
ATT_GROUP = ATT_HEADS // ATT_KV_HEADS
ATT_HEAD_DIM = 64
ATT_BLOCK = 128
RET_HEADS = 8
RET_KEY_DIM = 128
RET_VAL_DIM = 256
RET_CHUNK = 128
N_EXPERTS = 32
TOP_K = 4
EXPERT_FF = 1536
SWIGLU_LIMIT = 7.0
SWIGLU_ALPHA = 1.702
LN_EPS = 1e-5
DEPTH = 1
DEEPNORM_ALPHA = float((2 * DEPTH) ** 0.25)

ATT_Q_WIDTH = ATT_HEADS * ATT_HEAD_DIM
ATT_KV_WIDTH = ATT_KV_HEADS * ATT_HEAD_DIM
RET_QK_WIDTH = RET_HEADS * RET_KEY_DIM
RET_V_WIDTH = RET_HEADS * RET_VAL_DIM

LANES = 128
VMEM_LIMIT_BYTES = 56 * 1024 * 1024
MM_TM = 1024
MM_TN = 512
LN_TM = 256
COMBINE_TM = 128
MOE_BM = 512
MOE_TF = 512
MOE_TD = 2048
MOE_DC = 512
ROUTER_PAD = LANES
NEG_BIG = -1e30
SLAB_PITCH = 24
HI_MASK = 0xFFFF0000


def _params(*sem):
    return pltpu.CompilerParams(dimension_semantics=sem, vmem_limit_bytes=VMEM_LIMIT_BYTES)


def _sigmoid(x):
    return 1.0 / (1.0 + jnp.exp(-x))


def _bf16_bits(x):
    return lax.bitcast_convert_type(x.astype(BF16).astype(F32), jnp.uint32)


def _pack_words(lo, hi):
    return (_bf16_bits(lo) >> 16) | _bf16_bits(hi)


def _unpack_lo(w):
    return lax.bitcast_convert_type(w << 16, F32)


def _unpack_hi(w):
    return lax.bitcast_convert_type(w & jnp.uint32(HI_MASK), F32)


def _cast_at_first_row_tile(w_ref, wb_ref):
    @pl.when(pl.program_id(1) == 0)
    def _():
        wb_ref[...] = w_ref[...].astype(BF16)


def _proj_kernel(x_ref, w_ref, o_ref, wb_ref):
    _cast_at_first_row_tile(w_ref, wb_ref)
    o_ref[...] = jnp.dot(x_ref[...], wb_ref[...], preferred_element_type=F32).astype(o_ref.dtype)


def _proj(xb, w, col_off, ncols, name):
    m, k = xb.shape
    tm, tn = min(MM_TM, m), MM_TN
    assert m % tm == 0 and ncols % tn == 0 and col_off % tn == 0
    off = col_off // tn
    return pl.pallas_call(
        _proj_kernel,
        grid=(ncols // tn, m // tm),
        in_specs=[pl.BlockSpec((tm, k), lambda j, i: (i, 0)),
                  pl.BlockSpec((k, tn), lambda j, i: (0, j + off))],
        out_specs=pl.BlockSpec((tm, tn), lambda j, i: (i, j)),
        out_shape=jax.ShapeDtypeStruct((m, ncols), BF16),
        scratch_shapes=[pltpu.VMEM((k, tn), BF16)],
        compiler_params=_params("arbitrary", "arbitrary"),
        name=name,
    )(xb, w)


def _merge_kernel(ya_ref, yr_ref, wa_ref, wr_ref, ga_ref, gr_ref, o_ref, wab_ref, wrb_ref):
    _cast_at_first_row_tile(wa_ref, wab_ref)
    _cast_at_first_row_tile(wr_ref, wrb_ref)
    a = jnp.dot(ya_ref[...], wab_ref[...], preferred_element_type=F32)
    r = jnp.dot(yr_ref[...], wrb_ref[...], preferred_element_type=F32)
    ga = _sigmoid(ga_ref[...].astype(F32))
    gr = _sigmoid(gr_ref[...].astype(F32))
    o_ref[...] = (ga * a + gr * r).astype(o_ref.dtype)


def _merge(ya, yr, wa, wr, ga, gr):
    m, ka = ya.shape
    kr = yr.shape[1]
    n = wa.shape[1]
    tm, tn = min(MM_TM, m), MM_TN
    return pl.pallas_call(
        _merge_kernel,
        grid=(n // tn, m // tm),
        in_specs=[pl.BlockSpec((tm, ka), lambda j, i: (i, 0)),
                  pl.BlockSpec((tm, kr), lambda j, i: (i, 0)),
                  pl.BlockSpec((ka, tn), lambda j, i: (0, j)),
                  pl.BlockSpec((kr, tn), lambda j, i: (0, j)),
                  pl.BlockSpec((tm, tn), lambda j, i: (i, j)),
                  pl.BlockSpec((tm, tn), lambda j, i: (i, j))],
        out_specs=pl.BlockSpec((tm, tn), lambda j, i: (i, j)),
        out_shape=jax.ShapeDtypeStruct((m, n), BF16),
        scratch_shapes=[pltpu.VMEM((ka, tn), BF16), pltpu.VMEM((kr, tn), BF16)],
        compiler_params=_params("arbitrary", "arbitrary"),
        name="merge",
    )(ya, yr, wa, wr, ga, gr)


def _resid_mm_kernel(a_ref, w_ref, x_ref, o_ref, wb_ref):
    _cast_at_first_row_tile(w_ref, wb_ref)
    acc = jnp.dot(a_ref[...], wb_ref[...], preferred_element_type=F32)
    o_ref[...] = DEEPNORM_ALPHA * x_ref[...] + acc


def _resid_mm(a, w, x):
    m, k = a.shape
    n = w.shape[1]
    tm, tn = min(MM_TM, m), MM_TN
    return pl.pallas_call(
        _resid_mm_kernel,
        grid=(n // tn, m // tm),
        in_specs=[pl.BlockSpec((tm, k), lambda j, i: (i, 0)),
                  pl.BlockSpec((k, tn), lambda j, i: (0, j)),
                  pl.BlockSpec((tm, tn), lambda j, i: (i, j))],
        out_specs=pl.BlockSpec((tm, tn), lambda j, i: (i, j)),
        out_shape=jax.ShapeDtypeStruct((m, n), F32),
        scratch_shapes=[pltpu.VMEM((k, tn), BF16)],
        compiler_params=_params("arbitrary", "arbitrary"),
        name="resid_mm",
    )(a, w, x)


def _ple_kernel(xb_ref, wg_ref, p_ref, wp_ref, x_ref, o_ref, wgb_ref, wpb_ref):
    _cast_at_first_row_tile(wg_ref, wgb_ref)
    _cast_at_first_row_tile(wp_ref, wpb_ref)
    gate = jnp.dot(xb_ref[...], wgb_ref[...], preferred_element_type=F32)
    emb = jnp.dot(p_ref[...].astype(BF16), wpb_ref[...], preferred_element_type=F32)
    o_ref[...] = DEEPNORM_ALPHA * x_ref[...] + emb * _sigmoid(gate)


def _ple(xb, wg, p, wp, x):
    m, k = xb.shape
    n = wg.shape[1]
    kp = p.shape[1]
    tm, tn = min(MM_TM, m), MM_TN
    return pl.pallas_call(
        _ple_kernel,
        grid=(n // tn, m // tm),
        in_specs=[pl.BlockSpec((tm, k), lambda j, i: (i, 0)),
                  pl.BlockSpec((k, tn), lambda j, i: (0, j)),
                  pl.BlockSpec((tm, kp), lambda j, i: (i, 0)),
                  pl.BlockSpec((kp, tn), lambda j, i: (0, j)),
                  pl.BlockSpec((tm, tn), lambda j, i: (i, j))],
        out_specs=pl.BlockSpec((tm, tn), lambda j, i: (i, j)),
        out_shape=jax.ShapeDtypeStruct((m, n), F32),
        scratch_shapes=[pltpu.VMEM((k, tn), BF16), pltpu.VMEM((kp, tn), BF16)],
        compiler_params=_params("arbitrary", "arbitrary"),
        name="ple",
    )(xb, wg, p, wp, x)


def _alibi_slope(h):
    return float(2.0 ** (-8.0 * (h + 1) / ATT_HEADS))


def _attn_kernel(sink_ref, q_ref, kvc_ref, kvp_ref, o_ref, *, nblk):
    blk = ATT_BLOCK
    n = lax.rem(pl.program_id(0), nblk)
    has_prev = n > 0
    row = lax.broadcasted_iota(jnp.int32, (blk, 2 * blk), 0)
    col = lax.broadcasted_iota(jnp.int32, (blk, 2 * blk), 1)
    dist = blk + row - col
    valid = (dist >= 0) & (dist < blk) & ((col >= blk) | has_prev)
    distf = dist.astype(F32)
    lane = lax.broadcasted_iota(jnp.int32, (2 * blk, LANES), 1)
    lo = lane < ATT_HEAD_DIM
    scale = ATT_HEAD_DIM ** -0.5

    def halves(t, half):
        swapped = jnp.concatenate([t[:, ATT_HEAD_DIM:], t[:, :ATT_HEAD_DIM]], axis=1)
        zero = jnp.zeros_like(t)
        if half == 0:
            return jnp.where(lo, t, zero), jnp.where(lo, zero, swapped)
        return jnp.where(lo, swapped, zero), jnp.where(lo, zero, t)

    for kh in range(ATT_KV_HEADS):
        tile, half = kh // 2, kh % 2
        ks = slice(tile * LANES, (tile + 1) * LANES)
        vs = slice(ATT_KV_WIDTH + tile * LANES, ATT_KV_WIDTH + (tile + 1) * LANES)
        kcat = jnp.concatenate([kvp_ref[:, ks], kvc_ref[:, ks]], axis=0)
        vcat = jnp.concatenate([kvp_ref[:, vs], kvc_ref[:, vs]], axis=0)
        k_lo, k_hi = halves(kcat, half)
        v_lo, v_hi = halves(vcat, half)
        for jp in range(ATT_GROUP // 2):
            h0 = kh * ATT_GROUP + 2 * jp
            qs = slice(h0 * ATT_HEAD_DIM, h0 * ATT_HEAD_DIM + LANES)
            qp = q_ref[:, qs]
            acc = jnp.zeros((blk, LANES), F32)
            for par, (kx, vx) in enumerate(((k_lo, v_lo), (k_hi, v_hi))):
                h = h0 + par
                s = lax.dot_general(qp, kx, (((1,), (1,)), ((), ())), preferred_element_type=F32)
                s = s * scale - _alibi_slope(h) * distf
                s = jnp.where(valid, s, -jnp.inf)
                sink = sink_ref[h]
                m = jnp.maximum(jnp.max(s, axis=-1, keepdims=True), sink)
                e = jnp.exp(s - m)
                denom = jnp.sum(e, axis=-1, keepdims=True) + jnp.exp(sink - m)
                pv = jnp.dot(e.astype(BF16), vx, preferred_element_type=F32)
                acc = acc + pv * (1.0 / denom)
            o_ref[:, qs] = acc.astype(o_ref.dtype)


def _attention(aq, akv, sinks, nblk):
    t = aq.shape[0]
    blk = ATT_BLOCK
    return pl.pallas_call(
        functools.partial(_attn_kernel, nblk=nblk),
        grid=(t // blk,),
        in_specs=[pl.BlockSpec(memory_space=pltpu.SMEM),
                  pl.BlockSpec((blk, ATT_Q_WIDTH), lambda i: (i, 0)),
                  pl.BlockSpec((blk, 2 * ATT_KV_WIDTH), lambda i: (i, 0)),
                  pl.BlockSpec((blk, 2 * ATT_KV_WIDTH), lambda i: (jnp.maximum(i - 1, 0), 0))],
        out_specs=pl.BlockSpec((blk, ATT_Q_WIDTH), lambda i: (i, 0)),
        out_shape=jax.ShapeDtypeStruct((t, ATT_Q_WIDTH), BF16),
        compiler_params=_params("arbitrary"),
        name="swa_attention",
    )(sinks, aq, akv, akv)


def _ret_constants():
    c = RET_CHUNK
    gam = 1.0 - 2.0 ** (-5.0 - np.arange(RET_HEADS, dtype=np.float64))
    lg = np.log(gam)
    pos = np.arange(c, dtype=np.float64)
    diff = pos[:, None] - pos[None, :]
    kscale = RET_KEY_DIM ** -0.5
    inner = np.where(diff >= 0, np.exp(np.maximum(diff, 0.0) * lg[:, None, None]), 0.0) * kscale
    qdec = np.exp((pos + 1.0) * lg[:, None])
    kdec = np.exp((c - 1.0 - pos) * lg[:, None]) * kscale
    cdec = np.exp(c * lg)
    qdec_b = np.broadcast_to(qdec[:, :, None], (RET_HEADS, c, RET_VAL_DIM))
    kdec_b = np.broadcast_to(kdec[:, :, None], (RET_HEADS, c, RET_KEY_DIM))
    return (jnp.asarray(inner, F32), jnp.asarray(qdec_b, F32), jnp.asarray(kdec_b, F32),
            [float(v) for v in cdec])


def _ret_kernel(qk_ref, v_ref, g_ref, ng_ref, inner_ref, qdec_ref, kdec_ref, o_ref, state_ref, *, cdec):
    @pl.when(pl.program_id(1) == 0)
    def _():
        state_ref[...] = jnp.zeros_like(state_ref)

    dk, dv = RET_KEY_DIM, RET_VAL_DIM
    for h in range(RET_HEADS):
        q = qk_ref[:, h * dk:(h + 1) * dk]
        k = qk_ref[:, RET_QK_WIDTH + h * dk:RET_QK_WIDTH + (h + 1) * dk]
        v = v_ref[:, h * dv:(h + 1) * dv]
        st = state_ref[h]
        qk = lax.dot_general(q, k, (((1,), (1,)), ((), ())), preferred_element_type=F32)
        a = (qk * inner_ref[h]).astype(BF16)
        inner = jnp.dot(a, v, preferred_element_type=F32)
        cross = jnp.dot(q, st.astype(BF16), preferred_element_type=F32) * qdec_ref[h]
        kd = (k.astype(F32) * kdec_ref[h]).astype(BF16)
        upd = lax.dot_general(kd, v, (((0,), (0,)), ((), ())), preferred_element_type=F32)
        state_ref[h] = st * cdec[h] + upd
        o = inner + cross
        mu = jnp.mean(o, axis=-1, keepdims=True)
        d = o - mu
        var = jnp.mean(d * d, axis=-1, keepdims=True)
        y = d * lax.rsqrt(var + LN_EPS) * ng_ref[:, h * dv:(h + 1) * dv]
        g = g_ref[:, h * dv:(h + 1) * dv].astype(F32)
        o_ref[:, h * dv:(h + 1) * dv] = (g * _sigmoid(g) * y).astype(o_ref.dtype)


def _retention(rqk, rv, rg, norm_g, batch, nchunk):
    t = rqk.shape[0]
    c = RET_CHUNK
    inner, qdec, kdec, cdec = _ret_constants()
    row = lambda b, n: (b * nchunk + n, 0)
    const3 = lambda b, n: (0, 0, 0)
    return pl.pallas_call(
        functools.partial(_ret_kernel, cdec=cdec),
        grid=(batch, nchunk),
        in_specs=[pl.BlockSpec((c, 2 * RET_QK_WIDTH), row),
                  pl.BlockSpec((c, RET_V_WIDTH), row),
                  pl.BlockSpec((c, RET_V_WIDTH), row),
                  pl.BlockSpec((1, RET_V_WIDTH), lambda b, n: (0, 0)),
                  pl.BlockSpec((RET_HEADS, c, c), const3),
                  pl.BlockSpec((RET_HEADS, c, RET_VAL_DIM), const3),
                  pl.BlockSpec((RET_HEADS, c, RET_KEY_DIM), const3)],
        out_specs=pl.BlockSpec((c, RET_V_WIDTH), row),
        out_shape=jax.ShapeDtypeStruct((t, RET_V_WIDTH), BF16),
        scratch_shapes=[pltpu.VMEM((RET_HEADS, RET_KEY_DIM, RET_VAL_DIM), F32)],
        compiler_params=_params("arbitrary", "arbitrary"),
        name="retention",
    )(rqk, rv, rg, norm_g.reshape(1, RET_V_WIDTH), inner, qdec, kdec)


def _ln_rows(z, g, b):
    mu = jnp.mean(z, axis=-1, keepdims=True)
    d = z - mu
    var = jnp.mean(d * d, axis=-1, keepdims=True)
    return d * lax.rsqrt(var + LN_EPS) * g + b


def _ln_router_kernel(z_ref, g_ref, b_ref, wr_ref, br_ref, x_ref, xp_ref, gate_ref, idx_ref):
    x = _ln_rows(z_ref[...], g_ref[...], b_ref[...])
    x_ref[...] = x
    tm, d = x.shape
    half = d // 2
    nsl = half // LANES
    for s in range(nsl):
        cs = slice(s * LANES, (s + 1) * LANES)
        hs = slice(half + s * LANES, half + (s + 1) * LANES)
        xp_ref[pl.ds(s, tm, stride=nsl), :] = _pack_words(x[:, cs], x[:, hs])
    logits = jnp.dot(x, wr_ref[...], preferred_element_type=F32,
                     precision=lax.Precision.HIGHEST) + br_ref[...]
    lane = lax.broadcasted_iota(jnp.int32, logits.shape, 1)
    lane_f = lane.astype(F32)
    vals, idxs = [], []
    cur = logits
    for _ in range(TOP_K):
        m = jnp.max(cur, axis=-1, keepdims=True)
        idx_f = jnp.min(jnp.where(cur == m, lane_f, float(ROUTER_PAD)), axis=-1, keepdims=True)
        idx = idx_f.astype(jnp.int32)
        vals.append(m)
        idxs.append(idx)
        cur = jnp.where(lane == idx, -jnp.inf, cur)
    es = [jnp.exp(v - vals[0]) for v in vals]
    inv = 1.0 / (es[0] + es[1] + es[2] + es[3])
    gates = jnp.zeros(logits.shape, F32)
    ids = jnp.zeros(logits.shape, jnp.int32)
    for k in range(TOP_K):
        gates = jnp.where(lane == k, es[k] * inv, gates)
        ids = jnp.where(lane == k, idxs[k], ids)
    gate_ref[...] = gates
    idx_ref[...] = ids


def _ln_router(z, g, b, w_router, b_router):
    t, d = z.shape
    tm = min(LN_TM, t)
    wr = jnp.zeros((d, ROUTER_PAD), F32).at[:, :N_EXPERTS].set(w_router.astype(F32))
    br = jnp.full((1, ROUTER_PAD), NEG_BIG, F32).at[0, :N_EXPERTS].set(b_router.astype(F32))
    row = lambda i: (i, 0)
    const = lambda i: (0, 0)
    nsl = d // 2 // LANES
    return pl.pallas_call(
        _ln_router_kernel,
        grid=(t // tm,),
        in_specs=[pl.BlockSpec((tm, d), row),
                  pl.BlockSpec((1, d), const),
                  pl.BlockSpec((1, d), const),
                  pl.BlockSpec((d, ROUTER_PAD), const),
                  pl.BlockSpec((1, ROUTER_PAD), const)],
        out_specs=[pl.BlockSpec((tm, d), row),
                   pl.BlockSpec((tm * nsl, LANES), row),
                   pl.BlockSpec((tm, ROUTER_PAD), row),
                   pl.BlockSpec((tm, ROUTER_PAD), row)],
        out_shape=[jax.ShapeDtypeStruct((t, d), F32),
                   jax.ShapeDtypeStruct((t * nsl, LANES), jnp.uint32),
                   jax.ShapeDtypeStruct((t, ROUTER_PAD), F32),
                   jax.ShapeDtypeStruct((t, ROUTER_PAD), jnp.int32)],
        compiler_params=_params("arbitrary"),
        name="ln_router",
    )(z, g.reshape(1, d), b.reshape(1, d), wr, br)


def _combine_ln_kernel(x_ref, y0_ref, y1_ref, y2_ref, y3_ref, gate_ref, g_ref, b_ref, o_ref, ob_ref):
    tm, d = x_ref.shape
    nsl = d // 2 // LANES
    gt = gate_ref[...]
    gk = [jnp.broadcast_to(gt[:, k:k + 1], (tm, LANES)) for k in range(TOP_K)]
    los, his = [], []
    for s in range(nsl):
        lo = hi = None
        for k, y_ref in enumerate((y0_ref, y1_ref, y2_ref, y3_ref)):
            w = y_ref[pl.ds(s, tm, stride=nsl), :]
            tl, th = gk[k] * _unpack_lo(w), gk[k] * _unpack_hi(w)
            lo = tl if lo is None else lo + tl
            hi = th if hi is None else hi + th
        los.append(lo)
        his.append(hi)
    z = DEEPNORM_ALPHA * x_ref[...] + jnp.concatenate(los + his, axis=1)
    out = _ln_rows(z, g_ref[...], b_ref[...])
    o_ref[...] = out
    ob_ref[...] = out.astype(ob_ref.dtype)


def _combine_ln(x1, yp, gates, g, b):
    t, d = x1.shape
    tm = min(COMBINE_TM, t)
    nrb = t // tm
    nsl = d // 2 // LANES
    row = lambda i: (i, 0)
    const = lambda i: (0, 0)
    y_specs = [pl.BlockSpec((tm * nsl, LANES), functools.partial(lambda i, k: (k * nrb + i, 0), k=k))
               for k in range(TOP_K)]
    return pl.pallas_call(
        _combine_ln_kernel,
        grid=(nrb,),
        in_specs=[pl.BlockSpec((tm, d), row)] + y_specs +
                 [pl.BlockSpec((tm, ROUTER_PAD), row),
                  pl.BlockSpec((1, d), const),
                  pl.BlockSpec((1, d), const)],
        out_specs=[pl.BlockSpec((tm, d), row), pl.BlockSpec((tm, d), row)],
        out_shape=[jax.ShapeDtypeStruct((t, d), F32), jax.ShapeDtypeStruct((t, d), BF16)],
        compiler_params=_params("arbitrary"),
        name="combine_ln",
    )(x1, yp, yp, yp, yp, gates, g.reshape(1, d), b.reshape(1, d))


def _ln_kernel(z_ref, g_ref, b_ref, o_ref):
    o_ref[...] = _ln_rows(z_ref[...], g_ref[...], b_ref[...])


def _ln(z, g, b):
    t, d = z.shape
    tm = min(LN_TM, t)
    return pl.pallas_call(
        _ln_kernel,
        grid=(t // tm,),
        in_specs=[pl.BlockSpec((tm, d), lambda i: (i, 0)),
                  pl.BlockSpec((1, d), lambda i: (0, 0)),
                  pl.BlockSpec((1, d), lambda i: (0, 0))],
        out_specs=pl.BlockSpec((tm, d), lambda i: (i, 0)),
        out_shape=jax.ShapeDtypeStruct((t, d), F32),
        compiler_params=_params("arbitrary"),
        name="layer_norm",
    )(z, g.reshape(1, d), b.reshape(1, d))


def _moe_kernel(be_ref, nu_ref, hf_ref, idx_hbm, x_hbm, wg_ref, wu_ref, bg_ref, bu_ref, wd_ref, bd_ref,
                y_hbm, idx_smem, xg, xb, act, ybuf, sem_idx, sem_g, sem_s, *, bm, ngu, ntok):
    del be_ref
    i = pl.program_id(0)
    j = pl.program_id(1)
    nu = nu_ref[0]
    active = i < nu
    d = xb.shape[1]

    def with_rows(item, fn):
        is_half = hf_ref[item] == 1

        @pl.when(is_half)
        def _():
            fn(bm // 2)

        @pl.when(jnp.logical_not(is_half))
        def _():
            fn(bm)

    half = d // 2
    nsl = half // LANES
    td = wd_ref.shape[2]

    def load_idx(item, slot):
        src = idx_hbm.at[pl.ds(pl.multiple_of(item * (2 * bm), 2 * bm), 2 * bm)]
        dst = idx_smem.at[pl.ds(pl.multiple_of(slot * (2 * bm), 2 * bm), 2 * bm)]
        cp = pltpu.make_async_copy(src, dst, sem_idx)
        cp.start()
        cp.wait()

    def for_each_row(slot, table, rows, fn):
        base = slot * (2 * bm) + table * bm

        def body(r, c):
            fn(idx_smem[base + r], r)
            return c
        lax.fori_loop(0, rows, body, 0, unroll=8)

    def vmem_slab(buf, r):
        return buf.at[pl.ds(pl.multiple_of(r * SLAB_PITCH, 8), nsl), :]

    def hbm_slab(ref, row):
        return ref.at[pl.ds(pl.multiple_of(row, 8), nsl), :]

    def start_gather(slot, rows):
        def one(src, r):
            pltpu.make_async_copy(hbm_slab(x_hbm, src), vmem_slab(xg, r), sem_g).start()
        for_each_row(slot, 0, rows, one)

    def wait_gather(rows):
        pltpu.make_async_copy(x_hbm.at[pl.ds(0, rows * nsl), :], xg.at[pl.ds(0, rows * nsl), :], sem_g).wait()

    def start_scatter(slot, rows):
        def one(dst, r):
            pltpu.make_async_copy(vmem_slab(ybuf, r), hbm_slab(y_hbm, dst), sem_s).start()
        for_each_row(slot, 1, rows, one)

    def wait_scatter(rows):
        pltpu.make_async_copy(ybuf.at[pl.ds(0, rows * nsl), :], y_hbm.at[pl.ds(0, rows * nsl), :], sem_s).wait()

    def unpack(rows):
        wait_gather(rows)
        for s in range(nsl):
            w = xg[pl.ds(s, rows, stride=SLAB_PITCH), :]
            xb[:rows, s * LANES:(s + 1) * LANES] = _unpack_lo(w).astype(BF16)
            xb[:rows, half + s * LANES:half + (s + 1) * LANES] = _unpack_hi(w).astype(BF16)

    @pl.when(active & (j == 0))
    def _():
        @pl.when(i == 0)
        def _():
            load_idx(0, 0)
            with_rows(0, lambda rows: start_gather(0, rows))
            ybuf[...] = jnp.zeros_like(ybuf)
            pad = pltpu.make_async_copy(ybuf.at[pl.ds(0, bm * nsl), :],
                                        y_hbm.at[pl.ds(TOP_K * ntok * nsl, bm * nsl), :], sem_s)
            pad.start()
            pad.wait()
        with_rows(i, unpack)

        @pl.when(i + 1 < nu)
        def _():
            nxt = lax.rem(i + 1, 2)
            load_idx(i + 1, nxt)
            with_rows(i + 1, lambda rows: start_gather(nxt, rows))

    def gate_up(rows):
        x = xb[:rows, :]
        g = jnp.dot(x, wg_ref[0], preferred_element_type=F32) + bg_ref[0]
        u = jnp.dot(x, wu_ref[0], preferred_element_type=F32) + bu_ref[0]
        g = jnp.minimum(g, SWIGLU_LIMIT)
        u = jnp.clip(u, -SWIGLU_LIMIT, SWIGLU_LIMIT)
        act[j, :rows, :] = (g * _sigmoid(SWIGLU_ALPHA * g) * (u + 1.0)).astype(BF16)

    @pl.when(active & (j < ngu))
    def _():
        with_rows(i, gate_up)

    def down(is_hi, rows):
        a = jnp.concatenate([act[q, :rows, :] for q in range(ngu)], axis=1)
        for c in range(td // MOE_DC):
            cs = slice(c * MOE_DC, (c + 1) * MOE_DC)
            y = jnp.dot(a, wd_ref[0, :, cs], preferred_element_type=F32) + bd_ref[0, :, cs]
            bits = _bf16_bits(y)
            for q in range(MOE_DC // LANES):
                s = c * (MOE_DC // LANES) + q
                piece = bits[:, q * LANES:(q + 1) * LANES]
                srows = pl.ds(s, rows, stride=SLAB_PITCH)
                if is_hi:
                    ybuf[srows, :] = ybuf[srows, :] | piece
                else:
                    ybuf[srows, :] = piece >> 16

    @pl.when(active & (j == ngu))
    def _():
        @pl.when(i > 0)
        def _():
            with_rows(i - 1, wait_scatter)
        with_rows(i, lambda rows: down(False, rows))

    @pl.when(active & (j == ngu + 1))
    def _():
        def finish(rows):
            down(True, rows)
            start_scatter(lax.rem(i, 2), rows)

            @pl.when(i == nu - 1)
            def _():
                wait_scatter(rows)
        with_rows(i, finish)


def _moe(xp, item_dst, block_expert, is_half, n_used, wgu, bgu, wd, bd, t, d):
    bm, tf, td = MOE_BM, MOE_TF, MOE_TD
    half = d // 2
    nsl = half // LANES
    assert td == half and td % MOE_DC == 0 and EXPERT_FF % tf == 0
    assert nsl % 8 == 0 and SLAB_PITCH % 8 == 0 and SLAB_PITCH >= nsl
    n_items = item_dst.shape[0]
    ngu = EXPERT_FF // tf
    nj = ngu + d // td
    idx3 = jnp.stack([(item_dst % t) * nsl, item_dst * nsl], axis=1).reshape(n_items * 2 * bm)
    bgu3 = bgu.reshape(N_EXPERTS, 1, 2 * EXPERT_FF)
    bd3 = bd.reshape(N_EXPERTS, 1, d)

    def step(i, j, nu):
        return jnp.where(i < nu[0], j, nj - 1)

    def gu(i, j, nu):
        return jnp.minimum(step(i, j, nu), ngu - 1)

    def dn(i, j, nu):
        return jnp.maximum(step(i, j, nu) - ngu, 0)

    grid_spec = pltpu.PrefetchScalarGridSpec(
        num_scalar_prefetch=3,
        grid=(n_items, nj),
        in_specs=[
            pl.BlockSpec(memory_space=pl.ANY),
            pl.BlockSpec(memory_space=pl.ANY),
            pl.BlockSpec((1, d, tf), lambda i, j, be, nu, hf: (be[i], 0, gu(i, j, nu))),
            pl.BlockSpec((1, d, tf), lambda i, j, be, nu, hf: (be[i], 0, gu(i, j, nu) + ngu)),
            pl.BlockSpec((1, 1, tf), lambda i, j, be, nu, hf: (be[i], 0, gu(i, j, nu))),
            pl.BlockSpec((1, 1, tf), lambda i, j, be, nu, hf: (be[i], 0, gu(i, j, nu) + ngu)),
            pl.BlockSpec((1, EXPERT_FF, td), lambda i, j, be, nu, hf: (be[i], 0, dn(i, j, nu))),
            pl.BlockSpec((1, 1, td), lambda i, j, be, nu, hf: (be[i], 0, dn(i, j, nu))),
        ],
        out_specs=pl.BlockSpec(memory_space=pl.ANY),
        scratch_shapes=[
            pltpu.SMEM((2 * 2 * bm,), jnp.int32),
            pltpu.VMEM((bm * SLAB_PITCH, LANES), jnp.uint32),
            pltpu.VMEM((bm, d), BF16),
            pltpu.VMEM((ngu, bm, tf), BF16),
            pltpu.VMEM((bm * SLAB_PITCH, LANES), jnp.uint32),
            pltpu.SemaphoreType.DMA(()),
            pltpu.SemaphoreType.DMA(()),
            pltpu.SemaphoreType.DMA(()),
        ],
    )
    return pl.pallas_call(
        functools.partial(_moe_kernel, bm=bm, ngu=ngu, ntok=t),
        grid_spec=grid_spec,
        out_shape=jax.ShapeDtypeStruct(((TOP_K * t + bm) * nsl, LANES), jnp.uint32),
        compiler_params=_params("arbitrary", "arbitrary"),
        name="moe_experts",
    )(block_expert, n_used, is_half, idx3, xp, wgu, wgu, bgu3, bgu3, wd, bd3)


def _routing_tables(top_idx, t):
    bm = MOE_BM
    hb = bm // 2
    tk = t * TOP_K
    n_items = tk // bm + N_EXPERTS
    n_rows = (tk // hb + N_EXPERTS) * hb + bm
    flat_e = top_idx.T.reshape(tk)
    experts = jnp.arange(N_EXPERTS, dtype=jnp.int32)
    order = jnp.argsort(flat_e, stable=True).astype(jnp.int32)
    counts = jnp.sum((flat_e[:, None] == experts[None, :]).astype(jnp.int32), axis=0)
    halves = (counts + hb - 1) // hb
    padded = halves * hb
    group_end = jnp.cumsum(counts)
    group_start = group_end - counts
    padded_end = jnp.cumsum(padded)
    padded_start = padded_end - padded
    items_per = (halves + 1) // 2
    item_end = jnp.cumsum(items_per)
    item_first = item_end - items_per

    def bucket(ends, v):
        e = jnp.sum((ends[None, :] <= v[:, None]).astype(jnp.int32), axis=1)
        return jnp.minimum(e, N_EXPERTS - 1)

    def lookup(table, e):
        return jnp.sum(jnp.where(e[:, None] == experts[None, :], table[None, :], 0), axis=1)

    rows = jnp.arange(n_rows, dtype=jnp.int32)
    row_e = bucket(padded_end, rows)
    pos = rows - lookup(padded_start - group_start, row_e)
    valid = pos < lookup(group_end, row_e)
    row_dst = jnp.where(valid, order[jnp.clip(pos, 0, tk - 1)], tk + rows % bm)

    n_used = item_end[-1].astype(jnp.int32)
    items = jnp.minimum(jnp.arange(n_items, dtype=jnp.int32), n_used - 1)
    item_e = bucket(item_end, items)
    local = items - lookup(item_first, item_e)
    is_half = (lookup(halves, item_e) - 2 * local == 1).astype(jnp.int32)
    first_row = lookup(padded_start, item_e) + local * bm
    item_dst = row_dst[first_row[:, None] + jnp.arange(bm, dtype=jnp.int32)[None, :]]
    return item_dst, item_e, is_half, n_used.reshape(1)


def kernel(x, p, w_in, attn_sinks, ret_norm_g, w_att_out, w_ret_out, w_out, ln1_g, ln1_b,
           w_router, b_router, w_gate_up, b_gate_up, w_down, b_down, ln2_g, ln2_b,
           w_ple, w_ple_gate, ln3_g, ln3_b):
    bsz, seq, d = x.shape
    t = bsz * seq
    assert seq % ATT_BLOCK == 0 and seq % RET_CHUNK == 0
    xf = x.reshape(t, d)
    for i in range(DEPTH):
        xb = xf.astype(BF16)
        w_in_i = w_in[i].astype(F32)
        off = 0
        segs = {}
        for name, width in (("aq", ATT_Q_WIDTH), ("akv", 2 * ATT_KV_WIDTH), ("rqk", 2 * RET_QK_WIDTH),
                            ("rv", RET_V_WIDTH), ("rg", RET_V_WIDTH), ("ga", d), ("gr", d)):
            segs[name] = _proj(xb, w_in_i, off, width, "proj_" + name)
            off += width
        y_att = _attention(segs["aq"], segs["akv"], attn_sinks[i].astype(F32), seq // ATT_BLOCK)
        y_ret = _retention(segs["rqk"], segs["rv"], segs["rg"], ret_norm_g[i].astype(F32),
                           bsz, seq // RET_CHUNK)
        merged = _merge(y_att, y_ret, w_att_out[i].astype(F32), w_ret_out[i].astype(F32),
                        segs["ga"], segs["gr"])
        z1 = _resid_mm(merged, w_out[i].astype(F32), xf)
        x1, x1p, gates, top_idx = _ln_router(z1, ln1_g[i], ln1_b[i], w_router[i], b_router[i])
        item_dst, block_expert, is_half, n_used = _routing_tables(top_idx[:, :TOP_K], t)
        yp = _moe(x1p, item_dst, block_expert, is_half, n_used, w_gate_up[i].astype(BF16),
                  b_gate_up[i].astype(F32), w_down[i].astype(BF16), b_down[i].astype(F32), t, d)
        x2, x2b = _combine_ln(x1, yp, gates, ln2_g[i], ln2_b[i])
        z3 = _ple(x2b, w_ple_gate[i].astype(F32), p[i].reshape(t, -1), w_ple[i].astype(F32), x2)
        xf = _ln(z3, ln3_g[i], ln3_b[i])
    return xf.reshape(bsz, seq, d)
```

```python
import functools
import math

import jax
import jax.numpy as jnp
import numpy as np
from jax import lax
from jax.experimental import pallas as pl
from jax.experimental.pallas import tpu as pltpu

F32 = jnp.float32
BF16 = jnp.bfloat16

ATT_HEADS = 32
ATT_KV_HEADS = 4
ATT_GROUP = ATT_HEADS // ATT_KV_HEADS
ATT_HEAD_DIM = 64
ATT_BLOCK = 128
RET_HEADS = 8
RET_KEY_DIM = 128
RET_VAL_DIM = 256
RET_CHUNK = 128
N_EXPERTS = 32
TOP_K = 4
EXPERT_FF = 1536
SWIGLU_LIMIT = 7.0
SWIGLU_ALPHA = 1.702
LN_EPS = 1e-5
DEPTH = 1
DEEPNORM_ALPHA = float((2 * DEPTH) ** 0.25)

ATT_Q_WIDTH = ATT_HEADS * ATT_HEAD_DIM
ATT_KV_WIDTH = ATT_KV_HEADS * ATT_HEAD_DIM
RET_QK_WIDTH = RET_HEADS * RET_KEY_DIM
RET_V_WIDTH = RET_HEADS * RET_VAL_DIM

LANES = 128
VMEM_LIMIT_BYTES = 56 * 1024 * 1024
MM_TM = 1024
MM_TN = 512
LN_TM = 256
COMBINE_TM = 128
MOE_BM = 512
MOE_TF = 512
MOE_TD = 2048
MOE_DC = 512
ROUTER_PAD = LANES
NEG_BIG = -1e30
SLAB_PITCH = 24
HI_MASK = 0xFFFF0000


def _params(*sem):
    return pltpu.CompilerParams(dimension_semantics=sem, vmem_limit_bytes=VMEM_LIMIT_BYTES)


def _sigmoid(x):
    return 1.0 / (1.0 + jnp.exp(-x))


def _bf16_bits(x):
    return lax.bitcast_convert_type(x.astype(BF16).astype(F32), jnp.uint32)


def _pack_words(lo, hi):
    return (_bf16_bits(lo) >> 16) | _bf16_bits(hi)


def _unpack_lo(w):
    return lax.bitcast_convert_type(w << 16, F32)


def _unpack_hi(w):
    return lax.bitcast_convert_type(w & jnp.uint32(HI_MASK), F32)


def _cast_at_first_row_tile(w_ref, wb_ref):
    @pl.when(pl.program_id(1) == 0)
    def _():
        wb_ref[...] = w_ref[...].astype(BF16)


def _proj_kernel(x_ref, w_ref, o_ref, wb_ref):
    _cast_at_first_row_tile(w_ref, wb_ref)
    o_ref[...] = jnp.dot(x_ref[...], wb_ref[...], preferred_element_type=F32).astype(o_ref.dtype)


def _proj(xb, w, col_off, ncols, name):
    m, k = xb.shape
    tm, tn = min(MM_TM, m), MM_TN
    assert m % tm == 0 and ncols % tn == 0 and col_off % tn == 0
    off = col_off // tn
    return pl.pallas_call(
        _proj_kernel,
        grid=(ncols // tn, m // tm),
        in_specs=[pl.BlockSpec((tm, k), lambda j, i: (i, 0)),
                  pl.BlockSpec((k, tn), lambda j, i: (0, j + off))],
        out_specs=pl.BlockSpec((tm, tn), lambda j, i: (i, j)),
        out_shape=jax.ShapeDtypeStruct((m, ncols), BF16),
        scratch_shapes=[pltpu.VMEM((k, tn), BF16)],
        compiler_params=_params("arbitrary", "arbitrary"),
        name=name,
    )(xb, w)


def _merge_kernel(ya_ref, yr_ref, wa_ref, wr_ref, ga_ref, gr_ref, o_ref, wab_ref, wrb_ref):
    _cast_at_first_row_tile(wa_ref, wab_ref)
    _cast_at_first_row_tile(wr_ref, wrb_ref)
    a = jnp.dot(ya_ref[...], wab_ref[...], preferred_element_type=F32)
    r = jnp.dot(yr_ref[...], wrb_ref[...], preferred_element_type=F32)
    ga = _sigmoid(ga_ref[...].astype(F32))
    gr = _sigmoid(gr_ref[...].astype(F32))
    o_ref[...] = (ga * a + gr * r).astype(o_ref.dtype)


def _merge(ya, yr, wa, wr, ga, gr):
    m, ka = ya.shape
    kr = yr.shape[1]
    n = wa.shape[1]
    tm, tn = min(MM_TM, m), MM_TN
    return pl.pallas_call(
        _merge_kernel,
        grid=(n // tn, m // tm),
        in_specs=[pl.BlockSpec((tm, ka), lambda j, i: (i, 0)),
                  pl.BlockSpec((tm, kr), lambda j, i: (i, 0)),
                  pl.BlockSpec((ka, tn), lambda j, i: (0, j)),
                  pl.BlockSpec((kr, tn), lambda j, i: (0, j)),
                  pl.BlockSpec((tm, tn), lambda j, i: (i, j)),
                  pl.BlockSpec((tm, tn), lambda j, i: (i, j))],
        out_specs=pl.BlockSpec((tm, tn), lambda j, i: (i, j)),
        out_shape=jax.ShapeDtypeStruct((m, n), BF16),
        scratch_shapes=[pltpu.VMEM((ka, tn), BF16), pltpu.VMEM((kr, tn), BF16)],
        compiler_params=_params("arbitrary", "arbitrary"),
        name="merge",
    )(ya, yr, wa, wr, ga, gr)


def _resid_mm_kernel(a_ref, w_ref, x_ref, o_ref, wb_ref):
    _cast_at_first_row_tile(w_ref, wb_ref)
    acc = jnp.dot(a_ref[...], wb_ref[...], preferred_element_type=F32)
    o_ref[...] = DEEPNORM_ALPHA * x_ref[...] + acc


def _resid_mm(a, w, x):
    m, k = a.shape
    n = w.shape[1]
    tm, tn = min(MM_TM, m), MM_TN
    return pl.pallas_call(
        _resid_mm_kernel,
        grid=(n // tn, m // tm),
        in_specs=[pl.BlockSpec((tm, k), lambda j, i: (i, 0)),
                  pl.BlockSpec((k, tn), lambda j, i: (0, j)),
                  pl.BlockSpec((tm, tn), lambda j, i: (i, j))],
        out_specs=pl.BlockSpec((tm, tn), lambda j, i: (i, j)),
        out_shape=jax.ShapeDtypeStruct((m, n), F32),
        scratch_shapes=[pltpu.VMEM((k, tn), BF16)],
        compiler_params=_params("arbitrary", "arbitrary"),
        name="resid_mm",
    )(a, w, x)


def _ple_kernel(xb_ref, wg_ref, p_ref, wp_ref, x_ref, o_ref, wgb_ref, wpb_ref):
    _cast_at_first_row_tile(wg_ref, wgb_ref)
    _cast_at_first_row_tile(wp_ref, wpb_ref)
    gate = jnp.dot(xb_ref[...], wgb_ref[...], preferred_element_type=F32)
    emb = jnp.dot(p_ref[...].astype(BF16), wpb_ref[...], preferred_element_type=F32)
    o_ref[...] = DEEPNORM_ALPHA * x_ref[...] + emb * _sigmoid(gate)


def _ple(xb, wg, p, wp, x):
    m, k = xb.shape
    n = wg.shape[1]
    kp = p.shape[1]
    tm, tn = min(MM_TM, m), MM_TN
    return pl.pallas_call(
        _ple_kernel,
        grid=(n // tn, m // tm),
        in_specs=[pl.BlockSpec((tm, k), lambda j, i: (i, 0)),
                  pl.BlockSpec((k, tn), lambda j, i: (0, j)),
                  pl.BlockSpec((tm, kp), lambda j, i: (i, 0)),
                  pl.BlockSpec((kp, tn), lambda j, i: (0, j)),
                  pl.BlockSpec((tm, tn), lambda j, i: (i, j))],
        out_specs=pl.BlockSpec((tm, tn), lambda j, i: (i, j)),
        out_shape=jax.ShapeDtypeStruct((m, n), F32),
        scratch_shapes=[pltpu.VMEM((k, tn), BF16), pltpu.VMEM((kp, tn), BF16)],
        compiler_params=_params("arbitrary", "arbitrary"),
        name="ple",
    )(xb, wg, p, wp, x)


def _alibi_slope(h):
    return float(2.0 ** (-8.0 * (h + 1) / ATT_HEADS))


def _attn_kernel(sink_ref, q_ref, kvc_ref, kvp_ref, o_ref, *, nblk):
    blk = ATT_BLOCK
    n = lax.rem(pl.program_id(0), nblk)
    has_prev = n > 0
    row = lax.broadcasted_iota(jnp.int32, (blk, 2 * blk), 0)
    col = lax.broadcasted_iota(jnp.int32, (blk, 2 * blk), 1)
    dist = blk + row - col
    valid = (dist >= 0) & (dist < blk) & ((col >= blk) | has_prev)
    distf = dist.astype(F32)
    lane = lax.broadcasted_iota(jnp.int32, (2 * blk, LANES), 1)
    lo = lane < ATT_HEAD_DIM
    scale = ATT_HEAD_DIM ** -0.5

    def halves(t, half):
        swapped = jnp.concatenate([t[:, ATT_HEAD_DIM:], t[:, :ATT_HEAD_DIM]], axis=1)
        zero = jnp.zeros_like(t)
        if half == 0:
            return jnp.where(lo, t, zero), jnp.where(lo, zero, swapped)
        return jnp.where(lo, swapped, zero), jnp.where(lo, zero, t)

    for kh in range(ATT_KV_HEADS):
        tile, half = kh // 2, kh % 2
        ks = slice(tile * LANES, (tile + 1) * LANES)
        vs = slice(ATT_KV_WIDTH + tile * LANES, ATT_KV_WIDTH + (tile + 1) * LANES)
        kcat = jnp.concatenate([kvp_ref[:, ks], kvc_ref[:, ks]], axis=0)
        vcat = jnp.concatenate([kvp_ref[:, vs], kvc_ref[:, vs]], axis=0)
        k_lo, k_hi = halves(kcat, half)
        v_lo, v_hi = halves(vcat, half)
        for jp in range(ATT_GROUP // 2):
            h0 = kh * ATT_GROUP + 2 * jp
            qs = slice(h0 * ATT_HEAD_DIM, h0 * ATT_HEAD_DIM + LANES)
            qp = q_ref[:, qs]
            acc = jnp.zeros((blk, LANES), F32)
            for par, (kx, vx) in enumerate(((k_lo, v_lo), (k_hi, v_hi))):
                h = h0 + par
                s = lax.dot_general(qp, kx, (((1,), (1,)), ((), ())), preferred_element_type=F32)
                s = s * scale - _alibi_slope(h) * distf
                s = jnp.where(valid, s, -jnp.inf)
                sink = sink_ref[h]
                m = jnp.maximum(jnp.max(s, axis=-1, keepdims=True), sink)
                e = jnp.exp(s - m)
                denom = jnp.sum(e, axis=-1, keepdims=True) + jnp.exp(sink - m)
                pv = jnp.dot(e.astype(BF16), vx, preferred_element_type=F32)
                acc = acc + pv * (1.0 / denom)
            o_ref[:, qs] = acc.astype(o_ref.dtype)


def _attention(aq, akv, sinks, nblk):
    t = aq.shape[0]
    blk = ATT_BLOCK
    return pl.pallas_call(
        functools.partial(_attn_kernel, nblk=nblk),
        grid=(t // blk,),
        in_specs=[pl.BlockSpec(memory_space=pltpu.SMEM),
                  pl.BlockSpec((blk, ATT_Q_WIDTH), lambda i: (i, 0)),
                  pl.BlockSpec((blk, 2 * ATT_KV_WIDTH), lambda i: (i, 0)),
                  pl.BlockSpec((blk, 2 * ATT_KV_WIDTH), lambda i: (jnp.maximum(i - 1, 0), 0))],
        out_specs=pl.BlockSpec((blk, ATT_Q_WIDTH), lambda i: (i, 0)),
        out_shape=jax.ShapeDtypeStruct((t, ATT_Q_WIDTH), BF16),
        compiler_params=_params("arbitrary"),
        name="swa_attention",
    )(sinks, aq, akv, akv)


def _ret_constants():
    c = RET_CHUNK
    gam = 1.0 - 2.0 ** (-5.0 - np.arange(RET_HEADS, dtype=np.float64))
    lg = np.log(gam)
    pos = np.arange(c, dtype=np.float64)
    diff = pos[:, None] - pos[None, :]
    kscale = RET_KEY_DIM ** -0.5
    inner = np.where(diff >= 0, np.exp(np.maximum(diff, 0.0) * lg[:, None, None]), 0.0) * kscale
    qdec = np.exp((pos + 1.0) * lg[:, None])
    kdec = np.exp((c - 1.0 - pos) * lg[:, None]) * kscale
    cdec = np.exp(c * lg)
    qdec_b = np.broadcast_to(qdec[:, :, None], (RET_HEADS, c, RET_VAL_DIM))
    kdec_b = np.broadcast_to(kdec[:, :, None], (RET_HEADS, c, RET_KEY_DIM))
    return (jnp.asarray(inner, F32), jnp.asarray(qdec_b, F32), jnp.asarray(kdec_b, F32),
            [float(v) for v in cdec])


def _ret_kernel(qk_ref, v_ref, g_ref, ng_ref, inner_ref, qdec_ref, kdec_ref, o_ref, state_ref, *, cdec):
    @pl.when(pl.program_id(1) == 0)
    def _():
        state_ref[...] = jnp.zeros_like(state_ref)

    dk, dv = RET_KEY_DIM, RET_VAL_DIM
    for h in range(RET_HEADS):
        q = qk_ref[:, h * dk:(h + 1) * dk]
        k = qk_ref[:, RET_QK_WIDTH + h * dk:RET_QK_WIDTH + (h + 1) * dk]
        v = v_ref[:, h * dv:(h + 1) * dv]
        st = state_ref[h]
        qk = lax.dot_general(q, k, (((1,), (1,)), ((), ())), preferred_element_type=F32)
        a = (qk * inner_ref[h]).astype(BF16)
        inner = jnp.dot(a, v, preferred_element_type=F32)
        cross = jnp.dot(q, st.astype(BF16), preferred_element_type=F32) * qdec_ref[h]
        kd = (k.astype(F32) * kdec_ref[h]).astype(BF16)
        upd = lax.dot_general(kd, v, (((0,), (0,)), ((), ())), preferred_element_type=F32)
        state_ref[h] = st * cdec[h] + upd
        o = inner + cross
        mu = jnp.mean(o, axis=-1, keepdims=True)
        d = o - mu
        var = jnp.mean(d * d, axis=-1, keepdims=True)
        y = d * lax.rsqrt(var + LN_EPS) * ng_ref[:, h * dv:(h + 1) * dv]
        g = g_ref[:, h * dv:(h + 1) * dv].astype(F32)
        o_ref[:, h * dv:(h + 1) * dv] = (g * _sigmoid(g) * y).astype(o_ref.dtype)


def _retention(rqk, rv, rg, norm_g, batch, nchunk):
    t = rqk.shape[0]
    c = RET_CHUNK
    inner, qdec, kdec, cdec = _ret_constants()
    row = lambda b, n: (b * nchunk + n, 0)
    const3 = lambda b, n: (0, 0, 0)
    return pl.pallas_call(
        functools.partial(_ret_kernel, cdec=cdec),
        grid=(batch, nchunk),
        in_specs=[pl.BlockSpec((c, 2 * RET_QK_WIDTH), row),
                  pl.BlockSpec((c, RET_V_WIDTH), row),
                  pl.BlockSpec((c, RET_V_WIDTH), row),
                  pl.BlockSpec((1, RET_V_WIDTH), lambda b, n: (0, 0)),
                  pl.BlockSpec((RET_HEADS, c, c), const3),
                  pl.BlockSpec((RET_HEADS, c, RET_VAL_DIM), const3),
                  pl.BlockSpec((RET_HEADS, c, RET_KEY_DIM), const3)],
        out_specs=pl.BlockSpec((c, RET_V_WIDTH), row),
        out_shape=jax.ShapeDtypeStruct((t, RET_V_WIDTH), BF16),
        scratch_shapes=[pltpu.VMEM((RET_HEADS, RET_KEY_DIM, RET_VAL_DIM), F32)],
        compiler_params=_params("arbitrary", "arbitrary"),
        name="retention",
    )(rqk, rv, rg, norm_g.reshape(1, RET_V_WIDTH), inner, qdec, kdec)


def _ln_rows(z, g, b):
    mu = jnp.mean(z, axis=-1, keepdims=True)
    d = z - mu
    var = jnp.mean(d * d, axis=-1, keepdims=True)
    return d * lax.rsqrt(var + LN_EPS) * g + b


def _ln_router_kernel(z_ref, g_ref, b_ref, wr_ref, br_ref, x_ref, xp_ref, gate_ref, idx_ref):
    x = _ln_rows(z_ref[...], g_ref[...], b_ref[...])
    x_ref[...] = x
    tm, d = x.shape
    half = d // 2
    nsl = half // LANES
    for s in range(nsl):
        cs = slice(s * LANES, (s + 1) * LANES)
        hs = slice(half + s * LANES, half + (s + 1) * LANES)
        xp_ref[pl.ds(s, tm, stride=nsl), :] = _pack_words(x[:, cs], x[:, hs])
    logits = jnp.dot(x, wr_ref[...], preferred_element_type=F32,
                     precision=lax.Precision.HIGHEST) + br_ref[...]
    lane = lax.broadcasted_iota(jnp.int32, logits.shape, 1)
    lane_f = lane.astype(F32)
    vals, idxs = [], []
    cur = logits
    for _ in range(TOP_K):
        m = jnp.max(cur, axis=-1, keepdims=True)
        idx_f = jnp.min(jnp.where(cur == m, lane_f, float(ROUTER_PAD)), axis=-1, keepdims=True)
        idx = idx_f.astype(jnp.int32)
        vals.append(m)
        idxs.append(idx)
        cur = jnp.where(lane == idx, -jnp.inf, cur)
    es = [jnp.exp(v - vals[0]) for v in vals]
    inv = 1.0 / (es[0] + es[1] + es[2] + es[3])
    gates = jnp.zeros(logits.shape, F32)
    ids = jnp.zeros(logits.shape, jnp.int32)
    for k in range(TOP_K):
        gates = jnp.where(lane == k, es[k] * inv, gates)
        ids = jnp.where(lane == k, idxs[k], ids)
    gate_ref[...] = gates
    idx_ref[...] = ids


def _ln_router(z, g, b, w_router, b_router):
    t, d = z.shape
    tm = min(LN_TM, t)
    wr = jnp.zeros((d, ROUTER_PAD), F32).at[:, :N_EXPERTS].set(w_router.astype(F32))
    br = jnp.full((1, ROUTER_PAD), NEG_BIG, F32).at[0, :N_EXPERTS].set(b_router.astype(F32))
    row = lambda i: (i, 0)
    const = lambda i: (0, 0)
    nsl = d // 2 // LANES
    return pl.pallas_call(
        _ln_router_kernel,
        grid=(t // tm,),
        in_specs=[pl.BlockSpec((tm, d), row),
                  pl.BlockSpec((1, d), const),
                  pl.BlockSpec((1, d), const),
                  pl.BlockSpec((d, ROUTER_PAD), const),
                  pl.BlockSpec((1, ROUTER_PAD), const)],
        out_specs=[pl.BlockSpec((tm, d), row),
                   pl.BlockSpec((tm * nsl, LANES), row),
                   pl.BlockSpec((tm, ROUTER_PAD), row),
                   pl.BlockSpec((tm, ROUTER_PAD), row)],
        out_shape=[jax.ShapeDtypeStruct((t, d), F32),
                   jax.ShapeDtypeStruct((t * nsl, LANES), jnp.uint32),
                   jax.ShapeDtypeStruct((t, ROUTER_PAD), F32),
                   jax.ShapeDtypeStruct((t, ROUTER_PAD), jnp.int32)],
        compiler_params=_params("arbitrary"),
        name="ln_router",
    )(z, g.reshape(1, d), b.reshape(1, d), wr, br)


def _combine_ln_kernel(x_ref, y0_ref, y1_ref, y2_ref, y3_ref, gate_ref, g_ref, b_ref, o_ref, ob_ref):
    tm, d = x_ref.shape
    nsl = d // 2 // LANES
    gt = gate_ref[...]
    gk = [jnp.broadcast_to(gt[:, k:k + 1], (tm, LANES)) for k in range(TOP_K)]
    los, his = [], []
    for s in range(nsl):
        lo = hi = None
        for k, y_ref in enumerate((y0_ref, y1_ref, y2_ref, y3_ref)):
            w = y_ref[pl.ds(s, tm, stride=nsl), :]
            tl, th = gk[k] * _unpack_lo(w), gk[k] * _unpack_hi(w)
            lo = tl if lo is None else lo + tl
            hi = th if hi is None else hi + th
        los.append(lo)
        his.append(hi)
    z = DEEPNORM_ALPHA * x_ref[...] + jnp.concatenate(los + his, axis=1)
    out = _ln_rows(z, g_ref[...], b_ref[...])
    o_ref[...] = out
    ob_ref[...] = out.astype(ob_ref.dtype)


def _combine_ln(x1, yp, gates, g, b):
    t, d = x1.shape
    tm = min(COMBINE_TM, t)
    nrb = t // tm
    nsl = d // 2 // LANES
    row = lambda i: (i, 0)
    const = lambda i: (0, 0)
    y_specs = [pl.BlockSpec((tm * nsl, LANES), functools.partial(lambda i, k: (k * nrb + i, 0), k=k))
               for k in range(TOP_K)]
    return pl.pallas_call(
        _combine_ln_kernel,
        grid=(nrb,),
        in_specs=[pl.BlockSpec((tm, d), row)] + y_specs +
                 [pl.BlockSpec((tm, ROUTER_PAD), row),
                  pl.BlockSpec((1, d), const),
                  pl.BlockSpec((1, d), const)],
        out_specs=[pl.BlockSpec((tm, d), row), pl.BlockSpec((tm, d), row)],
        out_shape=[jax.ShapeDtypeStruct((t, d), F32), jax.ShapeDtypeStruct((t, d), BF16)],
        compiler_params=_params("arbitrary"),
        name="combine_ln",
    )(x1, yp, yp, yp, yp, gates, g.reshape(1, d), b.reshape(1, d))


def _ln_kernel(z_ref, g_ref, b_ref, o_ref):
    o_ref[...] = _ln_rows(z_ref[...], g_ref[...], b_ref[...])


def _ln(z, g, b):
    t, d = z.shape
    tm = min(LN_TM, t)
    return pl.pallas_call(
        _ln_kernel,
        grid=(t // tm,),
        in_specs=[pl.BlockSpec((tm, d), lambda i: (i, 0)),
                  pl.BlockSpec((1, d), lambda i: (0, 0)),
                  pl.BlockSpec((1, d), lambda i: (0, 0))],
        out_specs=pl.BlockSpec((tm, d), lambda i: (i, 0)),
        out_shape=jax.ShapeDtypeStruct((t, d), F32),
        compiler_params=_params("arbitrary"),
        name="layer_norm",
    )(z, g.reshape(1, d), b.reshape(1, d))


def _moe_kernel(be_ref, nu_ref, hf_ref, idx_hbm, x_hbm, wg_ref, wu_ref, bg_ref, bu_ref, wd_ref, bd_ref,
                y_hbm, idx_smem, xg, xb, act, ybuf, sem_idx, sem_g, sem_s, *, bm, ngu, ntok):
    del be_ref
    i = pl.program_id(0)
    j = pl.program_id(1)
    nu = nu_ref[0]
    active = i < nu
    d = xb.shape[1]

    def with_rows(item, fn):
        is_half = hf_ref[item] == 1

        @pl.when(is_half)
        def _():
            fn(bm // 2)

        @pl.when(jnp.logical_not(is_half))
        def _():
            fn(bm)

    half = d // 2
    nsl = half // LANES
    td = wd_ref.shape[2]

    def load_idx(item, slot):
        src = idx_hbm.at[pl.ds(pl.multiple_of(item * (2 * bm), 2 * bm), 2 * bm)]
        dst = idx_smem.at[pl.ds(pl.multiple_of(slot * (2 * bm), 2 * bm), 2 * bm)]
        cp = pltpu.make_async_copy(src, dst, sem_idx)
        cp.start()
        cp.wait()

    def for_each_row(slot, table, rows, fn):
        base = slot * (2 * bm) + table * bm

        def body(r, c):
            fn(idx_smem[base + r], r)
            return c
        lax.fori_loop(0, rows, body, 0, unroll=8)

    def vmem_slab(buf, r):
        return buf.at[pl.ds(pl.multiple_of(r * SLAB_PITCH, 8), nsl), :]

    def hbm_slab(ref, row):
        return ref.at[pl.ds(pl.multiple_of(row, 8), nsl), :]

    def start_gather(slot, rows):
        def one(src, r):
            pltpu.make_async_copy(hbm_slab(x_hbm, src), vmem_slab(xg, r), sem_g).start()
        for_each_row(slot, 0, rows, one)

    def wait_gather(rows):
        pltpu.make_async_copy(x_hbm.at[pl.ds(0, rows * nsl), :], xg.at[pl.ds(0, rows * nsl), :], sem_g).wait()

    def start_scatter(slot, rows):
        def one(dst, r):
            pltpu.make_async_copy(vmem_slab(ybuf, r), hbm_slab(y_hbm, dst), sem_s).start()
        for_each_row(slot, 1, rows, one)

    def wait_scatter(rows):
        pltpu.make_async_copy(ybuf.at[pl.ds(0, rows * nsl), :], y_hbm.at[pl.ds(0, rows * nsl), :], sem_s).wait()

    def unpack(rows):
        wait_gather(rows)
        for s in range(nsl):
            w = xg[pl.ds(s, rows, stride=SLAB_PITCH), :]
            xb[:rows, s * LANES:(s + 1) * LANES] = _unpack_lo(w).astype(BF16)
            xb[:rows, half + s * LANES:half + (s + 1) * LANES] = _unpack_hi(w).astype(BF16)

    @pl.when(active & (j == 0))
    def _():
        @pl.when(i == 0)
        def _():
            load_idx(0, 0)
            with_rows(0, lambda rows: start_gather(0, rows))
            ybuf[...] = jnp.zeros_like(ybuf)
            pad = pltpu.make_async_copy(ybuf.at[pl.ds(0, bm * nsl), :],
                                        y_hbm.at[pl.ds(TOP_K * ntok * nsl, bm * nsl), :], sem_s)
            pad.start()
            pad.wait()
        with_rows(i, unpack)

        @pl.when(i + 1 < nu)
        def _():
            nxt = lax.rem(i + 1, 2)
            load_idx(i + 1, nxt)
            with_rows(i + 1, lambda rows: start_gather(nxt, rows))

    def gate_up(rows):
        x = xb[:rows, :]
        g = jnp.dot(x, wg_ref[0], preferred_element_type=F32) + bg_ref[0]
        u = jnp.dot(x, wu_ref[0], preferred_element_type=F32) + bu_ref[0]
        g = jnp.minimum(g, SWIGLU_LIMIT)
        u = jnp.clip(u, -SWIGLU_LIMIT, SWIGLU_LIMIT)
        act[j, :rows, :] = (g * _sigmoid(SWIGLU_ALPHA * g) * (u + 1.0)).astype(BF16)

    @pl.when(active & (j < ngu))
    def _():
        with_rows(i, gate_up)

    def down(is_hi, rows):
        a = jnp.concatenate([act[q, :rows, :] for q in range(ngu)], axis=1)
        for c in range(td // MOE_DC):
            cs = slice(c * MOE_DC, (c + 1) * MOE_DC)
            y = jnp.dot(a, wd_ref[0, :, cs], preferred_element_type=F32) + bd_ref[0, :, cs]
            bits = _bf16_bits(y)
            for q in range(MOE_DC // LANES):
                s = c * (MOE_DC // LANES) + q
                piece = bits[:, q * LANES:(q + 1) * LANES]
                srows = pl.ds(s, rows, stride=SLAB_PITCH)
                if is_hi:
                    ybuf[srows, :] = ybuf[srows, :] | piece
                else:
                    ybuf[srows, :] = piece >> 16

    @pl.when(active & (j == ngu))
    def _():
        @pl.when(i > 0)
        def _():
            with_rows(i - 1, wait_scatter)
        with_rows(i, lambda rows: down(False, rows))

    @pl.when(active & (j == ngu + 1))
    def _():
        def finish(rows):
            down(True, rows)
            start_scatter(lax.rem(i, 2), rows)

            @pl.when(i == nu - 1)
            def _():
                wait_scatter(rows)
        with_rows(i, finish)


def _moe(xp, item_dst, block_expert, is_half, n_used, wgu, bgu, wd, bd, t, d):
    bm, tf, td = MOE_BM, MOE_TF, MOE_TD
    half = d // 2
    nsl = half // LANES
    assert td == half and td % MOE_DC == 0 and EXPERT_FF % tf == 0
    assert nsl % 8 == 0 and SLAB_PITCH % 8 == 0 and SLAB_PITCH >= nsl
    n_items = item_dst.shape[0]
    ngu = EXPERT_FF // tf
    nj = ngu + d // td
    idx3 = jnp.stack([(item_dst % t) * nsl, item_dst * nsl], axis=1).reshape(n_items * 2 * bm)
    bgu3 = bgu.reshape(N_EXPERTS, 1, 2 * EXPERT_FF)
    bd3 = bd.reshape(N_EXPERTS, 1, d)

    def step(i, j, nu):
        return jnp.where(i < nu[0], j, nj - 1)

    def gu(i, j, nu):
        return jnp.minimum(step(i, j, nu), ngu - 1)

    def dn(i, j, nu):
        return jnp.maximum(step(i, j, nu) - ngu, 0)

    grid_spec = pltpu.PrefetchScalarGridSpec(
        num_scalar_prefetch=3,
        grid=(n_items, nj),
        in_specs=[
            pl.BlockSpec(memory_space=pl.ANY),
            pl.BlockSpec(memory_space=pl.ANY),
            pl.BlockSpec((1, d, tf), lambda i, j, be, nu, hf: (be[i], 0, gu(i, j, nu))),
            pl.BlockSpec((1, d, tf), lambda i, j, be, nu, hf: (be[i], 0, gu(i, j, nu) + ngu)),
            pl.BlockSpec((1, 1, tf), lambda i, j, be, nu, hf: (be[i], 0, gu(i, j, nu))),
            pl.BlockSpec((1, 1, tf), lambda i, j, be, nu, hf: (be[i], 0, gu(i, j, nu) + ngu)),
            pl.BlockSpec((1, EXPERT_FF, td), lambda i, j, be, nu, hf: (be[i], 0, dn(i, j, nu))),
            pl.BlockSpec((1, 1, td), lambda i, j, be, nu, hf: (be[i], 0, dn(i, j, nu))),
        ],
        out_specs=pl.BlockSpec(memory_space=pl.ANY),
        scratch_shapes=[
            pltpu.SMEM((2 * 2 * bm,), jnp.int32),
            pltpu.VMEM((bm * SLAB_PITCH, LANES), jnp.uint32),
            pltpu.VMEM((bm, d), BF16),
            pltpu.VMEM((ngu, bm, tf), BF16),
            pltpu.VMEM((bm * SLAB_PITCH, LANES), jnp.uint32),
            pltpu.SemaphoreType.DMA(()),
            pltpu.SemaphoreType.DMA(()),
            pltpu.SemaphoreType.DMA(()),
        ],
    )
    return pl.pallas_call(
        functools.partial(_moe_kernel, bm=bm, ngu=ngu, ntok=t),
        grid_spec=grid_spec,
        out_shape=jax.ShapeDtypeStruct(((TOP_K * t + bm) * nsl, LANES), jnp.uint32),
        compiler_params=_params("arbitrary", "arbitrary"),
        name="moe_experts",
    )(block_expert, n_used, is_half, idx3, xp, wgu, wgu, bgu3, bgu3, wd, bd3)


def _with_rows(hf_ref, item, bm, fn):
    is_half = hf_ref[item] == 1

    @pl.when(is_half)
    def _():
        fn(bm // 2)

    @pl.when(jnp.logical_not(is_half))
    def _():
        fn(bm)


def _index_table_copy(idx_hbm, idx_smem, sem, item, slot, bm):
    src = idx_hbm.at[pl.ds(pl.multiple_of(item * (2 * bm), 2 * bm), 2 * bm)]
    dst = idx_smem.at[pl.ds(pl.multiple_of(slot * (2 * bm), 2 * bm), 2 * bm)]
    return pltpu.make_async_copy(src, dst, sem)


def _for_each_row(idx_smem, slot, table, bm, rows, fn):
    base = slot * (2 * bm) + table * bm

    def body(r, c):
        fn(idx_smem[base + r], r)
        return c
    lax.fori_loop(0, rows, body, 0, unroll=8)


def _dispatch_kernel(nu_ref, hf_ref, idx_hbm, x_hbm, o_ref, idx_smem, xg, sem_idx, sem_g, *, bm):
    i = pl.program_id(0)
    nu = nu_ref[0]
    d = o_ref.shape[1]
    half = d // 2
    nsl = half // LANES

    def start_gather(slot, rows):
        def one(src, r):
            pltpu.make_async_copy(x_hbm.at[pl.ds(pl.multiple_of(src, 8), nsl), :],
                                  xg.at[pl.ds(pl.multiple_of(r * SLAB_PITCH, 8), nsl), :], sem_g).start()
        _for_each_row(idx_smem, slot, 0, bm, rows, one)

    def unpack(rows):
        pltpu.make_async_copy(x_hbm.at[pl.ds(0, rows * nsl), :], xg.at[pl.ds(0, rows * nsl), :], sem_g).wait()
        for s in range(nsl):
            w = xg[pl.ds(s, rows, stride=SLAB_PITCH), :]
            o_ref[:rows, s * LANES:(s + 1) * LANES] = _unpack_lo(w).astype(BF16)
            o_ref[:rows, half + s * LANES:half + (s + 1) * LANES] = _unpack_hi(w).astype(BF16)
        if rows < bm:
            o_ref[rows:, :] = jnp.zeros((bm - rows, d), BF16)

    @pl.when(i < nu)
    def _():
        @pl.when(i == 0)
        def _():
            first = _index_table_copy(idx_hbm, idx_smem, sem_idx, 0, 0, bm)
            first.start()
            first.wait()
            _with_rows(hf_ref, 0, bm, lambda rows: start_gather(0, rows))

        nxt = lax.rem(i + 1, 2)

        @pl.when(i + 1 < nu)
        def _():
            _index_table_copy(idx_hbm, idx_smem, sem_idx, i + 1, nxt, bm).start()

        _with_rows(hf_ref, i, bm, unpack)

        @pl.when(i + 1 < nu)
        def _():
            _index_table_copy(idx_hbm, idx_smem, sem_idx, i + 1, nxt, bm).wait()
            _with_rows(hf_ref, i + 1, bm, lambda rows: start_gather(nxt, rows))

    @pl.when(i >= nu)
    def _():
        o_ref[...] = jnp.zeros_like(o_ref)


def _dispatch(xp, idx3, is_half, n_used, n_items, d):
    bm = MOE_BM
    grid_spec = pltpu.PrefetchScalarGridSpec(
        num_scalar_prefetch=2,
        grid=(n_items,),
        in_specs=[pl.BlockSpec(memory_space=pl.ANY), pl.BlockSpec(memory_space=pl.ANY)],
        out_specs=pl.BlockSpec((bm, d), lambda i, nu, hf: (i, 0)),
        scratch_shapes=[pltpu.SMEM((2 * 2 * bm,), jnp.int32),
                        pltpu.VMEM((bm * SLAB_PITCH, LANES), jnp.uint32),
                        pltpu.SemaphoreType.DMA(()),
                        pltpu.SemaphoreType.DMA(())],
    )
    return pl.pallas_call(
        functools.partial(_dispatch_kernel, bm=bm),
        grid_spec=grid_spec,
        out_shape=jax.ShapeDtypeStruct((n_items * bm, d), BF16),
        compiler_params=_params("arbitrary"),
        name="moe_dispatch",
    )(n_used, is_half, idx3, xp)


def _gate_up_kernel(be_ref, nu_ref, hf_ref, x_ref, wg_ref, wu_ref, bg_ref, bu_ref, a_ref, wgb, wub, *, bm):
    i = pl.program_id(1)
    nu = nu_ref[0]
    prev = jnp.maximum(i - 1, 0)
    new_expert = (i == 0) | (be_ref[i] != be_ref[prev])

    @pl.when((i < nu) & new_expert)
    def _():
        wgb[...] = wg_ref[0].astype(BF16)
        wub[...] = wu_ref[0].astype(BF16)

    def gate_up(rows):
        x = x_ref[:rows, :]
        g = jnp.dot(x, wgb[...], preferred_element_type=F32) + bg_ref[0]
        u = jnp.dot(x, wub[...], preferred_element_type=F32) + bu_ref[0]
        g = jnp.minimum(g, SWIGLU_LIMIT)
        u = jnp.clip(u, -SWIGLU_LIMIT, SWIGLU_LIMIT)
        a_ref[:rows, :] = (g * _sigmoid(SWIGLU_ALPHA * g) * (u + 1.0)).astype(BF16)
        if rows < bm:
            a_ref[rows:, :] = jnp.zeros((bm - rows, a_ref.shape[1]), BF16)

    @pl.when(i < nu)
    def _():
        _with_rows(hf_ref, i, bm, gate_up)

    @pl.when(i >= nu)
    def _():
        a_ref[...] = jnp.zeros_like(a_ref)


def _gate_up(xs, block_expert, is_half, n_used, wgu, bgu, n_items, d):
    bm, tf = MOE_BM, MOE_TF
    ngu = EXPERT_FF // tf
    bgu3 = bgu.reshape(N_EXPERTS, 1, 2 * EXPERT_FF)

    def item(i, nu):
        return jnp.minimum(i, nu[0] - 1)

    grid_spec = pltpu.PrefetchScalarGridSpec(
        num_scalar_prefetch=3,
        grid=(ngu, n_items),
        in_specs=[
            pl.BlockSpec((bm, d), lambda c, i, be, nu, hf: (item(i, nu), 0)),
            pl.BlockSpec((1, d, tf), lambda c, i, be, nu, hf: (be[i], 0, c)),
            pl.BlockSpec((1, d, tf), lambda c, i, be, nu, hf: (be[i], 0, c + ngu)),
            pl.BlockSpec((1, 1, tf), lambda c, i, be, nu, hf: (be[i], 0, c)),
            pl.BlockSpec((1, 1, tf), lambda c, i, be, nu, hf: (be[i], 0, c + ngu)),
        ],
        out_specs=pl.BlockSpec((bm, tf), lambda c, i, be, nu, hf: (i, c)),
        scratch_shapes=[pltpu.VMEM((d, tf), BF16), pltpu.VMEM((d, tf), BF16)],
    )
    return pl.pallas_call(
        functools.partial(_gate_up_kernel, bm=bm),
        grid_spec=grid_spec,
        out_shape=jax.ShapeDtypeStruct((n_items * bm, EXPERT_FF), BF16),
        compiler_params=_params("arbitrary", "arbitrary"),
        name="moe_gate_up",
    )(block_expert, n_used, is_half, xs, wgu, wgu, bgu3, bgu3)


def _down_kernel(be_ref, nu_ref, hf_ref, idx_hbm, a_ref, wd_ref, bd_ref, y_hbm,
                 idx_smem, ybuf, sem_idx, sem_s, *, bm, ntok):
    del be_ref
    i = pl.program_id(0)
    j = pl.program_id(1)
    nu = nu_ref[0]
    active = i < nu
    td = wd_ref.shape[2]
    nsl = td // LANES
    slot = lax.rem(i, 2)

    def start_scatter(rows):
        def one(dst, r):
            pltpu.make_async_copy(ybuf.at[pl.ds(pl.multiple_of(r * SLAB_PITCH, 8), nsl), :],
                                  y_hbm.at[pl.ds(pl.multiple_of(dst, 8), nsl), :], sem_s).start()
        _for_each_row(idx_smem, slot, 1, bm, rows, one)

    def wait_scatter(rows):
        pltpu.make_async_copy(ybuf.at[pl.ds(0, rows * nsl), :], y_hbm.at[pl.ds(0, rows * nsl), :], sem_s).wait()

    def down(is_hi, rows):
        a = a_ref[:rows, :]
        for c in range(td // MOE_DC):
            cs = slice(c * MOE_DC, (c + 1) * MOE_DC)
            y = jnp.dot(a, wd_ref[0, :, cs], preferred_element_type=F32) + bd_ref[0, :, cs]
            bits = _bf16_bits(y)
            for q in range(MOE_DC // LANES):
                s = c * (MOE_DC // LANES) + q
                piece = bits[:, q * LANES:(q + 1) * LANES]
                srows = pl.ds(s, rows, stride=SLAB_PITCH)
                if is_hi:
                    ybuf[srows, :] = ybuf[srows, :] | piece
                else:
                    ybuf[srows, :] = piece >> 16

    @pl.when(active & (j == 0))
    def _():
        @pl.when(i == 0)
        def _():
            ybuf[...] = jnp.zeros_like(ybuf)
            pad = pltpu.make_async_copy(ybuf.at[pl.ds(0, bm * nsl), :],
                                        y_hbm.at[pl.ds(TOP_K * ntok * nsl, bm * nsl), :], sem_s)
            pad.start()
            pad.wait()

        _index_table_copy(idx_hbm, idx_smem, sem_idx, i, slot, bm).start()

        @pl.when(i > 0)
        def _():
            _with_rows(hf_ref, i - 1, bm, wait_scatter)
        _with_rows(hf_ref, i, bm, lambda rows: down(False, rows))

    @pl.when(active & (j == 1))
    def _():
        def finish(rows):
            down(True, rows)
            _index_table_copy(idx_hbm, idx_smem, sem_idx, i, slot, bm).wait()
            start_scatter(rows)

            @pl.when(i == nu - 1)
            def _():
                wait_scatter(rows)
        _with_rows(hf_ref, i, bm, finish)


def _down_scatter(act, idx3, block_expert, is_half, n_used, wd, bd, n_items, t, d):
    bm, td = MOE_BM, MOE_TD
    half = d // 2
    nsl = half // LANES
    assert td == half and td % MOE_DC == 0
    assert nsl % 8 == 0 and SLAB_PITCH % 8 == 0 and SLAB_PITCH >= nsl
    bd3 = bd.reshape(N_EXPERTS, 1, d)

    def item(i, nu):
        return jnp.minimum(i, nu[0] - 1)

    def chunk(i, j, nu):
        return jnp.where(i < nu[0], j, d // td - 1)

    grid_spec = pltpu.PrefetchScalarGridSpec(
        num_scalar_prefetch=3,
        grid=(n_items, d // td),
        in_specs=[
            pl.BlockSpec(memory_space=pl.ANY),
            pl.BlockSpec((bm, EXPERT_FF), lambda i, j, be, nu, hf: (item(i, nu), 0)),
            pl.BlockSpec((1, EXPERT_FF, td), lambda i, j, be, nu, hf: (be[i], 0, chunk(i, j, nu))),
            pl.BlockSpec((1, 1, td), lambda i, j, be, nu, hf: (be[i], 0, chunk(i, j, nu))),
        ],
        out_specs=pl.BlockSpec(memory_space=pl.ANY),
        scratch_shapes=[pltpu.SMEM((2 * 2 * bm,), jnp.int32),
                        pltpu.VMEM((bm * SLAB_PITCH, LANES), jnp.uint32),
                        pltpu.SemaphoreType.DMA(()),
                        pltpu.SemaphoreType.DMA(())],
    )
    return pl.pallas_call(
        functools.partial(_down_kernel, bm=bm, ntok=t),
        grid_spec=grid_spec,
        out_shape=jax.ShapeDtypeStruct(((TOP_K * t + bm) * nsl, LANES), jnp.uint32),
        compiler_params=_params("arbitrary", "arbitrary"),
        name="moe_down_scatter",
    )(block_expert, n_used, is_half, idx3, act, wd, bd3)


def _moe_split(xp, item_dst, block_expert, is_half, n_used, wgu, bgu, wd, bd, t, d):
    n_items = item_dst.shape[0]
    nsl = d // 2 // LANES
    idx3 = jnp.stack([(item_dst % t) * nsl, item_dst * nsl], axis=1).reshape(n_items * 2 * MOE_BM)
    xs = _dispatch(xp, idx3, is_half, n_used, n_items, d)
    act = _gate_up(xs, block_expert, is_half, n_used, wgu, bgu, n_items, d)
    return _down_scatter(act, idx3, block_expert, is_half, n_used, wd, bd, n_items, t, d)


def _routing_tables(top_idx, t):
    bm = MOE_BM
    hb = bm // 2
    tk = t * TOP_K
    n_items = tk // bm + N_EXPERTS
    n_rows = (tk // hb + N_EXPERTS) * hb + bm
    flat_e = top_idx.T.reshape(tk)
    experts = jnp.arange(N_EXPERTS, dtype=jnp.int32)
    order = jnp.argsort(flat_e, stable=True).astype(jnp.int32)
    counts = jnp.sum((flat_e[:, None] == experts[None, :]).astype(jnp.int32), axis=0)
    halves = (counts + hb - 1) // hb
    padded = halves * hb
    group_end = jnp.cumsum(counts)
    group_start = group_end - counts
    padded_end = jnp.cumsum(padded)
    padded_start = padded_end - padded
    items_per = (halves + 1) // 2
    item_end = jnp.cumsum(items_per)
    item_first = item_end - items_per

    def bucket(ends, v):
        e = jnp.sum((ends[None, :] <= v[:, None]).astype(jnp.int32), axis=1)
        return jnp.minimum(e, N_EXPERTS - 1)

    def lookup(table, e):
        return jnp.sum(jnp.where(e[:, None] == experts[None, :], table[None, :], 0), axis=1)

    rows = jnp.arange(n_rows, dtype=jnp.int32)
    row_e = bucket(padded_end, rows)
    pos = rows - lookup(padded_start - group_start, row_e)
    valid = pos < lookup(group_end, row_e)
    row_dst = jnp.where(valid, order[jnp.clip(pos, 0, tk - 1)], tk + rows % bm)

    n_used = item_end[-1].astype(jnp.int32)
    items = jnp.minimum(jnp.arange(n_items, dtype=jnp.int32), n_used - 1)
    item_e = bucket(item_end, items)
    local = items - lookup(item_first, item_e)
    is_half = (lookup(halves, item_e) - 2 * local == 1).astype(jnp.int32)
    first_row = lookup(padded_start, item_e) + local * bm
    item_dst = row_dst[first_row[:, None] + jnp.arange(bm, dtype=jnp.int32)[None, :]]
    return item_dst, item_e, is_half, n_used.reshape(1)


def kernel(x, p, w_in, attn_sinks, ret_norm_g, w_att_out, w_ret_out, w_out, ln1_g, ln1_b,
           w_router, b_router, w_gate_up, b_gate_up, w_down, b_down, ln2_g, ln2_b,
           w_ple, w_ple_gate, ln3_g, ln3_b):
    bsz, seq, d = x.shape
    t = bsz * seq
    assert seq % ATT_BLOCK == 0 and seq % RET_CHUNK == 0
    xf = x.reshape(t, d)
    for i in range(DEPTH):
        xb = xf.astype(BF16)
        w_in_i = w_in[i].astype(F32)
        off = 0
        segs = {}
        for name, width in (("aq", ATT_Q_WIDTH), ("akv", 2 * ATT_KV_WIDTH), ("rqk", 2 * RET_QK_WIDTH),
                            ("rv", RET_V_WIDTH), ("rg", RET_V_WIDTH), ("ga", d), ("gr", d)):
            segs[name] = _proj(xb, w_in_i, off, width, "proj_" + name)
            off += width
        y_att = _attention(segs["aq"], segs["akv"], attn_sinks[i].astype(F32), seq // ATT_BLOCK)
        y_ret = _retention(segs["rqk"], segs["rv"], segs["rg"], ret_norm_g[i].astype(F32),
                           bsz, seq // RET_CHUNK)
        merged = _merge(y_att, y_ret, w_att_out[i].astype(F32), w_ret_out[i].astype(F32),
                        segs["ga"], segs["gr"])
        z1 = _resid_mm(merged, w_out[i].astype(F32), xf)
        x1, x1p, gates, top_idx = _ln_router(z1, ln1_g[i], ln1_b[i], w_router[i], b_router[i])
        item_dst, block_expert, is_half, n_used = _routing_tables(top_idx[:, :TOP_K], t)
        yp = _moe_split(x1p, item_dst, block_expert, is_half, n_used, w_gate_up[i].astype(F32),
                        b_gate_up[i].astype(F32), w_down[i].astype(BF16), b_down[i].astype(F32), t, d)
        x2, x2b = _combine_ln(x1, yp, gates, ln2_g[i], ln2_b[i])
        z3 = _ple(x2b, w_ple_gate[i].astype(F32), p[i].reshape(t, -1), w_ple[i].astype(F32), x2)
        xf = _ln(z3, ln3_g[i], ln3_b[i])
    return xf.reshape(bsz, seq, d)
```

```python
import functools
import math

import jax
import jax.numpy as jnp
import numpy as np
from jax import lax
from jax.experimental import pallas as pl
from jax.experimental.pallas import tpu as pltpu

F32 = jnp.float32
BF16 = jnp.bfloat16

ATT_HEADS = 32
ATT_KV_HEADS = 4
ATT_GROUP = ATT_HEADS // ATT_KV_HEADS
ATT_HEAD_DIM = 64
ATT_BLOCK = 128
RET_HEADS = 8
RET_KEY_DIM = 128
RET_VAL_DIM = 256
RET_CHUNK = 128
N_EXPERTS = 32
TOP_K = 4
EXPERT_FF = 1536
SWIGLU_LIMIT = 7.0
SWIGLU_ALPHA = 1.702
LN_EPS = 1e-5
DEPTH = 1
DEEPNORM_ALPHA = float((2 * DEPTH) ** 0.25)

ATT_Q_WIDTH = ATT_HEADS * ATT_HEAD_DIM
ATT_KV_WIDTH = ATT_KV_HEADS * ATT_HEAD_DIM
RET_QK_WIDTH = RET_HEADS * RET_KEY_DIM
RET_V_WIDTH = RET_HEADS * RET_VAL_DIM

LANES = 128
VMEM_LIMIT_BYTES = 56 * 1024 * 1024
MM_TM = 1024
MM_TN = 512
LN_TM = 256
COMBINE_TM = 128
MOE_BM = 512
MOE_TF = 512
MOE_TD = 2048
MOE_DC = 512
ROUTER_PAD = LANES
NEG_BIG = -1e30
SLAB_PITCH = 24
HI_MASK = 0xFFFF0000


def _params(*sem):
    return pltpu.CompilerParams(dimension_semantics=sem, vmem_limit_bytes=VMEM_LIMIT_BYTES)


def _sigmoid(x):
    return 1.0 / (1.0 + jnp.exp(-x))


def _bf16_bits(x):
    return lax.bitcast_convert_type(x.astype(BF16).astype(F32), jnp.uint32)


def _pack_words(lo, hi):
    return (_bf16_bits(lo) >> 16) | _bf16_bits(hi)


def _unpack_lo(w):
    return lax.bitcast_convert_type(w << 16, F32)


def _unpack_hi(w):
    return lax.bitcast_convert_type(w & jnp.uint32(HI_MASK), F32)


def _cast_at_first_row_tile(w_ref, wb_ref):
    @pl.when(pl.program_id(1) == 0)
    def _():
        wb_ref[...] = w_ref[...].astype(BF16)


def _proj_kernel(x_ref, w_ref, o_ref, wb_ref):
    _cast_at_first_row_tile(w_ref, wb_ref)
    o_ref[...] = jnp.dot(x_ref[...], wb_ref[...], preferred_element_type=F32).astype(o_ref.dtype)


def _proj(xb, w, col_off, ncols, name):
    m, k = xb.shape
    tm, tn = min(MM_TM, m), MM_TN
    assert m % tm == 0 and ncols % tn == 0 and col_off % tn == 0
    off = col_off // tn
    return pl.pallas_call(
        _proj_kernel,
        grid=(ncols // tn, m // tm),
        in_specs=[pl.BlockSpec((tm, k), lambda j, i: (i, 0)),
                  pl.BlockSpec((k, tn), lambda j, i: (0, j + off))],
        out_specs=pl.BlockSpec((tm, tn), lambda j, i: (i, j)),
        out_shape=jax.ShapeDtypeStruct((m, ncols), BF16),
        scratch_shapes=[pltpu.VMEM((k, tn), BF16)],
        compiler_params=_params("arbitrary", "arbitrary"),
        name=name,
    )(xb, w)


def _merge_kernel(ya_ref, yr_ref, wa_ref, wr_ref, ga_ref, gr_ref, o_ref, wab_ref, wrb_ref):
    _cast_at_first_row_tile(wa_ref, wab_ref)
    _cast_at_first_row_tile(wr_ref, wrb_ref)
    a = jnp.dot(ya_ref[...], wab_ref[...], preferred_element_type=F32)
    r = jnp.dot(yr_ref[...], wrb_ref[...], preferred_element_type=F32)
    ga = _sigmoid(ga_ref[...].astype(F32))
    gr = _sigmoid(gr_ref[...].astype(F32))
    o_ref[...] = (ga * a + gr * r).astype(o_ref.dtype)


def _merge(ya, yr, wa, wr, ga, gr):
    m, ka = ya.shape
    kr = yr.shape[1]
    n = wa.shape[1]
    tm, tn = min(MM_TM, m), MM_TN
    return pl.pallas_call(
        _merge_kernel,
        grid=(n // tn, m // tm),
        in_specs=[pl.BlockSpec((tm, ka), lambda j, i: (i, 0)),
                  pl.BlockSpec((tm, kr), lambda j, i: (i, 0)),
                  pl.BlockSpec((ka, tn), lambda j, i: (0, j)),
                  pl.BlockSpec((kr, tn), lambda j, i: (0, j)),
                  pl.BlockSpec((tm, tn), lambda j, i: (i, j)),
                  pl.BlockSpec((tm, tn), lambda j, i: (i, j))],
        out_specs=pl.BlockSpec((tm, tn), lambda j, i: (i, j)),
        out_shape=jax.ShapeDtypeStruct((m, n), BF16),
        scratch_shapes=[pltpu.VMEM((ka, tn), BF16), pltpu.VMEM((kr, tn), BF16)],
        compiler_params=_params("arbitrary", "arbitrary"),
        name="merge",
    )(ya, yr, wa, wr, ga, gr)


def _resid_mm_kernel(a_ref, w_ref, x_ref, o_ref, wb_ref):
    _cast_at_first_row_tile(w_ref, wb_ref)
    acc = jnp.dot(a_ref[...], wb_ref[...], preferred_element_type=F32)
    o_ref[...] = DEEPNORM_ALPHA * x_ref[...] + acc


def _resid_mm(a, w, x):
    m, k = a.shape
    n = w.shape[1]
    tm, tn = min(MM_TM, m), MM_TN
    return pl.pallas_call(
        _resid_mm_kernel,
        grid=(n // tn, m // tm),
        in_specs=[pl.BlockSpec((tm, k), lambda j, i: (i, 0)),
                  pl.BlockSpec((k, tn), lambda j, i: (0, j)),
                  pl.BlockSpec((tm, tn), lambda j, i: (i, j))],
        out_specs=pl.BlockSpec((tm, tn), lambda j, i: (i, j)),
        out_shape=jax.ShapeDtypeStruct((m, n), F32),
        scratch_shapes=[pltpu.VMEM((k, tn), BF16)],
        compiler_params=_params("arbitrary", "arbitrary"),
        name="resid_mm",
    )(a, w, x)


def _ple_kernel(xb_ref, wg_ref, p_ref, wp_ref, x_ref, o_ref, wgb_ref, wpb_ref):
    _cast_at_first_row_tile(wg_ref, wgb_ref)
    _cast_at_first_row_tile(wp_ref, wpb_ref)
    gate = jnp.dot(xb_ref[...], wgb_ref[...], preferred_element_type=F32)
    emb = jnp.dot(p_ref[...].astype(BF16), wpb_ref[...], preferred_element_type=F32)
    o_ref[...] = DEEPNORM_ALPHA * x_ref[...] + emb * _sigmoid(gate)


def _ple(xb, wg, p, wp, x):
    m, k = xb.shape
    n = wg.shape[1]
    kp = p.shape[1]
    tm, tn = min(MM_TM, m), MM_TN
    return pl.pallas_call(
        _ple_kernel,
        grid=(n // tn, m // tm),
        in_specs=[pl.BlockSpec((tm, k), lambda j, i: (i, 0)),
                  pl.BlockSpec((k, tn), lambda j, i: (0, j)),
                  pl.BlockSpec((tm, kp), lambda j, i: (i, 0)),
                  pl.BlockSpec((kp, tn), lambda j, i: (0, j)),
                  pl.BlockSpec((tm, tn), lambda j, i: (i, j))],
        out_specs=pl.BlockSpec((tm, tn), lambda j, i: (i, j)),
        out_shape=jax.ShapeDtypeStruct((m, n), F32),
        scratch_shapes=[pltpu.VMEM((k, tn), BF16), pltpu.VMEM((kp, tn), BF16)],
        compiler_params=_params("arbitrary", "arbitrary"),
        name="ple",
    )(xb, wg, p, wp, x)


def _alibi_slope(h):
    return float(2.0 ** (-8.0 * (h + 1) / ATT_HEADS))


def _attn_kernel(sink_ref, q_ref, kvc_ref, kvp_ref, o_ref, *, nblk):
    blk = ATT_BLOCK
    n = lax.rem(pl.program_id(0), nblk)
    has_prev = n > 0
    row = lax.broadcasted_iota(jnp.int32, (blk, 2 * blk), 0)
    col = lax.broadcasted_iota(jnp.int32, (blk, 2 * blk), 1)
    dist = blk + row - col
    valid = (dist >= 0) & (dist < blk) & ((col >= blk) | has_prev)
    distf = dist.astype(F32)
    lane = lax.broadcasted_iota(jnp.int32, (2 * blk, LANES), 1)
    lo = lane < ATT_HEAD_DIM
    scale = ATT_HEAD_DIM ** -0.5

    def halves(t, half):
        swapped = jnp.concatenate([t[:, ATT_HEAD_DIM:], t[:, :ATT_HEAD_DIM]], axis=1)
        zero = jnp.zeros_like(t)
        if half == 0:
            return jnp.where(lo, t, zero), jnp.where(lo, zero, swapped)
        return jnp.where(lo, swapped, zero), jnp.where(lo, zero, t)

    for kh in range(ATT_KV_HEADS):
        tile, half = kh // 2, kh % 2
        ks = slice(tile * LANES, (tile + 1) * LANES)
        vs = slice(ATT_KV_WIDTH + tile * LANES, ATT_KV_WIDTH + (tile + 1) * LANES)
        kcat = jnp.concatenate([kvp_ref[:, ks], kvc_ref[:, ks]], axis=0)
        vcat = jnp.concatenate([kvp_ref[:, vs], kvc_ref[:, vs]], axis=0)
        k_lo, k_hi = halves(kcat, half)
        v_lo, v_hi = halves(vcat, half)
        for jp in range(ATT_GROUP // 2):
            h0 = kh * ATT_GROUP + 2 * jp
            qs = slice(h0 * ATT_HEAD_DIM, h0 * ATT_HEAD_DIM + LANES)
            qp = q_ref[:, qs]
            acc = jnp.zeros((blk, LANES), F32)
            for par, (kx, vx) in enumerate(((k_lo, v_lo), (k_hi, v_hi))):
                h = h0 + par
                s = lax.dot_general(qp, kx, (((1,), (1,)), ((), ())), preferred_element_type=F32)
                s = s * scale - _alibi_slope(h) * distf
                s = jnp.where(valid, s, -jnp.inf)
                sink = sink_ref[h]
                m = jnp.maximum(jnp.max(s, axis=-1, keepdims=True), sink)
                e = jnp.exp(s - m)
                denom = jnp.sum(e, axis=-1, keepdims=True) + jnp.exp(sink - m)
                pv = jnp.dot(e.astype(BF16), vx, preferred_element_type=F32)
                acc = acc + pv * (1.0 / denom)
            o_ref[:, qs] = acc.astype(o_ref.dtype)


def _attention(aq, akv, sinks, nblk):
    t = aq.shape[0]
    blk = ATT_BLOCK
    return pl.pallas_call(
        functools.partial(_attn_kernel, nblk=nblk),
        grid=(t // blk,),
        in_specs=[pl.BlockSpec(memory_space=pltpu.SMEM),
                  pl.BlockSpec((blk, ATT_Q_WIDTH), lambda i: (i, 0)),
                  pl.BlockSpec((blk, 2 * ATT_KV_WIDTH), lambda i: (i, 0)),
                  pl.BlockSpec((blk, 2 * ATT_KV_WIDTH), lambda i: (jnp.maximum(i - 1, 0), 0))],
        out_specs=pl.BlockSpec((blk, ATT_Q_WIDTH), lambda i: (i, 0)),
        out_shape=jax.ShapeDtypeStruct((t, ATT_Q_WIDTH), BF16),
        compiler_params=_params("arbitrary"),
        name="swa_attention",
    )(sinks, aq, akv, akv)


def _ret_constants():
    c = RET_CHUNK
    gam = 1.0 - 2.0 ** (-5.0 - np.arange(RET_HEADS, dtype=np.float64))
    lg = np.log(gam)
    pos = np.arange(c, dtype=np.float64)
    diff = pos[:, None] - pos[None, :]
    kscale = RET_KEY_DIM ** -0.5
    inner = np.where(diff >= 0, np.exp(np.maximum(diff, 0.0) * lg[:, None, None]), 0.0) * kscale
    qdec = np.exp((pos + 1.0) * lg[:, None])
    kdec = np.exp((c - 1.0 - pos) * lg[:, None]) * kscale
    cdec = np.exp(c * lg)
    qdec_b = np.broadcast_to(qdec[:, :, None], (RET_HEADS, c, RET_VAL_DIM))
    kdec_b = np.broadcast_to(kdec[:, :, None], (RET_HEADS, c, RET_KEY_DIM))
    return (jnp.asarray(inner, F32), jnp.asarray(qdec_b, F32), jnp.asarray(kdec_b, F32),
            [float(v) for v in cdec])


def _ret_kernel(qk_ref, v_ref, g_ref, ng_ref, inner_ref, qdec_ref, kdec_ref, o_ref, state_ref, *, cdec):
    @pl.when(pl.program_id(1) == 0)
    def _():
        state_ref[...] = jnp.zeros_like(state_ref)

    dk, dv = RET_KEY_DIM, RET_VAL_DIM
    for h in range(RET_HEADS):
        q = qk_ref[:, h * dk:(h + 1) * dk]
        k = qk_ref[:, RET_QK_WIDTH + h * dk:RET_QK_WIDTH + (h + 1) * dk]
        v = v_ref[:, h * dv:(h + 1) * dv]
        st = state_ref[h]
        qk = lax.dot_general(q, k, (((1,), (1,)), ((), ())), preferred_element_type=F32)
        a = (qk * inner_ref[h]).astype(BF16)
        inner = jnp.dot(a, v, preferred_element_type=F32)
        cross = jnp.dot(q, st.astype(BF16), preferred_element_type=F32) * qdec_ref[h]
        kd = (k.astype(F32) * kdec_ref[h]).astype(BF16)
        upd = lax.dot_general(kd, v, (((0,), (0,)), ((), ())), preferred_element_type=F32)
        state_ref[h] = st * cdec[h] + upd
        o = inner + cross
        mu = jnp.mean(o, axis=-1, keepdims=True)
        d = o - mu
        var = jnp.mean(d * d, axis=-1, keepdims=True)
        y = d * lax.rsqrt(var + LN_EPS) * ng_ref[:, h * dv:(h + 1) * dv]
        g = g_ref[:, h * dv:(h + 1) * dv].astype(F32)
        o_ref[:, h * dv:(h + 1) * dv] = (g * _sigmoid(g) * y).astype(o_ref.dtype)


def _retention(rqk, rv, rg, norm_g, batch, nchunk):
    t = rqk.shape[0]
    c = RET_CHUNK
    inner, qdec, kdec, cdec = _ret_constants()
    row = lambda b, n: (b * nchunk + n, 0)
    const3 = lambda b, n: (0, 0, 0)
    return pl.pallas_call(
        functools.partial(_ret_kernel, cdec=cdec),
        grid=(batch, nchunk),
        in_specs=[pl.BlockSpec((c, 2 * RET_QK_WIDTH), row),
                  pl.BlockSpec((c, RET_V_WIDTH), row),
                  pl.BlockSpec((c, RET_V_WIDTH), row),
                  pl.BlockSpec((1, RET_V_WIDTH), lambda b, n: (0, 0)),
                  pl.BlockSpec((RET_HEADS, c, c), const3),
                  pl.BlockSpec((RET_HEADS, c, RET_VAL_DIM), const3),
                  pl.BlockSpec((RET_HEADS, c, RET_KEY_DIM), const3)],
        out_specs=pl.BlockSpec((c, RET_V_WIDTH), row),
        out_shape=jax.ShapeDtypeStruct((t, RET_V_WIDTH), BF16),
        scratch_shapes=[pltpu.VMEM((RET_HEADS, RET_KEY_DIM, RET_VAL_DIM), F32)],
        compiler_params=_params("arbitrary", "arbitrary"),
        name="retention",
    )(rqk, rv, rg, norm_g.reshape(1, RET_V_WIDTH), inner, qdec, kdec)


def _ln_rows(z, g, b):
    mu = jnp.mean(z, axis=-1, keepdims=True)
    d = z - mu
    var = jnp.mean(d * d, axis=-1, keepdims=True)
    return d * lax.rsqrt(var + LN_EPS) * g + b


def _ln_router_kernel(z_ref, g_ref, b_ref, wr_ref, br_ref, x_ref, xp_ref, gate_ref, idx_ref):
    x = _ln_rows(z_ref[...], g_ref[...], b_ref[...])
    x_ref[...] = x
    tm, d = x.shape
    half = d // 2
    nsl = half // LANES
    for s in range(nsl):
        cs = slice(s * LANES, (s + 1) * LANES)
        hs = slice(half + s * LANES, half + (s + 1) * LANES)
        xp_ref[pl.ds(s, tm, stride=nsl), :] = _pack_words(x[:, cs], x[:, hs])
    logits = jnp.dot(x, wr_ref[...], preferred_element_type=F32,
                     precision=lax.Precision.HIGHEST) + br_ref[...]
    lane = lax.broadcasted_iota(jnp.int32, logits.shape, 1)
    lane_f = lane.astype(F32)
    vals, idxs = [], []
    cur = logits
    for _ in range(TOP_K):
        m = jnp.max(cur, axis=-1, keepdims=True)
        idx_f = jnp.min(jnp.where(cur == m, lane_f, float(ROUTER_PAD)), axis=-1, keepdims=True)
        idx = idx_f.astype(jnp.int32)
        vals.append(m)
        idxs.append(idx)
        cur = jnp.where(lane == idx, -jnp.inf, cur)
    es = [jnp.exp(v - vals[0]) for v in vals]
    inv = 1.0 / (es[0] + es[1] + es[2] + es[3])
    gates = jnp.zeros(logits.shape, F32)
    ids = jnp.zeros(logits.shape, jnp.int32)
    for k in range(TOP_K):
        gates = jnp.where(lane == k, es[k] * inv, gates)
        ids = jnp.where(lane == k, idxs[k], ids)
    gate_ref[...] = gates
    idx_ref[...] = ids


def _ln_router(z, g, b, w_router, b_router):
    t, d = z.shape
    tm = min(LN_TM, t)
    wr = jnp.zeros((d, ROUTER_PAD), F32).at[:, :N_EXPERTS].set(w_router.astype(F32))
    br = jnp.full((1, ROUTER_PAD), NEG_BIG, F32).at[0, :N_EXPERTS].set(b_router.astype(F32))
    row = lambda i: (i, 0)
    const = lambda i: (0, 0)
    nsl = d // 2 // LANES
    return pl.pallas_call(
        _ln_router_kernel,
        grid=(t // tm,),
        in_specs=[pl.BlockSpec((tm, d), row),
                  pl.BlockSpec((1, d), const),
                  pl.BlockSpec((1, d), const),
                  pl.BlockSpec((d, ROUTER_PAD), const),
                  pl.BlockSpec((1, ROUTER_PAD), const)],
        out_specs=[pl.BlockSpec((tm, d), row),
                   pl.BlockSpec((tm * nsl, LANES), row),
                   pl.BlockSpec((tm, ROUTER_PAD), row),
                   pl.BlockSpec((tm, ROUTER_PAD), row)],
        out_shape=[jax.ShapeDtypeStruct((t, d), F32),
                   jax.ShapeDtypeStruct((t * nsl, LANES), jnp.uint32),
                   jax.ShapeDtypeStruct((t, ROUTER_PAD), F32),
                   jax.ShapeDtypeStruct((t, ROUTER_PAD), jnp.int32)],
        compiler_params=_params("arbitrary"),
        name="ln_router",
    )(z, g.reshape(1, d), b.reshape(1, d), wr, br)


def _combine_ln_kernel(x_ref, y0_ref, y1_ref, y2_ref, y3_ref, gate_ref, g_ref, b_ref, o_ref, ob_ref):
    tm, d = x_ref.shape
    nsl = d // 2 // LANES
    gt = gate_ref[...]
    gk = [jnp.broadcast_to(gt[:, k:k + 1], (tm, LANES)) for k in range(TOP_K)]
    los, his = [], []
    for s in range(nsl):
        lo = hi = None
        for k, y_ref in enumerate((y0_ref, y1_ref, y2_ref, y3_ref)):
            w = y_ref[pl.ds(s, tm, stride=nsl), :]
            tl, th = gk[k] * _unpack_lo(w), gk[k] * _unpack_hi(w)
            lo = tl if lo is None else lo + tl
            hi = th if hi is None else hi + th
        los.append(lo)
        his.append(hi)
    z = DEEPNORM_ALPHA * x_ref[...] + jnp.concatenate(los + his, axis=1)
    out = _ln_rows(z, g_ref[...], b_ref[...])
    o_ref[...] = out
    ob_ref[...] = out.astype(ob_ref.dtype)


def _combine_ln(x1, yp, gates, g, b):
    t, d = x1.shape
    tm = min(COMBINE_TM, t)
    nrb = t // tm
    nsl = d // 2 // LANES
    row = lambda i: (i, 0)
    const = lambda i: (0, 0)
    y_specs = [pl.BlockSpec((tm * nsl, LANES), functools.partial(lambda i, k: (k * nrb + i, 0), k=k))
               for k in range(TOP_K)]
    return pl.pallas_call(
        _combine_ln_kernel,
        grid=(nrb,),
        in_specs=[pl.BlockSpec((tm, d), row)] + y_specs +
                 [pl.BlockSpec((tm, ROUTER_PAD), row),
                  pl.BlockSpec((1, d), const),
                  pl.BlockSpec((1, d), const)],
        out_specs=[pl.BlockSpec((tm, d), row), pl.BlockSpec((tm, d), row)],
        out_shape=[jax.ShapeDtypeStruct((t, d), F32), jax.ShapeDtypeStruct((t, d), BF16)],
        compiler_params=_params("arbitrary"),
        name="combine_ln",
    )(x1, yp, yp, yp, yp, gates, g.reshape(1, d), b.reshape(1, d))


def _ln_kernel(z_ref, g_ref, b_ref, o_ref):
    o_ref[...] = _ln_rows(z_ref[...], g_ref[...], b_ref[...])


def _ln(z, g, b):
    t, d = z.shape
    tm = min(LN_TM, t)
    return pl.pallas_call(
        _ln_kernel,
        grid=(t // tm,),
        in_specs=[pl.BlockSpec((tm, d), lambda i: (i, 0)),
                  pl.BlockSpec((1, d), lambda i: (0, 0)),
                  pl.BlockSpec((1, d), lambda i: (0, 0))],
        out_specs=pl.BlockSpec((tm, d), lambda i: (i, 0)),
        out_shape=jax.ShapeDtypeStruct((t, d), F32),
        compiler_params=_params("arbitrary"),
        name="layer_norm",
    )(z, g.reshape(1, d), b.reshape(1, d))


def _moe_kernel(be_ref, nu_ref, hf_ref, idx_hbm, x_hbm, wg_ref, wu_ref, bg_ref, bu_ref, wd_ref, bd_ref,
                y_hbm, idx_smem, xg, xb, act, ybuf, sem_idx, sem_g, sem_s, *, bm, ngu, ntok):
    del be_ref
    i = pl.program_id(0)
    j = pl.program_id(1)
    nu = nu_ref[0]
    active = i < nu
    d = xb.shape[1]

    def with_rows(item, fn):
        is_half = hf_ref[item] == 1

        @pl.when(is_half)
        def _():
            fn(bm // 2)

        @pl.when(jnp.logical_not(is_half))
        def _():
            fn(bm)

    half = d // 2
    nsl = half // LANES
    td = wd_ref.shape[2]

    def idx_copy(item, slot):
        src = idx_hbm.at[pl.ds(pl.multiple_of(item * (2 * bm), 2 * bm), 2 * bm)]
        dst = idx_smem.at[pl.ds(pl.multiple_of(slot * (2 * bm), 2 * bm), 2 * bm)]
        return pltpu.make_async_copy(src, dst, sem_idx)

    def for_each_row(slot, table, rows, fn):
        base = slot * (2 * bm) + table * bm

        def body(r, c):
            fn(idx_smem[base + r], r)
            return c
        lax.fori_loop(0, rows, body, 0, unroll=8)

    def vmem_slab(buf, r):
        return buf.at[pl.ds(pl.multiple_of(r * SLAB_PITCH, 8), nsl), :]

    def hbm_slab(ref, row):
        return ref.at[pl.ds(pl.multiple_of(row, 8), nsl), :]

    def start_gather(slot, rows):
        def one(src, r):
            pltpu.make_async_copy(hbm_slab(x_hbm, src), vmem_slab(xg, r), sem_g).start()
        for_each_row(slot, 0, rows, one)

    def wait_gather(rows):
        pltpu.make_async_copy(x_hbm.at[pl.ds(0, rows * nsl), :], xg.at[pl.ds(0, rows * nsl), :], sem_g).wait()

    def start_scatter(slot, rows):
        def one(dst, r):
            pltpu.make_async_copy(vmem_slab(ybuf, r), hbm_slab(y_hbm, dst), sem_s).start()
        for_each_row(slot, 1, rows, one)

    def wait_scatter(rows):
        pltpu.make_async_copy(ybuf.at[pl.ds(0, rows * nsl), :], y_hbm.at[pl.ds(0, rows * nsl), :], sem_s).wait()

    def unpack(rows):
        wait_gather(rows)
        for s in range(nsl):
            w = xg[pl.ds(s, rows, stride=SLAB_PITCH), :]
            xb[:rows, s * LANES:(s + 1) * LANES] = _unpack_lo(w).astype(BF16)
            xb[:rows, half + s * LANES:half + (s + 1) * LANES] = _unpack_hi(w).astype(BF16)

    @pl.when(active & (j == 0))
    def _():
        @pl.when(i == 0)
        def _():
            idx_copy(0, 0).start()
            idx_copy(0, 0).wait()
            with_rows(0, lambda rows: start_gather(0, rows))
            ybuf[...] = jnp.zeros_like(ybuf)
            pad = pltpu.make_async_copy(ybuf.at[pl.ds(0, bm * nsl), :],
                                        y_hbm.at[pl.ds(TOP_K * ntok * nsl, bm * nsl), :], sem_s)
            pad.start()
            pad.wait()
        nxt = lax.rem(i + 1, 2)

        @pl.when(i + 1 < nu)
        def _():
            idx_copy(i + 1, nxt).start()

        with_rows(i, unpack)

        @pl.when(i + 1 < nu)
        def _():
            idx_copy(i + 1, nxt).wait()
            with_rows(i + 1, lambda rows: start_gather(nxt, rows))

    def gate_up(rows):
        x = xb[:rows, :]
        g = jnp.dot(x, wg_ref[0], preferred_element_type=F32) + bg_ref[0]
        u = jnp.dot(x, wu_ref[0], preferred_element_type=F32) + bu_ref[0]
        g = jnp.minimum(g, SWIGLU_LIMIT)
        u = jnp.clip(u, -SWIGLU_LIMIT, SWIGLU_LIMIT)
        act[j, :rows, :] = (g * _sigmoid(SWIGLU_ALPHA * g) * (u + 1.0)).astype(BF16)

    @pl.when(active & (j < ngu))
    def _():
        with_rows(i, gate_up)

    def down(is_hi, rows):
        a = jnp.concatenate([act[q, :rows, :] for q in range(ngu)], axis=1)
        for c in range(td // MOE_DC):
            cs = slice(c * MOE_DC, (c + 1) * MOE_DC)
            y = jnp.dot(a, wd_ref[0, :, cs], preferred_element_type=F32) + bd_ref[0, :, cs]
            bits = _bf16_bits(y)
            for q in range(MOE_DC // LANES):
                s = c * (MOE_DC // LANES) + q
                piece = bits[:, q * LANES:(q + 1) * LANES]
                srows = pl.ds(s, rows, stride=SLAB_PITCH)
                if is_hi:
                    ybuf[srows, :] = ybuf[srows, :] | piece
                else:
                    ybuf[srows, :] = piece >> 16

    @pl.when(active & (j == ngu))
    def _():
        @pl.when(i > 0)
        def _():
            with_rows(i - 1, wait_scatter)
        with_rows(i, lambda rows: down(False, rows))

    @pl.when(active & (j == ngu + 1))
    def _():
        def finish(rows):
            down(True, rows)
            start_scatter(lax.rem(i, 2), rows)

            @pl.when(i == nu - 1)
            def _():
                wait_scatter(rows)
        with_rows(i, finish)


def _moe(xp, item_dst, block_expert, is_half, n_used, wgu, bgu, wd, bd, t, d):
    bm, tf, td = MOE_BM, MOE_TF, MOE_TD
    half = d // 2
    nsl = half // LANES
    assert td == half and td % MOE_DC == 0 and EXPERT_FF % tf == 0
    assert nsl % 8 == 0 and SLAB_PITCH % 8 == 0 and SLAB_PITCH >= nsl
    n_items = item_dst.shape[0]
    ngu = EXPERT_FF // tf
    nj = ngu + d // td
    idx3 = jnp.stack([(item_dst % t) * nsl, item_dst * nsl], axis=1).reshape(n_items * 2 * bm)
    bgu3 = bgu.reshape(N_EXPERTS, 1, 2 * EXPERT_FF)
    bd3 = bd.reshape(N_EXPERTS, 1, d)

    def step(i, j, nu):
        return jnp.where(i < nu[0], j, nj - 1)

    def gu(i, j, nu):
        return jnp.minimum(step(i, j, nu), ngu - 1)

    def dn(i, j, nu):
        return jnp.maximum(step(i, j, nu) - ngu, 0)

    def wd_block(i, j, be, nu):
        in_down = step(i, j, nu) >= ngu
        e = jnp.where(in_down, be[i], be[jnp.maximum(i - 1, 0)])
        return (e, 0, jnp.where(in_down, dn(i, j, nu), d // td - 1))

    grid_spec = pltpu.PrefetchScalarGridSpec(
        num_scalar_prefetch=3,
        grid=(n_items, nj),
        in_specs=[
            pl.BlockSpec(memory_space=pl.ANY),
            pl.BlockSpec(memory_space=pl.ANY),
            pl.BlockSpec((1, d, tf), lambda i, j, be, nu, hf: (be[i], 0, gu(i, j, nu))),
            pl.BlockSpec((1, d, tf), lambda i, j, be, nu, hf: (be[i], 0, gu(i, j, nu) + ngu)),
            pl.BlockSpec((1, 1, tf), lambda i, j, be, nu, hf: (be[i], 0, gu(i, j, nu))),
            pl.BlockSpec((1, 1, tf), lambda i, j, be, nu, hf: (be[i], 0, gu(i, j, nu) + ngu)),
            pl.BlockSpec((1, EXPERT_FF, td), lambda i, j, be, nu, hf: wd_block(i, j, be, nu)),
            pl.BlockSpec((1, 1, td), lambda i, j, be, nu, hf: (be[i], 0, dn(i, j, nu))),
        ],
        out_specs=pl.BlockSpec(memory_space=pl.ANY),
        scratch_shapes=[
            pltpu.SMEM((2 * 2 * bm,), jnp.int32),
            pltpu.VMEM((bm * SLAB_PITCH, LANES), jnp.uint32),
            pltpu.VMEM((bm, d), BF16),
            pltpu.VMEM((ngu, bm, tf), BF16),
            pltpu.VMEM((bm * SLAB_PITCH, LANES), jnp.uint32),
            pltpu.SemaphoreType.DMA(()),
            pltpu.SemaphoreType.DMA(()),
            pltpu.SemaphoreType.DMA(()),
        ],
    )
    return pl.pallas_call(
        functools.partial(_moe_kernel, bm=bm, ngu=ngu, ntok=t),
        grid_spec=grid_spec,
        out_shape=jax.ShapeDtypeStruct(((TOP_K * t + bm) * nsl, LANES), jnp.uint32),
        compiler_params=_params("arbitrary", "arbitrary"),
        name="moe_experts",
    )(block_expert, n_used, is_half, idx3, xp, wgu, wgu, bgu3, bgu3, wd, bd3)


def _routing_tables(top_idx, t):
    bm = MOE_BM
    hb = bm // 2
    tk = t * TOP_K
    n_items = tk // bm + N_EXPERTS
    n_rows = (tk // hb + N_EXPERTS) * hb + bm
    flat_e = top_idx.T.reshape(tk)
    experts = jnp.arange(N_EXPERTS, dtype=jnp.int32)
    order = jnp.argsort(flat_e, stable=True).astype(jnp.int32)
    counts = jnp.sum((flat_e[:, None] == experts[None, :]).astype(jnp.int32), axis=0)
    halves = (counts + hb - 1) // hb
    padded = halves * hb
    group_end = jnp.cumsum(counts)
    group_start = group_end - counts
    padded_end = jnp.cumsum(padded)
    padded_start = padded_end - padded
    items_per = (halves + 1) // 2
    item_end = jnp.cumsum(items_per)
    item_first = item_end - items_per

    def bucket(ends, v):
        e = jnp.sum((ends[None, :] <= v[:, None]).astype(jnp.int32), axis=1)
        return jnp.minimum(e, N_EXPERTS - 1)

    def lookup(table, e):
        return jnp.sum(jnp.where(e[:, None] == experts[None, :], table[None, :], 0), axis=1)

    rows = jnp.arange(n_rows, dtype=jnp.int32)
    row_e = bucket(padded_end, rows)
    pos = rows - lookup(padded_start - group_start, row_e)
    valid = pos < lookup(group_end, row_e)
    row_dst = jnp.where(valid, order[jnp.clip(pos, 0, tk - 1)], tk + rows % bm)

    n_used = item_end[-1].astype(jnp.int32)
    items = jnp.minimum(jnp.arange(n_items, dtype=jnp.int32), n_used - 1)
    item_e = bucket(item_end, items)
    local = items - lookup(item_first, item_e)
    is_half = (lookup(halves, item_e) - 2 * local == 1).astype(jnp.int32)
    first_row = lookup(padded_start, item_e) + local * bm
    item_dst = row_dst[first_row[:, None] + jnp.arange(bm, dtype=jnp.int32)[None, :]]
    return item_dst, item_e, is_half, n_used.reshape(1)


def kernel(x, p, w_in, attn_sinks, ret_norm_g, w_att_out, w_ret_out, w_out, ln1_g, ln1_b,
           w_router, b_router, w_gate_up, b_gate_up, w_down, b_down, ln2_g, ln2_b,
           w_ple, w_ple_gate, ln3_g, ln3_b):
    bsz, seq, d = x.shape
    t = bsz * seq
    assert seq % ATT_BLOCK == 0 and seq % RET_CHUNK == 0
    xf = x.reshape(t, d)
    for i in range(DEPTH):
        xb = xf.astype(BF16)
        w_in_i = w_in[i].astype(F32)
        off = 0
        segs = {}
        for name, width in (("aq", ATT_Q_WIDTH), ("akv", 2 * ATT_KV_WIDTH), ("rqk", 2 * RET_QK_WIDTH),
                            ("rv", RET_V_WIDTH), ("rg", RET_V_WIDTH), ("ga", d), ("gr", d)):
            segs[name] = _proj(xb, w_in_i, off, width, "proj_" + name)
            off += width
        y_att = _attention(segs["aq"], segs["akv"], attn_sinks[i].astype(F32), seq // ATT_BLOCK)
        y_ret = _retention(segs["rqk"], segs["rv"], segs["rg"], ret_norm_g[i].astype(F32),
                           bsz, seq // RET_CHUNK)
        merged = _merge(y_att, y_ret, w_att_out[i].astype(F32), w_ret_out[i].astype(F32),
                        segs["ga"], segs["gr"])
        z1 = _resid_mm(merged, w_out[i].astype(F32), xf)
        x1, x1p, gates, top_idx = _ln_router(z1, ln1_g[i], ln1_b[i], w_router[i], b_router[i])
        item_dst, block_expert, is_half, n_used = _routing_tables(top_idx[:, :TOP_K], t)
        yp = _moe(x1p, item_dst, block_expert, is_half, n_used, w_gate_up[i].astype(BF16),
                  b_gate_up[i].astype(F32), w_down[i].astype(BF16), b_down[i].astype(F32), t, d)
        x2, x2b = _combine_ln(x1, yp, gates, ln2_g[i], ln2_b[i])
        z3 = _ple(x2b, w_ple_gate[i].astype(F32), p[i].reshape(t, -1), w_ple[i].astype(F32), x2)
        xf = _ln(z3, ln3_g[i], ln3_b[i])
    return xf.reshape(bsz, seq, d)
```

```python
import functools
import math

import jax
import jax.numpy as jnp
import numpy as np
from jax import lax
from jax.experimental import pallas as pl
from jax.experimental.pallas import tpu as pltpu

F32 = jnp.float32
BF16 = jnp.bfloat16

ATT_HEADS = 32
ATT_KV_HEADS = 4
ATT_GROUP = ATT_HEADS // ATT_KV_HEADS
ATT_HEAD_DIM = 64
ATT_BLOCK = 128
RET_HEADS = 8
RET_KEY_DIM = 128
RET_VAL_DIM = 256
RET_CHUNK = 128
N_EXPERTS = 32
TOP_K = 4
EXPERT_FF = 1536
SWIGLU_LIMIT = 7.0
SWIGLU_ALPHA = 1.702
LN_EPS = 1e-5
DEPTH = 1
DEEPNORM_ALPHA = float((2 * DEPTH) ** 0.25)

ATT_Q_WIDTH = ATT_HEADS * ATT_HEAD_DIM
ATT_KV_WIDTH = ATT_KV_HEADS * ATT_HEAD_DIM
RET_QK_WIDTH = RET_HEADS * RET_KEY_DIM
RET_V_WIDTH = RET_HEADS * RET_VAL_DIM

LANES = 128
VMEM_LIMIT_BYTES = 56 * 1024 * 1024
MM_TM = 1024
MM_TN = 512
LN_TM = 256
COMBINE_TM = 128
MOE_BM = 512
MOE_TF = 512
MOE_TD = 2048
MOE_DC = 512
ROUTER_PAD = LANES
NEG_BIG = -1e30
SLAB_PITCH = 24
HI_MASK = 0xFFFF0000


def _params(*sem):
    return pltpu.CompilerParams(dimension_semantics=sem, vmem_limit_bytes=VMEM_LIMIT_BYTES)


def _sigmoid(x):
    return 1.0 / (1.0 + jnp.exp(-x))


def _bf16_bits(x):
    return lax.bitcast_convert_type(x.astype(BF16).astype(F32), jnp.uint32)


def _pack_words(lo, hi):
    return (_bf16_bits(lo) >> 16) | _bf16_bits(hi)


def _unpack_lo(w):
    return lax.bitcast_convert_type(w << 16, F32)


def _unpack_hi(w):
    return lax.bitcast_convert_type(w & jnp.uint32(HI_MASK), F32)


def _cast_at_first_row_tile(w_ref, wb_ref):
    @pl.when(pl.program_id(1) == 0)
    def _():
        wb_ref[...] = w_ref[...].astype(BF16)


def _proj_kernel(x_ref, w_ref, o_ref, wb_ref):
    _cast_at_first_row_tile(w_ref, wb_ref)
    o_ref[...] = jnp.dot(x_ref[...], wb_ref[...], preferred_element_type=F32).astype(o_ref.dtype)


def _proj(xb, w, col_off, ncols, name):
    m, k = xb.shape
    tm, tn = min(MM_TM, m), MM_TN
    assert m % tm == 0 and ncols % tn == 0 and col_off % tn == 0
    off = col_off // tn
    return pl.pallas_call(
        _proj_kernel,
        grid=(ncols // tn, m // tm),
        in_specs=[pl.BlockSpec((tm, k), lambda j, i: (i, 0)),
                  pl.BlockSpec((k, tn), lambda j, i: (0, j + off))],
        out_specs=pl.BlockSpec((tm, tn), lambda j, i: (i, j)),
        out_shape=jax.ShapeDtypeStruct((m, ncols), BF16),
        scratch_shapes=[pltpu.VMEM((k, tn), BF16)],
        compiler_params=_params("arbitrary", "arbitrary"),
        name=name,
    )(xb, w)


def _merge_kernel(ya_ref, yr_ref, wa_ref, wr_ref, ga_ref, gr_ref, o_ref, wab_ref, wrb_ref):
    _cast_at_first_row_tile(wa_ref, wab_ref)
    _cast_at_first_row_tile(wr_ref, wrb_ref)
    a = jnp.dot(ya_ref[...], wab_ref[...], preferred_element_type=F32)
    r = jnp.dot(yr_ref[...], wrb_ref[...], preferred_element_type=F32)
    ga = _sigmoid(ga_ref[...].astype(F32))
    gr = _sigmoid(gr_ref[...].astype(F32))
    o_ref[...] = (ga * a + gr * r).astype(o_ref.dtype)


def _merge(ya, yr, wa, wr, ga, gr):
    m, ka = ya.shape
    kr = yr.shape[1]
    n = wa.shape[1]
    tm, tn = min(MM_TM, m), MM_TN
    return pl.pallas_call(
        _merge_kernel,
        grid=(n // tn, m // tm),
        in_specs=[pl.BlockSpec((tm, ka), lambda j, i: (i, 0)),
                  pl.BlockSpec((tm, kr), lambda j, i: (i, 0)),
                  pl.BlockSpec((ka, tn), lambda j, i: (0, j)),
                  pl.BlockSpec((kr, tn), lambda j, i: (0, j)),
                  pl.BlockSpec((tm, tn), lambda j, i: (i, j)),
                  pl.BlockSpec((tm, tn), lambda j, i: (i, j))],
        out_specs=pl.BlockSpec((tm, tn), lambda j, i: (i, j)),
        out_shape=jax.ShapeDtypeStruct((m, n), BF16),
        scratch_shapes=[pltpu.VMEM((ka, tn), BF16), pltpu.VMEM((kr, tn), BF16)],
        compiler_params=_params("arbitrary", "arbitrary"),
        name="merge",
    )(ya, yr, wa, wr, ga, gr)


def _resid_mm_kernel(a_ref, w_ref, x_ref, o_ref, wb_ref):
    _cast_at_first_row_tile(w_ref, wb_ref)
    acc = jnp.dot(a_ref[...], wb_ref[...], preferred_element_type=F32)
    o_ref[...] = DEEPNORM_ALPHA * x_ref[...] + acc


def _resid_mm(a, w, x):
    m, k = a.shape
    n = w.shape[1]
    tm, tn = min(MM_TM, m), MM_TN
    return pl.pallas_call(
        _resid_mm_kernel,
        grid=(n // tn, m // tm),
        in_specs=[pl.BlockSpec((tm, k), lambda j, i: (i, 0)),
                  pl.BlockSpec((k, tn), lambda j, i: (0, j)),
                  pl.BlockSpec((tm, tn), lambda j, i: (i, j))],
        out_specs=pl.BlockSpec((tm, tn), lambda j, i: (i, j)),
        out_shape=jax.ShapeDtypeStruct((m, n), F32),
        scratch_shapes=[pltpu.VMEM((k, tn), BF16)],
        compiler_params=_params("arbitrary", "arbitrary"),
        name="resid_mm",
    )(a, w, x)


def _ple_kernel(xb_ref, wg_ref, p_ref, wp_ref, x_ref, o_ref, wgb_ref, wpb_ref):
    _cast_at_first_row_tile(wg_ref, wgb_ref)
    _cast_at_first_row_tile(wp_ref, wpb_ref)
    gate = jnp.dot(xb_ref[...], wgb_ref[...], preferred_element_type=F32)
    emb = jnp.dot(p_ref[...].astype(BF16), wpb_ref[...], preferred_element_type=F32)
    o_ref[...] = DEEPNORM_ALPHA * x_ref[...] + emb * _sigmoid(gate)


def _ple(xb, wg, p, wp, x):
    m, k = xb.shape
    n = wg.shape[1]
    kp = p.shape[1]
    tm, tn = min(MM_TM, m), MM_TN
    return pl.pallas_call(
        _ple_kernel,
        grid=(n // tn, m // tm),
        in_specs=[pl.BlockSpec((tm, k), lambda j, i: (i, 0)),
                  pl.BlockSpec((k, tn), lambda j, i: (0, j)),
                  pl.BlockSpec((tm, kp), lambda j, i: (i, 0)),
                  pl.BlockSpec((kp, tn), lambda j, i: (0, j)),
                  pl.BlockSpec((tm, tn), lambda j, i: (i, j))],
        out_specs=pl.BlockSpec((tm, tn), lambda j, i: (i, j)),
        out_shape=jax.ShapeDtypeStruct((m, n), F32),
        scratch_shapes=[pltpu.VMEM((k, tn), BF16), pltpu.VMEM((kp, tn), BF16)],
        compiler_params=_params("arbitrary", "arbitrary"),
        name="ple",
    )(xb, wg, p, wp, x)


def _alibi_slope(h):
    return float(2.0 ** (-8.0 * (h + 1) / ATT_HEADS))


def _attn_kernel(sink_ref, q_ref, kvc_ref, kvp_ref, o_ref, *, nblk):
    blk = ATT_BLOCK
    n = lax.rem(pl.program_id(0), nblk)
    has_prev = n > 0
    row = lax.broadcasted_iota(jnp.int32, (blk, 2 * blk), 0)
    col = lax.broadcasted_iota(jnp.int32, (blk, 2 * blk), 1)
    dist = blk + row - col
    valid = (dist >= 0) & (dist < blk) & ((col >= blk) | has_prev)
    distf = dist.astype(F32)
    lane = lax.broadcasted_iota(jnp.int32, (2 * blk, LANES), 1)
    lo = lane < ATT_HEAD_DIM
    scale = ATT_HEAD_DIM ** -0.5

    def halves(t, half):
        swapped = jnp.concatenate([t[:, ATT_HEAD_DIM:], t[:, :ATT_HEAD_DIM]], axis=1)
        zero = jnp.zeros_like(t)
        if half == 0:
            return jnp.where(lo, t, zero), jnp.where(lo, zero, swapped)
        return jnp.where(lo, swapped, zero), jnp.where(lo, zero, t)

    for kh in range(ATT_KV_HEADS):
        tile, half = kh // 2, kh % 2
        ks = slice(tile * LANES, (tile + 1) * LANES)
        vs = slice(ATT_KV_WIDTH + tile * LANES, ATT_KV_WIDTH + (tile + 1) * LANES)
        kcat = jnp.concatenate([kvp_ref[:, ks], kvc_ref[:, ks]], axis=0)
        vcat = jnp.concatenate([kvp_ref[:, vs], kvc_ref[:, vs]], axis=0)
        k_lo, k_hi = halves(kcat, half)
        v_lo, v_hi = halves(vcat, half)
        for jp in range(ATT_GROUP // 2):
            h0 = kh * ATT_GROUP + 2 * jp
            qs = slice(h0 * ATT_HEAD_DIM, h0 * ATT_HEAD_DIM + LANES)
            qp = q_ref[:, qs]
            acc = jnp.zeros((blk, LANES), F32)
            for par, (kx, vx) in enumerate(((k_lo, v_lo), (k_hi, v_hi))):
                h = h0 + par
                s = lax.dot_general(qp, kx, (((1,), (1,)), ((), ())), preferred_element_type=F32)
                s = s * scale - _alibi_slope(h) * distf
                s = jnp.where(valid, s, -jnp.inf)
                sink = sink_ref[h]
                m = jnp.maximum(jnp.max(s, axis=-1, keepdims=True), sink)
                e = jnp.exp(s - m)
                denom = jnp.sum(e, axis=-1, keepdims=True) + jnp.exp(sink - m)
                pv = jnp.dot(e.astype(BF16), vx, preferred_element_type=F32)
                acc = acc + pv * (1.0 / denom)
            o_ref[:, qs] = acc.astype(o_ref.dtype)


def _attention(aq, akv, sinks, nblk):
    t = aq.shape[0]
    blk = ATT_BLOCK
    return pl.pallas_call(
        functools.partial(_attn_kernel, nblk=nblk),
        grid=(t // blk,),
        in_specs=[pl.BlockSpec(memory_space=pltpu.SMEM),
                  pl.BlockSpec((blk, ATT_Q_WIDTH), lambda i: (i, 0)),
                  pl.BlockSpec((blk, 2 * ATT_KV_WIDTH), lambda i: (i, 0)),
                  pl.BlockSpec((blk, 2 * ATT_KV_WIDTH), lambda i: (jnp.maximum(i - 1, 0), 0))],
        out_specs=pl.BlockSpec((blk, ATT_Q_WIDTH), lambda i: (i, 0)),
        out_shape=jax.ShapeDtypeStruct((t, ATT_Q_WIDTH), BF16),
        compiler_params=_params("arbitrary"),
        name="swa_attention",
    )(sinks, aq, akv, akv)


def _ret_constants():
    c = RET_CHUNK
    gam = 1.0 - 2.0 ** (-5.0 - np.arange(RET_HEADS, dtype=np.float64))
    lg = np.log(gam)
    pos = np.arange(c, dtype=np.float64)
    diff = pos[:, None] - pos[None, :]
    kscale = RET_KEY_DIM ** -0.5
    inner = np.where(diff >= 0, np.exp(np.maximum(diff, 0.0) * lg[:, None, None]), 0.0) * kscale
    qdec = np.exp((pos + 1.0) * lg[:, None])
    kdec = np.exp((c - 1.0 - pos) * lg[:, None]) * kscale
    cdec = np.exp(c * lg)
    qdec_b = np.broadcast_to(qdec[:, :, None], (RET_HEADS, c, RET_VAL_DIM))
    kdec_b = np.broadcast_to(kdec[:, :, None], (RET_HEADS, c, RET_KEY_DIM))
    return (jnp.asarray(inner, F32), jnp.asarray(qdec_b, F32), jnp.asarray(kdec_b, F32),
            [float(v) for v in cdec])


def _ret_kernel(qk_ref, v_ref, g_ref, ng_ref, inner_ref, qdec_ref, kdec_ref, o_ref, state_ref, *, cdec):
    @pl.when(pl.program_id(1) == 0)
    def _():
        state_ref[...] = jnp.zeros_like(state_ref)

    dk, dv = RET_KEY_DIM, RET_VAL_DIM
    for h in range(RET_HEADS):
        q = qk_ref[:, h * dk:(h + 1) * dk]
        k = qk_ref[:, RET_QK_WIDTH + h * dk:RET_QK_WIDTH + (h + 1) * dk]
        v = v_ref[:, h * dv:(h + 1) * dv]
        st = state_ref[h]
        qk = lax.dot_general(q, k, (((1,), (1,)), ((), ())), preferred_element_type=F32)
        a = (qk * inner_ref[h]).astype(BF16)
        inner = jnp.dot(a, v, preferred_element_type=F32)
        cross = jnp.dot(q, st.astype(BF16), preferred_element_type=F32) * qdec_ref[h]
        kd = (k.astype(F32) * kdec_ref[h]).astype(BF16)
        upd = lax.dot_general(kd, v, (((0,), (0,)), ((), ())), preferred_element_type=F32)
        state_ref[h] = st * cdec[h] + upd
        o = inner + cross
        mu = jnp.mean(o, axis=-1, keepdims=True)
        d = o - mu
        var = jnp.mean(d * d, axis=-1, keepdims=True)
        y = d * lax.rsqrt(var + LN_EPS) * ng_ref[:, h * dv:(h + 1) * dv]
        g = g_ref[:, h * dv:(h + 1) * dv].astype(F32)
        o_ref[:, h * dv:(h + 1) * dv] = (g * _sigmoid(g) * y).astype(o_ref.dtype)


def _retention(rqk, rv, rg, norm_g, batch, nchunk):
    t = rqk.shape[0]
    c = RET_CHUNK
    inner, qdec, kdec, cdec = _ret_constants()
    row = lambda b, n: (b * nchunk + n, 0)
    const3 = lambda b, n: (0, 0, 0)
    return pl.pallas_call(
        functools.partial(_ret_kernel, cdec=cdec),
        grid=(batch, nchunk),
        in_specs=[pl.BlockSpec((c, 2 * RET_QK_WIDTH), row),
                  pl.BlockSpec((c, RET_V_WIDTH), row),
                  pl.BlockSpec((c, RET_V_WIDTH), row),
                  pl.BlockSpec((1, RET_V_WIDTH), lambda b, n: (0, 0)),
                  pl.BlockSpec((RET_HEADS, c, c), const3),
                  pl.BlockSpec((RET_HEADS, c, RET_VAL_DIM), const3),
                  pl.BlockSpec((RET_HEADS, c, RET_KEY_DIM), const3)],
        out_specs=pl.BlockSpec((c, RET_V_WIDTH), row),
        out_shape=jax.ShapeDtypeStruct((t, RET_V_WIDTH), BF16),
        scratch_shapes=[pltpu.VMEM((RET_HEADS, RET_KEY_DIM, RET_VAL_DIM), F32)],
        compiler_params=_params("arbitrary", "arbitrary"),
        name="retention",
    )(rqk, rv, rg, norm_g.reshape(1, RET_V_WIDTH), inner, qdec, kdec)


def _ln_rows(z, g, b):
    mu = jnp.mean(z, axis=-1, keepdims=True)
    d = z - mu
    var = jnp.mean(d * d, axis=-1, keepdims=True)
    return d * lax.rsqrt(var + LN_EPS) * g + b


def _ln_router_kernel(z_ref, g_ref, b_ref, wr_ref, br_ref, x_ref, xp_ref, gate_ref, idx_ref):
    x = _ln_rows(z_ref[...], g_ref[...], b_ref[...])
    x_ref[...] = x
    tm, d = x.shape
    half = d // 2
    nsl = half // LANES
    for s in range(nsl):
        cs = slice(s * LANES, (s + 1) * LANES)
        hs = slice(half + s * LANES, half + (s + 1) * LANES)
        xp_ref[pl.ds(s, tm, stride=nsl), :] = _pack_words(x[:, cs], x[:, hs])
    x_hi = x.astype(BF16)
    x_lo = (x - x_hi.astype(F32)).astype(BF16)
    p_hi = jnp.dot(x_hi, wr_ref[...], preferred_element_type=F32)
    p_lo = jnp.dot(x_lo, wr_ref[...], preferred_element_type=F32)
    logits = p_hi + pltpu.roll(p_hi, ROUTER_PAD - N_EXPERTS, axis=1) + p_lo + br_ref[...]
    lane = lax.broadcasted_iota(jnp.int32, logits.shape, 1)
    lane_f = lane.astype(F32)
    vals, idxs = [], []
    cur = logits
    for _ in range(TOP_K):
        m = jnp.max(cur, axis=-1, keepdims=True)
        idx_f = jnp.min(jnp.where(cur == m, lane_f, float(ROUTER_PAD)), axis=-1, keepdims=True)
        idx = idx_f.astype(jnp.int32)
        vals.append(m)
        idxs.append(idx)
        cur = jnp.where(lane == idx, -jnp.inf, cur)
    es = [jnp.exp(v - vals[0]) for v in vals]
    inv = 1.0 / (es[0] + es[1] + es[2] + es[3])
    gates = jnp.zeros(logits.shape, F32)
    ids = jnp.zeros(logits.shape, jnp.int32)
    for k in range(TOP_K):
        gates = jnp.where(lane == k, es[k] * inv, gates)
        ids = jnp.where(lane == k, idxs[k], ids)
    gate_ref[...] = gates
    idx_ref[...] = ids


def _ln_router(z, g, b, w_router, b_router):
    t, d = z.shape
    tm = min(LN_TM, t)
    w32 = w_router.astype(F32)
    w_hi = w32.astype(BF16)
    w_lo = (w32 - w_hi.astype(F32)).astype(BF16)
    wr = jnp.zeros((d, ROUTER_PAD), BF16).at[:, :N_EXPERTS].set(w_hi).at[:, N_EXPERTS:2 * N_EXPERTS].set(w_lo)
    br = jnp.full((1, ROUTER_PAD), NEG_BIG, F32).at[0, :N_EXPERTS].set(b_router.astype(F32))
    row = lambda i: (i, 0)
    const = lambda i: (0, 0)
    nsl = d // 2 // LANES
    return pl.pallas_call(
        _ln_router_kernel,
        grid=(t // tm,),
        in_specs=[pl.BlockSpec((tm, d), row),
                  pl.BlockSpec((1, d), const),
                  pl.BlockSpec((1, d), const),
                  pl.BlockSpec((d, ROUTER_PAD), const),
                  pl.BlockSpec((1, ROUTER_PAD), const)],
        out_specs=[pl.BlockSpec((tm, d), row),
                   pl.BlockSpec((tm * nsl, LANES), row),
                   pl.BlockSpec((tm, ROUTER_PAD), row),
                   pl.BlockSpec((tm, ROUTER_PAD), row)],
        out_shape=[jax.ShapeDtypeStruct((t, d), F32),
                   jax.ShapeDtypeStruct((t * nsl, LANES), jnp.uint32),
                   jax.ShapeDtypeStruct((t, ROUTER_PAD), F32),
                   jax.ShapeDtypeStruct((t, ROUTER_PAD), jnp.int32)],
        compiler_params=_params("arbitrary"),
        name="ln_router",
    )(z, g.reshape(1, d), b.reshape(1, d), wr, br)


def _combine_ln_kernel(x_ref, y0_ref, y1_ref, y2_ref, y3_ref, gate_ref, g_ref, b_ref, o_ref, ob_ref):
    tm, d = x_ref.shape
    nsl = d // 2 // LANES
    gt = gate_ref[...]
    gk = [jnp.broadcast_to(gt[:, k:k + 1], (tm, LANES)) for k in range(TOP_K)]
    los, his = [], []
    for s in range(nsl):
        lo = hi = None
        for k, y_ref in enumerate((y0_ref, y1_ref, y2_ref, y3_ref)):
            w = y_ref[pl.ds(s, tm, stride=nsl), :]
            tl, th = gk[k] * _unpack_lo(w), gk[k] * _unpack_hi(w)
            lo = tl if lo is None else lo + tl
            hi = th if hi is None else hi + th
        los.append(lo)
        his.append(hi)
    z = DEEPNORM_ALPHA * x_ref[...] + jnp.concatenate(los + his, axis=1)
    out = _ln_rows(z, g_ref[...], b_ref[...])
    o_ref[...] = out
    ob_ref[...] = out.astype(ob_ref.dtype)


def _combine_ln(x1, yp, gates, g, b):
    t, d = x1.shape
    tm = min(COMBINE_TM, t)
    nrb = t // tm
    nsl = d // 2 // LANES
    row = lambda i: (i, 0)
    const = lambda i: (0, 0)
    y_specs = [pl.BlockSpec((tm * nsl, LANES), functools.partial(lambda i, k: (k * nrb + i, 0), k=k))
               for k in range(TOP_K)]
    return pl.pallas_call(
        _combine_ln_kernel,
        grid=(nrb,),
        in_specs=[pl.BlockSpec((tm, d), row)] + y_specs +
                 [pl.BlockSpec((tm, ROUTER_PAD), row),
                  pl.BlockSpec((1, d), const),
                  pl.BlockSpec((1, d), const)],
        out_specs=[pl.BlockSpec((tm, d), row), pl.BlockSpec((tm, d), row)],
        out_shape=[jax.ShapeDtypeStruct((t, d), F32), jax.ShapeDtypeStruct((t, d), BF16)],
        compiler_params=_params("arbitrary"),
        name="combine_ln",
    )(x1, yp, yp, yp, yp, gates, g.reshape(1, d), b.reshape(1, d))


def _ln_kernel(z_ref, g_ref, b_ref, o_ref):
    o_ref[...] = _ln_rows(z_ref[...], g_ref[...], b_ref[...])


def _ln(z, g, b):
    t, d = z.shape
    tm = min(LN_TM, t)
    return pl.pallas_call(
        _ln_kernel,
        grid=(t // tm,),
        in_specs=[pl.BlockSpec((tm, d), lambda i: (i, 0)),
                  pl.BlockSpec((1, d), lambda i: (0, 0)),
                  pl.BlockSpec((1, d), lambda i: (0, 0))],
        out_specs=pl.BlockSpec((tm, d), lambda i: (i, 0)),
        out_shape=jax.ShapeDtypeStruct((t, d), F32),
        compiler_params=_params("arbitrary"),
        name="layer_norm",
    )(z, g.reshape(1, d), b.reshape(1, d))


def _moe_kernel(be_ref, nu_ref, hf_ref, idx_hbm, x_hbm, wg_ref, wu_ref, bg_ref, bu_ref, wd_ref, bd_ref,
                y_hbm, idx_smem, xg, xb, act, ybuf, sem_idx, sem_g, sem_s, *, bm, ngu, ntok):
    del be_ref
    i = pl.program_id(0)
    j = pl.program_id(1)
    nu = nu_ref[0]
    active = i < nu
    d = xb.shape[1]

    def with_rows(item, fn):
        is_half = hf_ref[item] == 1

        @pl.when(is_half)
        def _():
            fn(bm // 2)

        @pl.when(jnp.logical_not(is_half))
        def _():
            fn(bm)

    half = d // 2
    nsl = half // LANES
    td = wd_ref.shape[2]

    def idx_copy(item, slot):
        src = idx_hbm.at[pl.ds(pl.multiple_of(item * (2 * bm), 2 * bm), 2 * bm)]
        dst = idx_smem.at[pl.ds(pl.multiple_of(slot * (2 * bm), 2 * bm), 2 * bm)]
        return pltpu.make_async_copy(src, dst, sem_idx)

    def for_each_row(slot, table, rows, fn):
        base = slot * (2 * bm) + table * bm

        def body(r, c):
            fn(idx_smem[base + r], r)
            return c
        lax.fori_loop(0, rows, body, 0, unroll=8)

    def vmem_slab(buf, r):
        return buf.at[pl.ds(pl.multiple_of(r * SLAB_PITCH, 8), nsl), :]

    def hbm_slab(ref, row):
        return ref.at[pl.ds(pl.multiple_of(row, 8), nsl), :]

    def start_gather(slot, rows):
        def one(src, r):
            pltpu.make_async_copy(hbm_slab(x_hbm, src), vmem_slab(xg, r), sem_g).start()
        for_each_row(slot, 0, rows, one)

    def wait_gather(rows):
        pltpu.make_async_copy(x_hbm.at[pl.ds(0, rows * nsl), :], xg.at[pl.ds(0, rows * nsl), :], sem_g).wait()

    def start_scatter(slot, rows):
        def one(dst, r):
            pltpu.make_async_copy(vmem_slab(ybuf, r), hbm_slab(y_hbm, dst), sem_s).start()
        for_each_row(slot, 1, rows, one)

    def wait_scatter(rows):
        pltpu.make_async_copy(ybuf.at[pl.ds(0, rows * nsl), :], y_hbm.at[pl.ds(0, rows * nsl), :], sem_s).wait()

    def unpack(rows):
        wait_gather(rows)
        for s in range(nsl):
            w = xg[pl.ds(s, rows, stride=SLAB_PITCH), :]
            xb[:rows, s * LANES:(s + 1) * LANES] = _unpack_lo(w).astype(BF16)
            xb[:rows, half + s * LANES:half + (s + 1) * LANES] = _unpack_hi(w).astype(BF16)

    @pl.when(active & (j == 0))
    def _():
        @pl.when(i == 0)
        def _():
            idx_copy(0, 0).start()
            idx_copy(0, 0).wait()
            with_rows(0, lambda rows: start_gather(0, rows))
            ybuf[...] = jnp.zeros_like(ybuf)
            pad = pltpu.make_async_copy(ybuf.at[pl.ds(0, bm * nsl), :],
                                        y_hbm.at[pl.ds(TOP_K * ntok * nsl, bm * nsl), :], sem_s)
            pad.start()
            pad.wait()
        nxt = lax.rem(i + 1, 2)

        @pl.when(i + 1 < nu)
        def _():
            idx_copy(i + 1, nxt).start()

        with_rows(i, unpack)

        @pl.when(i + 1 < nu)
        def _():
            idx_copy(i + 1, nxt).wait()
            with_rows(i + 1, lambda rows: start_gather(nxt, rows))

    def gate_up(rows):
        x = xb[:rows, :]
        g = jnp.dot(x, wg_ref[0], preferred_element_type=F32) + bg_ref[0]
        u = jnp.dot(x, wu_ref[0], preferred_element_type=F32) + bu_ref[0]
        g = jnp.minimum(g, SWIGLU_LIMIT)
        u = jnp.clip(u, -SWIGLU_LIMIT, SWIGLU_LIMIT)
        act[j, :rows, :] = (g * _sigmoid(SWIGLU_ALPHA * g) * (u + 1.0)).astype(BF16)

    @pl.when(active & (j < ngu))
    def _():
        with_rows(i, gate_up)

    def down(is_hi, rows):
        a = jnp.concatenate([act[q, :rows, :] for q in range(ngu)], axis=1)
        for c in range(td // MOE_DC):
            cs = slice(c * MOE_DC, (c + 1) * MOE_DC)
            y = jnp.dot(a, wd_ref[0, :, cs], preferred_element_type=F32) + bd_ref[0, :, cs]
            bits = _bf16_bits(y)
            for q in range(MOE_DC // LANES):
                s = c * (MOE_DC // LANES) + q
                piece = bits[:, q * LANES:(q + 1) * LANES]
                srows = pl.ds(s, rows, stride=SLAB_PITCH)
                if is_hi:
                    ybuf[srows, :] = ybuf[srows, :] | piece
                else:
                    ybuf[srows, :] = piece >> 16

    @pl.when(active & (j == ngu))
    def _():
        @pl.when(i > 0)
        def _():
            with_rows(i - 1, wait_scatter)
        with_rows(i, lambda rows: down(False, rows))

    @pl.when(active & (j == ngu + 1))
    def _():
        def finish(rows):
            down(True, rows)
            start_scatter(lax.rem(i, 2), rows)

            @pl.when(i == nu - 1)
            def _():
                wait_scatter(rows)
        with_rows(i, finish)


def _moe(xp, item_dst, block_expert, is_half, n_used, wgu, bgu, wd, bd, t, d):
    bm, tf, td = MOE_BM, MOE_TF, MOE_TD
    half = d // 2
    nsl = half // LANES
    assert td == half and td % MOE_DC == 0 and EXPERT_FF % tf == 0
    assert nsl % 8 == 0 and SLAB_PITCH % 8 == 0 and SLAB_PITCH >= nsl
    n_items = item_dst.shape[0]
    ngu = EXPERT_FF // tf
    nj = ngu + d // td
    idx3 = jnp.stack([(item_dst % t) * nsl, item_dst * nsl], axis=1).reshape(n_items * 2 * bm)
    bgu3 = bgu.reshape(N_EXPERTS, 1, 2 * EXPERT_FF)
    bd3 = bd.reshape(N_EXPERTS, 1, d)

    def step(i, j, nu):
        return jnp.where(i < nu[0], j, nj - 1)

    def gu(i, j, nu):
        return jnp.minimum(step(i, j, nu), ngu - 1)

    def dn(i, j, nu):
        return jnp.maximum(step(i, j, nu) - ngu, 0)

    def wd_block(i, j, be, nu):
        in_down = step(i, j, nu) >= ngu
        e = jnp.where(in_down, be[i], be[jnp.maximum(i - 1, 0)])
        return (e, 0, jnp.where(in_down, dn(i, j, nu), d // td - 1))

    grid_spec = pltpu.PrefetchScalarGridSpec(
        num_scalar_prefetch=3,
        grid=(n_items, nj),
        in_specs=[
            pl.BlockSpec(memory_space=pl.ANY),
            pl.BlockSpec(memory_space=pl.ANY),
            pl.BlockSpec((1, d, tf), lambda i, j, be, nu, hf: (be[i], 0, gu(i, j, nu))),
            pl.BlockSpec((1, d, tf), lambda i, j, be, nu, hf: (be[i], 0, gu(i, j, nu) + ngu)),
            pl.BlockSpec((1, 1, tf), lambda i, j, be, nu, hf: (be[i], 0, gu(i, j, nu))),
            pl.BlockSpec((1, 1, tf), lambda i, j, be, nu, hf: (be[i], 0, gu(i, j, nu) + ngu)),
            pl.BlockSpec((1, EXPERT_FF, td), lambda i, j, be, nu, hf: wd_block(i, j, be, nu)),
            pl.BlockSpec((1, 1, td), lambda i, j, be, nu, hf: (be[i], 0, dn(i, j, nu))),
        ],
        out_specs=pl.BlockSpec(memory_space=pl.ANY),
        scratch_shapes=[
            pltpu.SMEM((2 * 2 * bm,), jnp.int32),
            pltpu.VMEM((bm * SLAB_PITCH, LANES), jnp.uint32),
            pltpu.VMEM((bm, d), BF16),
            pltpu.VMEM((ngu, bm, tf), BF16),
            pltpu.VMEM((bm * SLAB_PITCH, LANES), jnp.uint32),
            pltpu.SemaphoreType.DMA(()),
            pltpu.SemaphoreType.DMA(()),
            pltpu.SemaphoreType.DMA(()),
        ],
    )
    return pl.pallas_call(
        functools.partial(_moe_kernel, bm=bm, ngu=ngu, ntok=t),
        grid_spec=grid_spec,
        out_shape=jax.ShapeDtypeStruct(((TOP_K * t + bm) * nsl, LANES), jnp.uint32),
        compiler_params=_params("arbitrary", "arbitrary"),
        name="moe_experts",
    )(block_expert, n_used, is_half, idx3, xp, wgu, wgu, bgu3, bgu3, wd, bd3)


def _routing_tables(top_idx, t):
    bm = MOE_BM
    hb = bm // 2
    tk = t * TOP_K
    n_items = tk // bm + N_EXPERTS
    n_rows = (tk // hb + N_EXPERTS) * hb + bm
    flat_e = top_idx.T.reshape(tk)
    experts = jnp.arange(N_EXPERTS, dtype=jnp.int32)
    order = jnp.argsort(flat_e, stable=True).astype(jnp.int32)
    counts = jnp.sum((flat_e[:, None] == experts[None, :]).astype(jnp.int32), axis=0)
    halves = (counts + hb - 1) // hb
    padded = halves * hb
    group_end = jnp.cumsum(counts)
    group_start = group_end - counts
    padded_end = jnp.cumsum(padded)
    padded_start = padded_end - padded
    items_per = (halves + 1) // 2
    item_end = jnp.cumsum(items_per)
    item_first = item_end - items_per

    def bucket(ends, v):
        e = jnp.sum((ends[None, :] <= v[:, None]).astype(jnp.int32), axis=1)
        return jnp.minimum(e, N_EXPERTS - 1)

    def lookup(table, e):
        return jnp.sum(jnp.where(e[:, None] == experts[None, :], table[None, :], 0), axis=1)

    rows = jnp.arange(n_rows, dtype=jnp.int32)
    row_e = bucket(padded_end, rows)
    pos = rows - lookup(padded_start - group_start, row_e)
    valid = pos < lookup(group_end, row_e)
    row_dst = jnp.where(valid, order[jnp.clip(pos, 0, tk - 1)], tk + rows % bm)

    n_used = item_end[-1].astype(jnp.int32)
    items = jnp.minimum(jnp.arange(n_items, dtype=jnp.int32), n_used - 1)
    item_e = bucket(item_end, items)
    local = items - lookup(item_first, item_e)
    is_half = (lookup(halves, item_e) - 2 * local == 1).astype(jnp.int32)
    first_row = lookup(padded_start, item_e) + local * bm
    item_dst = row_dst[first_row[:, None] + jnp.arange(bm, dtype=jnp.int32)[None, :]]
    return item_dst, item_e, is_half, n_used.reshape(1)


def kernel(x, p, w_in, attn_sinks, ret_norm_g, w_att_out, w_ret_out, w_out, ln1_g, ln1_b,
           w_router, b_router, w_gate_up, b_gate_up, w_down, b_down, ln2_g, ln2_b,
           w_ple, w_ple_gate, ln3_g, ln3_b):
    bsz, seq, d = x.shape
    t = bsz * seq
    assert seq % ATT_BLOCK == 0 and seq % RET_CHUNK == 0
    xf = x.reshape(t, d)
    for i in range(DEPTH):
        xb = xf.astype(BF16)
        w_in_i = w_in[i].astype(F32)
        off = 0
        segs = {}
        for name, width in (("aq", ATT_Q_WIDTH), ("akv", 2 * ATT_KV_WIDTH), ("rqk", 2 * RET_QK_WIDTH),
                            ("rv", RET_V_WIDTH), ("rg", RET_V_WIDTH), ("ga", d), ("gr", d)):
            segs[name] = _proj(xb, w_in_i, off, width, "proj_" + name)
            off += width
        y_att = _attention(segs["aq"], segs["akv"], attn_sinks[i].astype(F32), seq // ATT_BLOCK)
        y_ret = _retention(segs["rqk"], segs["rv"], segs["rg"], ret_norm_g[i].astype(F32),
                           bsz, seq // RET_CHUNK)
        merged = _merge(y_att, y_ret, w_att_out[i].astype(F32), w_ret_out[i].astype(F32),
                        segs["ga"], segs["gr"])
        z1 = _resid_mm(merged, w_out[i].astype(F32), xf)
        x1, x1p, gates, top_idx = _ln_router(z1, ln1_g[i], ln1_b[i], w_router[i], b_router[i])
        item_dst, block_expert, is_half, n_used = _routing_tables(top_idx[:, :TOP_K], t)
        yp = _moe(x1p, item_dst, block_expert, is_half, n_used, w_gate_up[i].astype(BF16),
                  b_gate_up[i].astype(F32), w_down[i].astype(BF16), b_down[i].astype(F32), t, d)
        x2, x2b = _combine_ln(x1, yp, gates, ln2_g[i], ln2_b[i])
        z3 = _ple(x2b, w_ple_gate[i].astype(F32), p[i].reshape(t, -1), w_ple[i].astype(F32), x2)
        xf = _ln(z3, ln3_g[i], ln3_b[i])
    return xf.reshape(bsz, seq, d)
```

```python
import functools
import math

import jax
import jax.numpy as jnp
import numpy as np
from jax import lax
from jax.experimental import pallas as pl
from jax.experimental.pallas import tpu as pltpu

F32 = jnp.float32
BF16 = jnp.bfloat16

ATT_HEADS = 32
ATT_KV_HEADS = 4
ATT_GROUP = ATT_HEADS // ATT_KV_HEADS
ATT_HEAD_DIM = 64
ATT_BLOCK = 128
RET_HEADS = 8
RET_KEY_DIM = 128
RET_VAL_DIM = 256
RET_CHUNK = 128
N_EXPERTS = 32
TOP_K = 4
EXPERT_FF = 1536
SWIGLU_LIMIT = 7.0
SWIGLU_ALPHA = 1.702
LN_EPS = 1e-5
DEPTH = 1
DEEPNORM_ALPHA = float((2 * DEPTH) ** 0.25)

ATT_Q_WIDTH = ATT_HEADS * ATT_HEAD_DIM
ATT_KV_WIDTH = ATT_KV_HEADS * ATT_HEAD_DIM
RET_QK_WIDTH = RET_HEADS * RET_KEY_DIM
RET_V_WIDTH = RET_HEADS * RET_VAL_DIM

LANES = 128
VMEM_LIMIT_BYTES = 56 * 1024 * 1024
MM_TM = 1024
MM_TN = 512
LN_TM = 256
COMBINE_TM = 128
MOE_BM = 512
MOE_TF = 512
MOE_TD = 1024
MOE_DC = 512
ROUTER_PAD = LANES
NEG_BIG = -1e30
SLAB_PITCH = 24
HI_MASK = 0xFFFF0000


def _params(*sem):
    return pltpu.CompilerParams(dimension_semantics=sem, vmem_limit_bytes=VMEM_LIMIT_BYTES)


def _sigmoid(x):
    return 1.0 / (1.0 + jnp.exp(-x))


def _bf16_bits(x):
    return lax.bitcast_convert_type(x.astype(BF16).astype(F32), jnp.uint32)


def _pack_words(lo, hi):
    return (_bf16_bits(lo) >> 16) | _bf16_bits(hi)


def _unpack_lo(w):
    return lax.bitcast_convert_type(w << 16, F32)


def _unpack_hi(w):
    return lax.bitcast_convert_type(w & jnp.uint32(HI_MASK), F32)


def _cast_at_first_row_tile(w_ref, wb_ref):
    @pl.when(pl.program_id(1) == 0)
    def _():
        wb_ref[...] = w_ref[...].astype(BF16)


def _proj_kernel(x_ref, w_ref, o_ref, wb_ref):
    _cast_at_first_row_tile(w_ref, wb_ref)
    o_ref[...] = jnp.dot(x_ref[...], wb_ref[...], preferred_element_type=F32).astype(o_ref.dtype)


def _proj(xb, w, col_off, ncols, name):
    m, k = xb.shape
    tm, tn = min(MM_TM, m), MM_TN
    assert m % tm == 0 and ncols % tn == 0 and col_off % tn == 0
    off = col_off // tn
    return pl.pallas_call(
        _proj_kernel,
        grid=(ncols // tn, m // tm),
        in_specs=[pl.BlockSpec((tm, k), lambda j, i: (i, 0)),
                  pl.BlockSpec((k, tn), lambda j, i: (0, j + off))],
        out_specs=pl.BlockSpec((tm, tn), lambda j, i: (i, j)),
        out_shape=jax.ShapeDtypeStruct((m, ncols), BF16),
        scratch_shapes=[pltpu.VMEM((k, tn), BF16)],
        compiler_params=_params("arbitrary", "arbitrary"),
        name=name,
    )(xb, w)


def _proj_cast_kernel(x_ref, w_ref, o_ref, xb_ref, wb_ref):
    _cast_at_first_row_tile(w_ref, wb_ref)
    xb = x_ref[...].astype(BF16)
    xb_ref[...] = xb
    o_ref[...] = jnp.dot(xb, wb_ref[...], preferred_element_type=F32).astype(o_ref.dtype)


def _proj_cast(x, w, col_off, name):
    m, k = x.shape
    tm, tn = min(MM_TM // 2, m), MM_TN
    assert m % tm == 0 and col_off % tn == 0
    off = col_off // tn
    return pl.pallas_call(
        _proj_cast_kernel,
        grid=(1, m // tm),
        in_specs=[pl.BlockSpec((tm, k), lambda j, i: (i, 0)),
                  pl.BlockSpec((k, tn), lambda j, i: (0, off))],
        out_specs=[pl.BlockSpec((tm, tn), lambda j, i: (i, 0)),
                   pl.BlockSpec((tm, k), lambda j, i: (i, 0))],
        out_shape=[jax.ShapeDtypeStruct((m, tn), BF16), jax.ShapeDtypeStruct((m, k), BF16)],
        scratch_shapes=[pltpu.VMEM((k, tn), BF16)],
        compiler_params=_params("arbitrary", "arbitrary"),
        name=name,
    )(x, w)


def _merge_kernel(ya_ref, yr_ref, wa_ref, wr_ref, ga_ref, gr_ref, o_ref, wab_ref, wrb_ref):
    _cast_at_first_row_tile(wa_ref, wab_ref)
    _cast_at_first_row_tile(wr_ref, wrb_ref)
    a = jnp.dot(ya_ref[...], wab_ref[...], preferred_element_type=F32)
    r = jnp.dot(yr_ref[...], wrb_ref[...], preferred_element_type=F32)
    ga = _sigmoid(ga_ref[...].astype(F32))
    gr = _sigmoid(gr_ref[...].astype(F32))
    o_ref[...] = (ga * a + gr * r).astype(o_ref.dtype)


def _merge(ya, yr, wa, wr, ga, gr):
    m, ka = ya.shape
    kr = yr.shape[1]
    n = wa.shape[1]
    tm, tn = min(MM_TM, m), MM_TN
    return pl.pallas_call(
        _merge_kernel,
        grid=(n // tn, m // tm),
        in_specs=[pl.BlockSpec((tm, ka), lambda j, i: (i, 0)),
                  pl.BlockSpec((tm, kr), lambda j, i: (i, 0)),
                  pl.BlockSpec((ka, tn), lambda j, i: (0, j)),
                  pl.BlockSpec((kr, tn), lambda j, i: (0, j)),
                  pl.BlockSpec((tm, tn), lambda j, i: (i, j)),
                  pl.BlockSpec((tm, tn), lambda j, i: (i, j))],
        out_specs=pl.BlockSpec((tm, tn), lambda j, i: (i, j)),
        out_shape=jax.ShapeDtypeStruct((m, n), BF16),
        scratch_shapes=[pltpu.VMEM((ka, tn), BF16), pltpu.VMEM((kr, tn), BF16)],
        compiler_params=_params("arbitrary", "arbitrary"),
        name="merge",
    )(ya, yr, wa, wr, ga, gr)


def _resid_mm_kernel(a_ref, w_ref, x_ref, o_ref, wb_ref):
    _cast_at_first_row_tile(w_ref, wb_ref)
    acc = jnp.dot(a_ref[...], wb_ref[...], preferred_element_type=F32)
    o_ref[...] = DEEPNORM_ALPHA * x_ref[...] + acc


def _resid_mm(a, w, x):
    m, k = a.shape
    n = w.shape[1]
    tm, tn = min(MM_TM, m), MM_TN
    return pl.pallas_call(
        _resid_mm_kernel,
        grid=(n // tn, m // tm),
        in_specs=[pl.BlockSpec((tm, k), lambda j, i: (i, 0)),
                  pl.BlockSpec((k, tn), lambda j, i: (0, j)),
                  pl.BlockSpec((tm, tn), lambda j, i: (i, j))],
        out_specs=pl.BlockSpec((tm, tn), lambda j, i: (i, j)),
        out_shape=jax.ShapeDtypeStruct((m, n), F32),
        scratch_shapes=[pltpu.VMEM((k, tn), BF16)],
        compiler_params=_params("arbitrary", "arbitrary"),
        name="resid_mm",
    )(a, w, x)


def _ple_kernel(xb_ref, wg_ref, p_ref, wp_ref, x_ref, o_ref, wgb_ref, wpb_ref):
    _cast_at_first_row_tile(wg_ref, wgb_ref)
    _cast_at_first_row_tile(wp_ref, wpb_ref)
    gate = jnp.dot(xb_ref[...], wgb_ref[...], preferred_element_type=F32)
    emb = jnp.dot(p_ref[...].astype(BF16), wpb_ref[...], preferred_element_type=F32)
    o_ref[...] = DEEPNORM_ALPHA * x_ref[...] + emb * _sigmoid(gate)


def _ple(xb, wg, p, wp, x):
    m, k = xb.shape
    n = wg.shape[1]
    kp = p.shape[1]
    tm, tn = min(MM_TM, m), MM_TN
    return pl.pallas_call(
        _ple_kernel,
        grid=(n // tn, m // tm),
        in_specs=[pl.BlockSpec((tm, k), lambda j, i: (i, 0)),
                  pl.BlockSpec((k, tn), lambda j, i: (0, j)),
                  pl.BlockSpec((tm, kp), lambda j, i: (i, 0)),
                  pl.BlockSpec((kp, tn), lambda j, i: (0, j)),
                  pl.BlockSpec((tm, tn), lambda j, i: (i, j))],
        out_specs=pl.BlockSpec((tm, tn), lambda j, i: (i, j)),
        out_shape=jax.ShapeDtypeStruct((m, n), F32),
        scratch_shapes=[pltpu.VMEM((k, tn), BF16), pltpu.VMEM((kp, tn), BF16)],
        compiler_params=_params("arbitrary", "arbitrary"),
        name="ple",
    )(xb, wg, p, wp, x)


def _alibi_slope(h):
    return float(2.0 ** (-8.0 * (h + 1) / ATT_HEADS))


def _attn_kernel(sink_ref, q_ref, kvc_ref, kvp_ref, o_ref, *, nblk):
    blk = ATT_BLOCK
    n = lax.rem(pl.program_id(0), nblk)
    has_prev = n > 0
    row = lax.broadcasted_iota(jnp.int32, (blk, 2 * blk), 0)
    col = lax.broadcasted_iota(jnp.int32, (blk, 2 * blk), 1)
    dist = blk + row - col
    valid = (dist >= 0) & (dist < blk) & ((col >= blk) | has_prev)
    distf = dist.astype(F32)
    lane = lax.broadcasted_iota(jnp.int32, (2 * blk, LANES), 1)
    lo = lane < ATT_HEAD_DIM
    scale = ATT_HEAD_DIM ** -0.5

    def halves(t, half):
        swapped = jnp.concatenate([t[:, ATT_HEAD_DIM:], t[:, :ATT_HEAD_DIM]], axis=1)
        zero = jnp.zeros_like(t)
        if half == 0:
            return jnp.where(lo, t, zero), jnp.where(lo, zero, swapped)
        return jnp.where(lo, swapped, zero), jnp.where(lo, zero, t)

    for kh in range(ATT_KV_HEADS):
        tile, half = kh // 2, kh % 2
        ks = slice(tile * LANES, (tile + 1) * LANES)
        vs = slice(ATT_KV_WIDTH + tile * LANES, ATT_KV_WIDTH + (tile + 1) * LANES)
        kcat = jnp.concatenate([kvp_ref[:, ks], kvc_ref[:, ks]], axis=0)
        vcat = jnp.concatenate([kvp_ref[:, vs], kvc_ref[:, vs]], axis=0)
        k_lo, k_hi = halves(kcat, half)
        v_lo, v_hi = halves(vcat, half)
        for jp in range(ATT_GROUP // 2):
            h0 = kh * ATT_GROUP + 2 * jp
            qs = slice(h0 * ATT_HEAD_DIM, h0 * ATT_HEAD_DIM + LANES)
            qp = q_ref[:, qs]
            acc = jnp.zeros((blk, LANES), F32)
            for par, (kx, vx) in enumerate(((k_lo, v_lo), (k_hi, v_hi))):
                h = h0 + par
                s = lax.dot_general(qp, kx, (((1,), (1,)), ((), ())), preferred_element_type=F32)
                s = s * scale - _alibi_slope(h) * distf
                s = jnp.where(valid, s, -jnp.inf)
                sink = sink_ref[h]
                m = jnp.maximum(jnp.max(s, axis=-1, keepdims=True), sink)
                e = jnp.exp(s - m)
                denom = jnp.sum(e, axis=-1, keepdims=True) + jnp.exp(sink - m)
                pv = jnp.dot(e.astype(BF16), vx, preferred_element_type=F32)
                acc = acc + pv * (1.0 / denom)
            o_ref[:, qs] = acc.astype(o_ref.dtype)


def _attention(aq, akv, sinks, nblk):
    t = aq.shape[0]
    blk = ATT_BLOCK
    return pl.pallas_call(
        functools.partial(_attn_kernel, nblk=nblk),
        grid=(t // blk,),
        in_specs=[pl.BlockSpec(memory_space=pltpu.SMEM),
                  pl.BlockSpec((blk, ATT_Q_WIDTH), lambda i: (i, 0)),
                  pl.BlockSpec((blk, 2 * ATT_KV_WIDTH), lambda i: (i, 0)),
                  pl.BlockSpec((blk, 2 * ATT_KV_WIDTH), lambda i: (jnp.maximum(i - 1, 0), 0))],
        out_specs=pl.BlockSpec((blk, ATT_Q_WIDTH), lambda i: (i, 0)),
        out_shape=jax.ShapeDtypeStruct((t, ATT_Q_WIDTH), BF16),
        compiler_params=_params("arbitrary"),
        name="swa_attention",
    )(sinks, aq, akv, akv)


def _ret_constants():
    c = RET_CHUNK
    gam = 1.0 - 2.0 ** (-5.0 - np.arange(RET_HEADS, dtype=np.float64))
    lg = np.log(gam)
    pos = np.arange(c, dtype=np.float64)
    diff = pos[:, None] - pos[None, :]
    kscale = RET_KEY_DIM ** -0.5
    inner = np.where(diff >= 0, np.exp(np.maximum(diff, 0.0) * lg[:, None, None]), 0.0) * kscale
    qdec = np.exp((pos + 1.0) * lg[:, None])
    kdec = np.exp((c - 1.0 - pos) * lg[:, None]) * kscale
    cdec = np.exp(c * lg)
    qdec_b = np.broadcast_to(qdec[:, :, None], (RET_HEADS, c, RET_VAL_DIM))
    kdec_b = np.broadcast_to(kdec[:, :, None], (RET_HEADS, c, RET_KEY_DIM))
    return (jnp.asarray(inner, F32), jnp.asarray(qdec_b, F32), jnp.asarray(kdec_b, F32),
            [float(v) for v in cdec])


def _ret_kernel(qk_ref, v_ref, g_ref, ng_ref, inner_ref, qdec_ref, kdec_ref, o_ref, state_ref, *, cdec):
    @pl.when(pl.program_id(1) == 0)
    def _():
        state_ref[...] = jnp.zeros_like(state_ref)

    dk, dv = RET_KEY_DIM, RET_VAL_DIM
    for h in range(RET_HEADS):
        q = qk_ref[:, h * dk:(h + 1) * dk]
        k = qk_ref[:, RET_QK_WIDTH + h * dk:RET_QK_WIDTH + (h + 1) * dk]
        v = v_ref[:, h * dv:(h + 1) * dv]
        st = state_ref[h]
        qk = lax.dot_general(q, k, (((1,), (1,)), ((), ())), preferred_element_type=F32)
        a = (qk * inner_ref[h]).astype(BF16)
        inner = jnp.dot(a, v, preferred_element_type=F32)
        cross = jnp.dot(q, st.astype(BF16), preferred_element_type=F32) * qdec_ref[h]
        kd = (k.astype(F32) * kdec_ref[h]).astype(BF16)
        upd = lax.dot_general(kd, v, (((0,), (0,)), ((), ())), preferred_element_type=F32)
        state_ref[h] = st * cdec[h] + upd
        o = inner + cross
        mu = jnp.mean(o, axis=-1, keepdims=True)
        d = o - mu
        var = jnp.mean(d * d, axis=-1, keepdims=True)
        y = d * lax.rsqrt(var + LN_EPS) * ng_ref[:, h * dv:(h + 1) * dv]
        g = g_ref[:, h * dv:(h + 1) * dv].astype(F32)
        o_ref[:, h * dv:(h + 1) * dv] = (g * _sigmoid(g) * y).astype(o_ref.dtype)


def _retention(rqk, rv, rg, norm_g, batch, nchunk):
    t = rqk.shape[0]
    c = RET_CHUNK
    inner, qdec, kdec, cdec = _ret_constants()
    row = lambda b, n: (b * nchunk + n, 0)
    const3 = lambda b, n: (0, 0, 0)
    return pl.pallas_call(
        functools.partial(_ret_kernel, cdec=cdec),
        grid=(batch, nchunk),
        in_specs=[pl.BlockSpec((c, 2 * RET_QK_WIDTH), row),
                  pl.BlockSpec((c, RET_V_WIDTH), row),
                  pl.BlockSpec((c, RET_V_WIDTH), row),
                  pl.BlockSpec((1, RET_V_WIDTH), lambda b, n: (0, 0)),
                  pl.BlockSpec((RET_HEADS, c, c), const3),
                  pl.BlockSpec((RET_HEADS, c, RET_VAL_DIM), const3),
                  pl.BlockSpec((RET_HEADS, c, RET_KEY_DIM), const3)],
        out_specs=pl.BlockSpec((c, RET_V_WIDTH), row),
        out_shape=jax.ShapeDtypeStruct((t, RET_V_WIDTH), BF16),
        scratch_shapes=[pltpu.VMEM((RET_HEADS, RET_KEY_DIM, RET_VAL_DIM), F32)],
        compiler_params=_params("arbitrary", "arbitrary"),
        name="retention",
    )(rqk, rv, rg, norm_g.reshape(1, RET_V_WIDTH), inner, qdec, kdec)


def _ln_rows(z, g, b):
    mu = jnp.mean(z, axis=-1, keepdims=True)
    d = z - mu
    var = jnp.mean(d * d, axis=-1, keepdims=True)
    return d * lax.rsqrt(var + LN_EPS) * g + b


def _ln_router_kernel(z_ref, g_ref, b_ref, wr_ref, br_ref, x_ref, xp_ref, gate_ref, idx_ref):
    x = _ln_rows(z_ref[...], g_ref[...], b_ref[...])
    x_ref[...] = x
    tm, d = x.shape
    half = d // 2
    nsl = half // LANES
    for s in range(nsl):
        cs = slice(s * LANES, (s + 1) * LANES)
        hs = slice(half + s * LANES, half + (s + 1) * LANES)
        xp_ref[pl.ds(s, tm, stride=nsl), :] = _pack_words(x[:, cs], x[:, hs])
    x_hi = x.astype(BF16)
    x_lo = (x - x_hi.astype(F32)).astype(BF16)
    p_hi = jnp.dot(x_hi, wr_ref[...], preferred_element_type=F32)
    p_lo = jnp.dot(x_lo, wr_ref[...], preferred_element_type=F32)
    logits = p_hi + pltpu.roll(p_hi, ROUTER_PAD - N_EXPERTS, axis=1) + p_lo + br_ref[...]
    lane = lax.broadcasted_iota(jnp.int32, logits.shape, 1)
    lane_f = lane.astype(F32)
    vals, idxs = [], []
    cur = logits
    for _ in range(TOP_K):
        m = jnp.max(cur, axis=-1, keepdims=True)
        idx_f = jnp.min(jnp.where(cur == m, lane_f, float(ROUTER_PAD)), axis=-1, keepdims=True)
        idx = idx_f.astype(jnp.int32)
        vals.append(m)
        idxs.append(idx)
        cur = jnp.where(lane == idx, -jnp.inf, cur)
    es = [jnp.exp(v - vals[0]) for v in vals]
    inv = 1.0 / (es[0] + es[1] + es[2] + es[3])
    gates = jnp.zeros(logits.shape, F32)
    ids = jnp.zeros(logits.shape, jnp.int32)
    for k in range(TOP_K):
        gates = jnp.where(lane == k, es[k] * inv, gates)
        ids = jnp.where(lane == k, idxs[k], ids)
    gate_ref[...] = gates
    idx_ref[...] = ids


def _ln_router(z, g, b, w_router, b_router):
    t, d = z.shape
    tm = min(LN_TM, t)
    w32 = w_router.astype(F32)
    w_hi = w32.astype(BF16)
    w_lo = (w32 - w_hi.astype(F32)).astype(BF16)
    wr = jnp.zeros((d, ROUTER_PAD), BF16).at[:, :N_EXPERTS].set(w_hi).at[:, N_EXPERTS:2 * N_EXPERTS].set(w_lo)
    br = jnp.full((1, ROUTER_PAD), NEG_BIG, F32).at[0, :N_EXPERTS].set(b_router.astype(F32))
    row = lambda i: (i, 0)
    const = lambda i: (0, 0)
    nsl = d // 2 // LANES
    return pl.pallas_call(
        _ln_router_kernel,
        grid=(t // tm,),
        in_specs=[pl.BlockSpec((tm, d), row),
                  pl.BlockSpec((1, d), const),
                  pl.BlockSpec((1, d), const),
                  pl.BlockSpec((d, ROUTER_PAD), const),
                  pl.BlockSpec((1, ROUTER_PAD), const)],
        out_specs=[pl.BlockSpec((tm, d), row),
                   pl.BlockSpec((tm * nsl, LANES), row),
                   pl.BlockSpec((tm, ROUTER_PAD), row),
                   pl.BlockSpec((tm, ROUTER_PAD), row)],
        out_shape=[jax.ShapeDtypeStruct((t, d), F32),
                   jax.ShapeDtypeStruct((t * nsl, LANES), jnp.uint32),
                   jax.ShapeDtypeStruct((t, ROUTER_PAD), F32),
                   jax.ShapeDtypeStruct((t, ROUTER_PAD), jnp.int32)],
        compiler_params=_params("arbitrary"),
        name="ln_router",
    )(z, g.reshape(1, d), b.reshape(1, d), wr, br)


def _combine_ln_kernel(x_ref, y0_ref, y1_ref, y2_ref, y3_ref, gate_ref, g_ref, b_ref, o_ref, ob_ref):
    tm, d = x_ref.shape
    nsl = d // 2 // LANES
    gt = gate_ref[...]
    gk = [jnp.broadcast_to(gt[:, k:k + 1], (tm, LANES)) for k in range(TOP_K)]
    los, his = [], []
    for s in range(nsl):
        lo = hi = None
        for k, y_ref in enumerate((y0_ref, y1_ref, y2_ref, y3_ref)):
            w = y_ref[pl.ds(s, tm, stride=nsl), :]
            tl, th = gk[k] * _unpack_lo(w), gk[k] * _unpack_hi(w)
            lo = tl if lo is None else lo + tl
            hi = th if hi is None else hi + th
        los.append(lo)
        his.append(hi)
    z = DEEPNORM_ALPHA * x_ref[...] + jnp.concatenate(los + his, axis=1)
    out = _ln_rows(z, g_ref[...], b_ref[...])
    o_ref[...] = out
    ob_ref[...] = out.astype(ob_ref.dtype)


def _combine_ln(x1, yp, gates, g, b):
    t, d = x1.shape
    tm = min(COMBINE_TM, t)
    nrb = t // tm
    nsl = d // 2 // LANES
    row = lambda i: (i, 0)
    const = lambda i: (0, 0)
    y_specs = [pl.BlockSpec((tm * nsl, LANES), functools.partial(lambda i, k: (k * nrb + i, 0), k=k))
               for k in range(TOP_K)]
    return pl.pallas_call(
        _combine_ln_kernel,
        grid=(nrb,),
        in_specs=[pl.BlockSpec((tm, d), row)] + y_specs +
                 [pl.BlockSpec((tm, ROUTER_PAD), row),
                  pl.BlockSpec((1, d), const),
                  pl.BlockSpec((1, d), const)],
        out_specs=[pl.BlockSpec((tm, d), row), pl.BlockSpec((tm, d), row)],
        out_shape=[jax.ShapeDtypeStruct((t, d), F32), jax.ShapeDtypeStruct((t, d), BF16)],
        compiler_params=_params("arbitrary"),
        name="combine_ln",
    )(x1, yp, yp, yp, yp, gates, g.reshape(1, d), b.reshape(1, d))


def _ln_kernel(z_ref, g_ref, b_ref, o_ref):
    o_ref[...] = _ln_rows(z_ref[...], g_ref[...], b_ref[...])


def _ln(z, g, b):
    t, d = z.shape
    tm = min(LN_TM, t)
    return pl.pallas_call(
        _ln_kernel,
        grid=(t // tm,),
        in_specs=[pl.BlockSpec((tm, d), lambda i: (i, 0)),
                  pl.BlockSpec((1, d), lambda i: (0, 0)),
                  pl.BlockSpec((1, d), lambda i: (0, 0))],
        out_specs=pl.BlockSpec((tm, d), lambda i: (i, 0)),
        out_shape=jax.ShapeDtypeStruct((t, d), F32),
        compiler_params=_params("arbitrary"),
        name="layer_norm",
    )(z, g.reshape(1, d), b.reshape(1, d))


def _moe_kernel(be_ref, nu_ref, hf_ref, idx_hbm, x_hbm, wg_ref, wu_ref, bg_ref, bu_ref,
                wdl_ref, wdh_ref, bdl_ref, bdh_ref, y_hbm, idx_smem, xg, xb, act, ybuf, sem_idx, sem_g, sem_s, *, bm, ngu, ntok):
    del be_ref
    i = pl.program_id(0)
    j = pl.program_id(1)
    nu = nu_ref[0]
    active = i < nu
    d = xb.shape[1]

    def with_rows(item, fn):
        is_half = hf_ref[item] == 1

        @pl.when(is_half)
        def _():
            fn(bm // 2)

        @pl.when(jnp.logical_not(is_half))
        def _():
            fn(bm)

    half = d // 2
    nsl = half // LANES
    td = wdl_ref.shape[2]

    def idx_copy(item, slot):
        src = idx_hbm.at[pl.ds(pl.multiple_of(item * (2 * bm), 2 * bm), 2 * bm)]
        dst = idx_smem.at[pl.ds(pl.multiple_of(slot * (2 * bm), 2 * bm), 2 * bm)]
        return pltpu.make_async_copy(src, dst, sem_idx)

    def for_each_row(slot, table, rows, fn):
        base = slot * (2 * bm) + table * bm

        def body(r, c):
            fn(idx_smem[base + r], r)
            return c
        lax.fori_loop(0, rows, body, 0, unroll=8)

    def vmem_slab(buf, r):
        return buf.at[pl.ds(pl.multiple_of(r * SLAB_PITCH, 8), nsl), :]

    def hbm_slab(ref, row):
        return ref.at[pl.ds(pl.multiple_of(row, 8), nsl), :]

    def start_gather(slot, rows):
        def one(src, r):
            pltpu.make_async_copy(hbm_slab(x_hbm, src), vmem_slab(xg, r), sem_g).start()
        for_each_row(slot, 0, rows, one)

    def wait_gather(rows):
        pltpu.make_async_copy(x_hbm.at[pl.ds(0, rows * nsl), :], xg.at[pl.ds(0, rows * nsl), :], sem_g).wait()

    def start_scatter(slot, rows):
        def one(dst, r):
            pltpu.make_async_copy(vmem_slab(ybuf, r), hbm_slab(y_hbm, dst), sem_s).start()
        for_each_row(slot, 1, rows, one)

    def wait_scatter(rows):
        pltpu.make_async_copy(ybuf.at[pl.ds(0, rows * nsl), :], y_hbm.at[pl.ds(0, rows * nsl), :], sem_s).wait()

    def unpack(rows):
        wait_gather(rows)
        for s in range(nsl):
            w = xg[pl.ds(s, rows, stride=SLAB_PITCH), :]
            xb[:rows, s * LANES:(s + 1) * LANES] = _unpack_lo(w).astype(BF16)
            xb[:rows, half + s * LANES:half + (s + 1) * LANES] = _unpack_hi(w).astype(BF16)

    @pl.when(active & (j == 0))
    def _():
        @pl.when(i == 0)
        def _():
            idx_copy(0, 0).start()
            idx_copy(0, 0).wait()
            with_rows(0, lambda rows: start_gather(0, rows))
            ybuf[...] = jnp.zeros_like(ybuf)
            pad = pltpu.make_async_copy(ybuf.at[pl.ds(0, bm * nsl), :],
                                        y_hbm.at[pl.ds(TOP_K * ntok * nsl, bm * nsl), :], sem_s)
            pad.start()
            pad.wait()
        nxt = lax.rem(i + 1, 2)

        @pl.when(i + 1 < nu)
        def _():
            idx_copy(i + 1, nxt).start()

        with_rows(i, unpack)

        @pl.when(i + 1 < nu)
        def _():
            idx_copy(i + 1, nxt).wait()
            with_rows(i + 1, lambda rows: start_gather(nxt, rows))

    def gate_up(rows):
        x = xb[:rows, :]
        g = jnp.dot(x, wg_ref[0], preferred_element_type=F32) + bg_ref[0]
        u = jnp.dot(x, wu_ref[0], preferred_element_type=F32) + bu_ref[0]
        g = jnp.minimum(g, SWIGLU_LIMIT)
        u = jnp.clip(u, -SWIGLU_LIMIT, SWIGLU_LIMIT)
        act[j, :rows, :] = (g * _sigmoid(SWIGLU_ALPHA * g) * (u + 1.0)).astype(BF16)

    @pl.when(active & (j < ngu))
    def _():
        with_rows(i, gate_up)

    def down(rows):
        c = j - ngu
        a = jnp.concatenate([act[q, :rows, :] for q in range(ngu)], axis=1)
        for p in range(td // MOE_DC):
            cs = slice(p * MOE_DC, (p + 1) * MOE_DC)
            lo = jnp.dot(a, wdl_ref[0, :, cs], preferred_element_type=F32) + bdl_ref[0, :, cs]
            hi = jnp.dot(a, wdh_ref[0, :, cs], preferred_element_type=F32) + bdh_ref[0, :, cs]
            words = _pack_words(lo, hi)
            for q in range(MOE_DC // LANES):
                s = c * (td // LANES) + p * (MOE_DC // LANES) + q
                ybuf[pl.ds(s, rows, stride=SLAB_PITCH), :] = words[:, q * LANES:(q + 1) * LANES]

    @pl.when(active & (j == ngu))
    def _():
        @pl.when(i > 0)
        def _():
            with_rows(i - 1, wait_scatter)

    @pl.when(active & (j >= ngu))
    def _():
        with_rows(i, down)

    @pl.when(active & (j == ngu + half // td - 1))
    def _():
        def finish(rows):
            start_scatter(lax.rem(i, 2), rows)

            @pl.when(i == nu - 1)
            def _():
                wait_scatter(rows)
        with_rows(i, finish)


def _moe(xp, item_dst, block_expert, is_half, n_used, wgu, bgu, wd, bd, t, d):
    bm, tf, td = MOE_BM, MOE_TF, MOE_TD
    half = d // 2
    nsl = half // LANES
    assert half % td == 0 and td % MOE_DC == 0 and EXPERT_FF % tf == 0
    assert nsl % 8 == 0 and SLAB_PITCH % 8 == 0 and SLAB_PITCH >= nsl
    n_items = item_dst.shape[0]
    ngu = EXPERT_FF // tf
    nd = half // td
    nj = ngu + nd
    idx3 = jnp.stack([(item_dst % t) * nsl, item_dst * nsl], axis=1).reshape(n_items * 2 * bm)
    bgu3 = bgu.reshape(N_EXPERTS, 1, 2 * EXPERT_FF)
    bd3 = bd.reshape(N_EXPERTS, 1, d)

    def step(i, j, nu):
        return jnp.where(i < nu[0], j, nj - 1)

    def gu(i, j, nu):
        return jnp.minimum(step(i, j, nu), ngu - 1)

    def dn(i, j, nu):
        return jnp.maximum(step(i, j, nu) - ngu, 0)

    def wd_block(i, j, be, nu, base):
        in_down = step(i, j, nu) >= ngu
        e = jnp.where(in_down, be[i], be[jnp.maximum(i - 1, 0)])
        return (e, 0, base + jnp.where(in_down, dn(i, j, nu), nd - 1))

    grid_spec = pltpu.PrefetchScalarGridSpec(
        num_scalar_prefetch=3,
        grid=(n_items, nj),
        in_specs=[
            pl.BlockSpec(memory_space=pl.ANY),
            pl.BlockSpec(memory_space=pl.ANY),
            pl.BlockSpec((1, d, tf), lambda i, j, be, nu, hf: (be[i], 0, gu(i, j, nu))),
            pl.BlockSpec((1, d, tf), lambda i, j, be, nu, hf: (be[i], 0, gu(i, j, nu) + ngu)),
            pl.BlockSpec((1, 1, tf), lambda i, j, be, nu, hf: (be[i], 0, gu(i, j, nu))),
            pl.BlockSpec((1, 1, tf), lambda i, j, be, nu, hf: (be[i], 0, gu(i, j, nu) + ngu)),
            pl.BlockSpec((1, EXPERT_FF, td), lambda i, j, be, nu, hf: wd_block(i, j, be, nu, 0)),
            pl.BlockSpec((1, EXPERT_FF, td), lambda i, j, be, nu, hf: wd_block(i, j, be, nu, nd)),
            pl.BlockSpec((1, 1, td), lambda i, j, be, nu, hf: (be[i], 0, dn(i, j, nu))),
            pl.BlockSpec((1, 1, td), lambda i, j, be, nu, hf: (be[i], 0, nd + dn(i, j, nu))),
        ],
        out_specs=pl.BlockSpec(memory_space=pl.ANY),
        scratch_shapes=[
            pltpu.SMEM((2 * 2 * bm,), jnp.int32),
            pltpu.VMEM((bm * SLAB_PITCH, LANES), jnp.uint32),
            pltpu.VMEM((bm, d), BF16),
            pltpu.VMEM((ngu, bm, tf), BF16),
            pltpu.VMEM((bm * SLAB_PITCH, LANES), jnp.uint32),
            pltpu.SemaphoreType.DMA(()),
            pltpu.SemaphoreType.DMA(()),
            pltpu.SemaphoreType.DMA(()),
        ],
    )
    return pl.pallas_call(
        functools.partial(_moe_kernel, bm=bm, ngu=ngu, ntok=t),
        grid_spec=grid_spec,
        out_shape=jax.ShapeDtypeStruct(((TOP_K * t + bm) * nsl, LANES), jnp.uint32),
        compiler_params=_params("arbitrary", "arbitrary"),
        name="moe_experts",
    )(block_expert, n_used, is_half, idx3, xp, wgu, wgu, bgu3, bgu3, wd, wd, bd3, bd3)


def _routing_tables(top_idx, t):
    bm = MOE_BM
    hb = bm // 2
    tk = t * TOP_K
    n_items = tk // bm + N_EXPERTS
    n_rows = (tk // hb + N_EXPERTS) * hb + bm
    flat_e = top_idx.T.reshape(tk)
    experts = jnp.arange(N_EXPERTS, dtype=jnp.int32)
    order = jnp.argsort(flat_e, stable=True).astype(jnp.int32)
    counts = jnp.sum((flat_e[:, None] == experts[None, :]).astype(jnp.int32), axis=0)
    halves = (counts + hb - 1) // hb
    padded = halves * hb
    group_end = jnp.cumsum(counts)
    group_start = group_end - counts
    padded_end = jnp.cumsum(padded)
    padded_start = padded_end - padded
    items_per = (halves + 1) // 2
    item_end = jnp.cumsum(items_per)
    item_first = item_end - items_per

    def bucket(ends, v):
        e = jnp.sum((ends[None, :] <= v[:, None]).astype(jnp.int32), axis=1)
        return jnp.minimum(e, N_EXPERTS - 1)

    def lookup(table, e):
        return jnp.sum(jnp.where(e[:, None] == experts[None, :], table[None, :], 0), axis=1)

    rows = jnp.arange(n_rows, dtype=jnp.int32)
    row_e = bucket(padded_end, rows)
    pos = rows - lookup(padded_start - group_start, row_e)
    valid = pos < lookup(group_end, row_e)
    row_dst = jnp.where(valid, order[jnp.clip(pos, 0, tk - 1)], tk + rows % bm)

    n_used = item_end[-1].astype(jnp.int32)
    items = jnp.minimum(jnp.arange(n_items, dtype=jnp.int32), n_used - 1)
    item_e = bucket(item_end, items)
    local = items - lookup(item_first, item_e)
    is_half = (lookup(halves, item_e) - 2 * local == 1).astype(jnp.int32)
    first_row = lookup(padded_start, item_e) + local * bm
    item_dst = row_dst[first_row[:, None] + jnp.arange(bm, dtype=jnp.int32)[None, :]]
    return item_dst, item_e, is_half, n_used.reshape(1)


def kernel(x, p, w_in, attn_sinks, ret_norm_g, w_att_out, w_ret_out, w_out, ln1_g, ln1_b,
           w_router, b_router, w_gate_up, b_gate_up, w_down, b_down, ln2_g, ln2_b,
           w_ple, w_ple_gate, ln3_g, ln3_b):
    bsz, seq, d = x.shape
    t = bsz * seq
    assert seq % ATT_BLOCK == 0 and seq % RET_CHUNK == 0
    xf = x.reshape(t, d)
    for i in range(DEPTH):
        w_in_i = w_in[i].astype(F32)
        widths = (("aq", ATT_Q_WIDTH), ("akv", 2 * ATT_KV_WIDTH), ("rqk", 2 * RET_QK_WIDTH),
                  ("rv", RET_V_WIDTH), ("rg", RET_V_WIDTH), ("ga", d), ("gr", d))
        offs, off = {}, 0
        for name, width in widths:
            offs[name] = off
            off += width
        assert 2 * ATT_KV_WIDTH == MM_TN
        segs = {}
        segs["akv"], xb = _proj_cast(xf.astype(F32), w_in_i, offs["akv"], "proj_akv")
        for name, width in widths:
            if name != "akv":
                segs[name] = _proj(xb, w_in_i, offs[name], width, "proj_" + name)
        y_att = _attention(segs["aq"], segs["akv"], attn_sinks[i].astype(F32), seq // ATT_BLOCK)
        y_ret = _retention(segs["rqk"], segs["rv"], segs["rg"], ret_norm_g[i].astype(F32),
                           bsz, seq // RET_CHUNK)
        merged = _merge(y_att, y_ret, w_att_out[i].astype(F32), w_ret_out[i].astype(F32),
                        segs["ga"], segs["gr"])
        z1 = _resid_mm(merged, w_out[i].astype(F32), xf)
        x1, x1p, gates, top_idx = _ln_router(z1, ln1_g[i], ln1_b[i], w_router[i], b_router[i])
        item_dst, block_expert, is_half, n_used = _routing_tables(top_idx[:, :TOP_K], t)
        yp = _moe(x1p, item_dst, block_expert, is_half, n_used, w_gate_up[i].astype(BF16),
                  b_gate_up[i].astype(F32), w_down[i].astype(BF16), b_down[i].astype(F32), t, d)
        x2, x2b = _combine_ln(x1, yp, gates, ln2_g[i], ln2_b[i])
        z3 = _ple(x2b, w_ple_gate[i].astype(F32), p[i].reshape(t, -1), w_ple[i].astype(F32), x2)
        xf = _ln(z3, ln3_g[i], ln3_b[i])
    return xf.reshape(bsz, seq, d)
```

```python
import functools
import math

import jax
import jax.numpy as jnp
import numpy as np
from jax import lax
from jax.experimental import pallas as pl
from jax.experimental.pallas import tpu as pltpu

F32 = jnp.float32
BF16 = jnp.bfloat16

ATT_HEADS = 32
ATT_KV_HEADS = 4
ATT_GROUP = ATT_HEADS // ATT_KV_HEADS
ATT_HEAD_DIM = 64
ATT_BLOCK = 128
RET_HEADS = 8
RET_KEY_DIM = 128
RET_VAL_DIM = 256
RET_CHUNK = 128
N_EXPERTS = 32
TOP_K = 4
EXPERT_FF = 1536
SWIGLU_LIMIT = 7.0
SWIGLU_ALPHA = 1.702
LN_EPS = 1e-5
DEPTH = 1
DEEPNORM_ALPHA = float((2 * DEPTH) ** 0.25)

ATT_Q_WIDTH = ATT_HEADS * ATT_HEAD_DIM
ATT_KV_WIDTH = ATT_KV_HEADS * ATT_HEAD_DIM
RET_QK_WIDTH = RET_HEADS * RET_KEY_DIM
RET_V_WIDTH = RET_HEADS * RET_VAL_DIM

LANES = 128
VMEM_LIMIT_BYTES = 56 * 1024 * 1024
MM_TM = 1024
MM_TN = 512
LN_TM = 256
COMBINE_TM = 128
MOE_BM = 512
MOE_PARTS = 4
MOE_TF = 512
MOE_TD = 1024
MOE_DC = 512
ROUTER_PAD = LANES
NEG_BIG = -1e30
SLAB_PITCH = 24
HI_MASK = 0xFFFF0000


def _params(*sem):
    return pltpu.CompilerParams(dimension_semantics=sem, vmem_limit_bytes=VMEM_LIMIT_BYTES)


def _sigmoid(x):
    return 1.0 / (1.0 + jnp.exp(-x))


def _bf16_bits(x):
    return lax.bitcast_convert_type(x.astype(BF16).astype(F32), jnp.uint32)


def _pack_words(lo, hi):
    return (_bf16_bits(lo) >> 16) | _bf16_bits(hi)


def _unpack_lo(w):
    return lax.bitcast_convert_type(w << 16, F32)


def _unpack_hi(w):
    return lax.bitcast_convert_type(w & jnp.uint32(HI_MASK), F32)


def _cast_at_first_row_tile(w_ref, wb_ref):
    @pl.when(pl.program_id(1) == 0)
    def _():
        wb_ref[...] = w_ref[...].astype(BF16)


def _proj_kernel(x_ref, w_ref, o_ref, wb_ref):
    _cast_at_first_row_tile(w_ref, wb_ref)
    o_ref[...] = jnp.dot(x_ref[...], wb_ref[...], preferred_element_type=F32).astype(o_ref.dtype)


def _proj(xb, w, col_off, ncols, name):
    m, k = xb.shape
    tm, tn = min(MM_TM, m), MM_TN
    assert m % tm == 0 and ncols % tn == 0 and col_off % tn == 0
    off = col_off // tn
    return pl.pallas_call(
        _proj_kernel,
        grid=(ncols // tn, m // tm),
        in_specs=[pl.BlockSpec((tm, k), lambda j, i: (i, 0)),
                  pl.BlockSpec((k, tn), lambda j, i: (0, j + off))],
        out_specs=pl.BlockSpec((tm, tn), lambda j, i: (i, j)),
        out_shape=jax.ShapeDtypeStruct((m, ncols), BF16),
        scratch_shapes=[pltpu.VMEM((k, tn), BF16)],
        compiler_params=_params("arbitrary", "arbitrary"),
        name=name,
    )(xb, w)


def _proj_cast_kernel(x_ref, w_ref, o_ref, xb_ref, wb_ref):
    _cast_at_first_row_tile(w_ref, wb_ref)
    xb = x_ref[...].astype(BF16)
    xb_ref[...] = xb
    o_ref[...] = jnp.dot(xb, wb_ref[...], preferred_element_type=F32).astype(o_ref.dtype)


def _proj_cast(x, w, col_off, name):
    m, k = x.shape
    tm, tn = min(MM_TM // 2, m), MM_TN
    assert m % tm == 0 and col_off % tn == 0
    off = col_off // tn
    return pl.pallas_call(
        _proj_cast_kernel,
        grid=(1, m // tm),
        in_specs=[pl.BlockSpec((tm, k), lambda j, i: (i, 0)),
                  pl.BlockSpec((k, tn), lambda j, i: (0, off))],
        out_specs=[pl.BlockSpec((tm, tn), lambda j, i: (i, 0)),
                   pl.BlockSpec((tm, k), lambda j, i: (i, 0))],
        out_shape=[jax.ShapeDtypeStruct((m, tn), BF16), jax.ShapeDtypeStruct((m, k), BF16)],
        scratch_shapes=[pltpu.VMEM((k, tn), BF16)],
        compiler_params=_params("arbitrary", "arbitrary"),
        name=name,
    )(x, w)


def _merge_kernel(ya_ref, yr_ref, wa_ref, wr_ref, ga_ref, gr_ref, o_ref, wab_ref, wrb_ref):
    _cast_at_first_row_tile(wa_ref, wab_ref)
    _cast_at_first_row_tile(wr_ref, wrb_ref)
    a = jnp.dot(ya_ref[...], wab_ref[...], preferred_element_type=F32)
    r = jnp.dot(yr_ref[...], wrb_ref[...], preferred_element_type=F32)
    ga = _sigmoid(ga_ref[...].astype(F32))
    gr = _sigmoid(gr_ref[...].astype(F32))
    o_ref[...] = (ga * a + gr * r).astype(o_ref.dtype)


def _merge(ya, yr, wa, wr, ga, gr):
    m, ka = ya.shape
    kr = yr.shape[1]
    n = wa.shape[1]
    tm, tn = min(MM_TM, m), MM_TN
    return pl.pallas_call(
        _merge_kernel,
        grid=(n // tn, m // tm),
        in_specs=[pl.BlockSpec((tm, ka), lambda j, i: (i, 0)),
                  pl.BlockSpec((tm, kr), lambda j, i: (i, 0)),
                  pl.BlockSpec((ka, tn), lambda j, i: (0, j)),
                  pl.BlockSpec((kr, tn), lambda j, i: (0, j)),
                  pl.BlockSpec((tm, tn), lambda j, i: (i, j)),
                  pl.BlockSpec((tm, tn), lambda j, i: (i, j))],
        out_specs=pl.BlockSpec((tm, tn), lambda j, i: (i, j)),
        out_shape=jax.ShapeDtypeStruct((m, n), BF16),
        scratch_shapes=[pltpu.VMEM((ka, tn), BF16), pltpu.VMEM((kr, tn), BF16)],
        compiler_params=_params("arbitrary", "arbitrary"),
        name="merge",
    )(ya, yr, wa, wr, ga, gr)


def _resid_mm_kernel(a_ref, w_ref, x_ref, o_ref, wb_ref):
    _cast_at_first_row_tile(w_ref, wb_ref)
    acc = jnp.dot(a_ref[...], wb_ref[...], preferred_element_type=F32)
    o_ref[...] = DEEPNORM_ALPHA * x_ref[...] + acc


def _resid_mm(a, w, x):
    m, k = a.shape
    n = w.shape[1]
    tm, tn = min(MM_TM, m), MM_TN
    return pl.pallas_call(
        _resid_mm_kernel,
        grid=(n // tn, m // tm),
        in_specs=[pl.BlockSpec((tm, k), lambda j, i: (i, 0)),
                  pl.BlockSpec((k, tn), lambda j, i: (0, j)),
                  pl.BlockSpec((tm, tn), lambda j, i: (i, j))],
        out_specs=pl.BlockSpec((tm, tn), lambda j, i: (i, j)),
        out_shape=jax.ShapeDtypeStruct((m, n), F32),
        scratch_shapes=[pltpu.VMEM((k, tn), BF16)],
        compiler_params=_params("arbitrary", "arbitrary"),
        name="resid_mm",
    )(a, w, x)


def _ple_kernel(xb_ref, wg_ref, p_ref, wp_ref, x_ref, o_ref, wgb_ref, wpb_ref):
    _cast_at_first_row_tile(wg_ref, wgb_ref)
    _cast_at_first_row_tile(wp_ref, wpb_ref)
    gate = jnp.dot(xb_ref[...], wgb_ref[...], preferred_element_type=F32)
    emb = jnp.dot(p_ref[...].astype(BF16), wpb_ref[...], preferred_element_type=F32)
    o_ref[...] = DEEPNORM_ALPHA * x_ref[...] + emb * _sigmoid(gate)


def _ple(xb, wg, p, wp, x):
    m, k = xb.shape
    n = wg.shape[1]
    kp = p.shape[1]
    tm, tn = min(MM_TM, m), MM_TN
    return pl.pallas_call(
        _ple_kernel,
        grid=(n // tn, m // tm),
        in_specs=[pl.BlockSpec((tm, k), lambda j, i: (i, 0)),
                  pl.BlockSpec((k, tn), lambda j, i: (0, j)),
                  pl.BlockSpec((tm, kp), lambda j, i: (i, 0)),
                  pl.BlockSpec((kp, tn), lambda j, i: (0, j)),
                  pl.BlockSpec((tm, tn), lambda j, i: (i, j))],
        out_specs=pl.BlockSpec((tm, tn), lambda j, i: (i, j)),
        out_shape=jax.ShapeDtypeStruct((m, n), F32),
        scratch_shapes=[pltpu.VMEM((k, tn), BF16), pltpu.VMEM((kp, tn), BF16)],
        compiler_params=_params("arbitrary", "arbitrary"),
        name="ple",
    )(xb, wg, p, wp, x)


def _alibi_slope(h):
    return float(2.0 ** (-8.0 * (h + 1) / ATT_HEADS))


def _attn_kernel(sink_ref, q_ref, kvc_ref, kvp_ref, o_ref, *, nblk):
    blk = ATT_BLOCK
    n = lax.rem(pl.program_id(0), nblk)
    has_prev = n > 0
    row = lax.broadcasted_iota(jnp.int32, (blk, 2 * blk), 0)
    col = lax.broadcasted_iota(jnp.int32, (blk, 2 * blk), 1)
    dist = blk + row - col
    valid = (dist >= 0) & (dist < blk) & ((col >= blk) | has_prev)
    distf = dist.astype(F32)
    lane = lax.broadcasted_iota(jnp.int32, (2 * blk, LANES), 1)
    lo = lane < ATT_HEAD_DIM
    scale = ATT_HEAD_DIM ** -0.5

    def halves(t, half):
        swapped = jnp.concatenate([t[:, ATT_HEAD_DIM:], t[:, :ATT_HEAD_DIM]], axis=1)
        zero = jnp.zeros_like(t)
        if half == 0:
            return jnp.where(lo, t, zero), jnp.where(lo, zero, swapped)
        return jnp.where(lo, swapped, zero), jnp.where(lo, zero, t)

    for kh in range(ATT_KV_HEADS):
        tile, half = kh // 2, kh % 2
        ks = slice(tile * LANES, (tile + 1) * LANES)
        vs = slice(ATT_KV_WIDTH + tile * LANES, ATT_KV_WIDTH + (tile + 1) * LANES)
        kcat = jnp.concatenate([kvp_ref[:, ks], kvc_ref[:, ks]], axis=0)
        vcat = jnp.concatenate([kvp_ref[:, vs], kvc_ref[:, vs]], axis=0)
        k_lo, k_hi = halves(kcat, half)
        v_lo, v_hi = halves(vcat, half)
        for jp in range(ATT_GROUP // 2):
            h0 = kh * ATT_GROUP + 2 * jp
            qs = slice(h0 * ATT_HEAD_DIM, h0 * ATT_HEAD_DIM + LANES)
            qp = q_ref[:, qs]
            acc = jnp.zeros((blk, LANES), F32)
            for par, (kx, vx) in enumerate(((k_lo, v_lo), (k_hi, v_hi))):
                h = h0 + par
                s = lax.dot_general(qp, kx, (((1,), (1,)), ((), ())), preferred_element_type=F32)
                s = s * scale - _alibi_slope(h) * distf
                s = jnp.where(valid, s, -jnp.inf)
                sink = sink_ref[h]
                m = jnp.maximum(jnp.max(s, axis=-1, keepdims=True), sink)
                e = jnp.exp(s - m)
                denom = jnp.sum(e, axis=-1, keepdims=True) + jnp.exp(sink - m)
                pv = jnp.dot(e.astype(BF16), vx, preferred_element_type=F32)
                acc = acc + pv * (1.0 / denom)
            o_ref[:, qs] = acc.astype(o_ref.dtype)


def _attention(aq, akv, sinks, nblk):
    t = aq.shape[0]
    blk = ATT_BLOCK
    return pl.pallas_call(
        functools.partial(_attn_kernel, nblk=nblk),
        grid=(t // blk,),
        in_specs=[pl.BlockSpec(memory_space=pltpu.SMEM),
                  pl.BlockSpec((blk, ATT_Q_WIDTH), lambda i: (i, 0)),
                  pl.BlockSpec((blk, 2 * ATT_KV_WIDTH), lambda i: (i, 0)),
                  pl.BlockSpec((blk, 2 * ATT_KV_WIDTH), lambda i: (jnp.maximum(i - 1, 0), 0))],
        out_specs=pl.BlockSpec((blk, ATT_Q_WIDTH), lambda i: (i, 0)),
        out_shape=jax.ShapeDtypeStruct((t, ATT_Q_WIDTH), BF16),
        compiler_params=_params("arbitrary"),
        name="swa_attention",
    )(sinks, aq, akv, akv)


def _ret_constants():
    c = RET_CHUNK
    gam = 1.0 - 2.0 ** (-5.0 - np.arange(RET_HEADS, dtype=np.float64))
    lg = np.log(gam)
    pos = np.arange(c, dtype=np.float64)
    diff = pos[:, None] - pos[None, :]
    kscale = RET_KEY_DIM ** -0.5
    inner = np.where(diff >= 0, np.exp(np.maximum(diff, 0.0) * lg[:, None, None]), 0.0) * kscale
    qdec = np.exp((pos + 1.0) * lg[:, None])
    kdec = np.exp((c - 1.0 - pos) * lg[:, None]) * kscale
    cdec = np.exp(c * lg)
    qdec_b = np.broadcast_to(qdec[:, :, None], (RET_HEADS, c, RET_VAL_DIM))
    kdec_b = np.broadcast_to(kdec[:, :, None], (RET_HEADS, c, RET_KEY_DIM))
    return (jnp.asarray(inner, F32), jnp.asarray(qdec_b, F32), jnp.asarray(kdec_b, F32),
            [float(v) for v in cdec])


def _ret_kernel(qk_ref, v_ref, g_ref, ng_ref, inner_ref, qdec_ref, kdec_ref, o_ref, state_ref, *, cdec):
    @pl.when(pl.program_id(1) == 0)
    def _():
        state_ref[...] = jnp.zeros_like(state_ref)

    dk, dv = RET_KEY_DIM, RET_VAL_DIM
    for h in range(RET_HEADS):
        q = qk_ref[:, h * dk:(h + 1) * dk]
        k = qk_ref[:, RET_QK_WIDTH + h * dk:RET_QK_WIDTH + (h + 1) * dk]
        v = v_ref[:, h * dv:(h + 1) * dv]
        st = state_ref[h]
        qk = lax.dot_general(q, k, (((1,), (1,)), ((), ())), preferred_element_type=F32)
        a = (qk * inner_ref[h]).astype(BF16)
        inner = jnp.dot(a, v, preferred_element_type=F32)
        cross = jnp.dot(q, st.astype(BF16), preferred_element_type=F32) * qdec_ref[h]
        kd = (k.astype(F32) * kdec_ref[h]).astype(BF16)
        upd = lax.dot_general(kd, v, (((0,), (0,)), ((), ())), preferred_element_type=F32)
        state_ref[h] = st * cdec[h] + upd
        o = inner + cross
        mu = jnp.mean(o, axis=-1, keepdims=True)
        d = o - mu
        var = jnp.mean(d * d, axis=-1, keepdims=True)
        y = d * lax.rsqrt(var + LN_EPS) * ng_ref[:, h * dv:(h + 1) * dv]
        g = g_ref[:, h * dv:(h + 1) * dv].astype(F32)
        o_ref[:, h * dv:(h + 1) * dv] = (g * _sigmoid(g) * y).astype(o_ref.dtype)


def _retention(rqk, rv, rg, norm_g, batch, nchunk):
    t = rqk.shape[0]
    c = RET_CHUNK
    inner, qdec, kdec, cdec = _ret_constants()
    row = lambda b, n: (b * nchunk + n, 0)
    const3 = lambda b, n: (0, 0, 0)
    return pl.pallas_call(
        functools.partial(_ret_kernel, cdec=cdec),
        grid=(batch, nchunk),
        in_specs=[pl.BlockSpec((c, 2 * RET_QK_WIDTH), row),
                  pl.BlockSpec((c, RET_V_WIDTH), row),
                  pl.BlockSpec((c, RET_V_WIDTH), row),
                  pl.BlockSpec((1, RET_V_WIDTH), lambda b, n: (0, 0)),
                  pl.BlockSpec((RET_HEADS, c, c), const3),
                  pl.BlockSpec((RET_HEADS, c, RET_VAL_DIM), const3),
                  pl.BlockSpec((RET_HEADS, c, RET_KEY_DIM), const3)],
        out_specs=pl.BlockSpec((c, RET_V_WIDTH), row),
        out_shape=jax.ShapeDtypeStruct((t, RET_V_WIDTH), BF16),
        scratch_shapes=[pltpu.VMEM((RET_HEADS, RET_KEY_DIM, RET_VAL_DIM), F32)],
        compiler_params=_params("arbitrary", "arbitrary"),
        name="retention",
    )(rqk, rv, rg, norm_g.reshape(1, RET_V_WIDTH), inner, qdec, kdec)


def _ln_rows(z, g, b):
    mu = jnp.mean(z, axis=-1, keepdims=True)
    d = z - mu
    var = jnp.mean(d * d, axis=-1, keepdims=True)
    return d * lax.rsqrt(var + LN_EPS) * g + b


def _ln_router_kernel(z_ref, g_ref, b_ref, wr_ref, br_ref, x_ref, xp_ref, gate_ref, idx_ref):
    x = _ln_rows(z_ref[...], g_ref[...], b_ref[...])
    x_ref[...] = x
    tm, d = x.shape
    half = d // 2
    nsl = half // LANES
    for s in range(nsl):
        cs = slice(s * LANES, (s + 1) * LANES)
        hs = slice(half + s * LANES, half + (s + 1) * LANES)
        xp_ref[pl.ds(s, tm, stride=nsl), :] = _pack_words(x[:, cs], x[:, hs])
    x_hi = x.astype(BF16)
    x_lo = (x - x_hi.astype(F32)).astype(BF16)
    p_hi = jnp.dot(x_hi, wr_ref[...], preferred_element_type=F32)
    p_lo = jnp.dot(x_lo, wr_ref[...], preferred_element_type=F32)
    logits = p_hi + pltpu.roll(p_hi, ROUTER_PAD - N_EXPERTS, axis=1) + p_lo + br_ref[...]
    lane = lax.broadcasted_iota(jnp.int32, logits.shape, 1)
    lane_f = lane.astype(F32)
    vals, idxs = [], []
    cur = logits
    for _ in range(TOP_K):
        m = jnp.max(cur, axis=-1, keepdims=True)
        idx_f = jnp.min(jnp.where(cur == m, lane_f, float(ROUTER_PAD)), axis=-1, keepdims=True)
        idx = idx_f.astype(jnp.int32)
        vals.append(m)
        idxs.append(idx)
        cur = jnp.where(lane == idx, -jnp.inf, cur)
    es = [jnp.exp(v - vals[0]) for v in vals]
    inv = 1.0 / (es[0] + es[1] + es[2] + es[3])
    gates = jnp.zeros(logits.shape, F32)
    ids = jnp.zeros(logits.shape, jnp.int32)
    for k in range(TOP_K):
        gates = jnp.where(lane == k, es[k] * inv, gates)
        ids = jnp.where(lane == k, idxs[k], ids)
    gate_ref[...] = gates
    idx_ref[...] = ids


def _ln_router(z, g, b, w_router, b_router):
    t, d = z.shape
    tm = min(LN_TM, t)
    w32 = w_router.astype(F32)
    w_hi = w32.astype(BF16)
    w_lo = (w32 - w_hi.astype(F32)).astype(BF16)
    wr = jnp.zeros((d, ROUTER_PAD), BF16).at[:, :N_EXPERTS].set(w_hi).at[:, N_EXPERTS:2 * N_EXPERTS].set(w_lo)
    br = jnp.full((1, ROUTER_PAD), NEG_BIG, F32).at[0, :N_EXPERTS].set(b_router.astype(F32))
    row = lambda i: (i, 0)
    const = lambda i: (0, 0)
    nsl = d // 2 // LANES
    return pl.pallas_call(
        _ln_router_kernel,
        grid=(t // tm,),
        in_specs=[pl.BlockSpec((tm, d), row),
                  pl.BlockSpec((1, d), const),
                  pl.BlockSpec((1, d), const),
                  pl.BlockSpec((d, ROUTER_PAD), const),
                  pl.BlockSpec((1, ROUTER_PAD), const)],
        out_specs=[pl.BlockSpec((tm, d), row),
                   pl.BlockSpec((tm * nsl, LANES), row),
                   pl.BlockSpec((tm, ROUTER_PAD), row),
                   pl.BlockSpec((tm, ROUTER_PAD), row)],
        out_shape=[jax.ShapeDtypeStruct((t, d), F32),
                   jax.ShapeDtypeStruct((t * nsl, LANES), jnp.uint32),
                   jax.ShapeDtypeStruct((t, ROUTER_PAD), F32),
                   jax.ShapeDtypeStruct((t, ROUTER_PAD), jnp.int32)],
        compiler_params=_params("arbitrary"),
        name="ln_router",
    )(z, g.reshape(1, d), b.reshape(1, d), wr, br)


def _combine_ln_kernel(x_ref, y0_ref, y1_ref, y2_ref, y3_ref, gate_ref, g_ref, b_ref, o_ref, ob_ref):
    tm, d = x_ref.shape
    nsl = d // 2 // LANES
    gt = gate_ref[...]
    gk = [jnp.broadcast_to(gt[:, k:k + 1], (tm, LANES)) for k in range(TOP_K)]
    los, his = [], []
    for s in range(nsl):
        lo = hi = None
        for k, y_ref in enumerate((y0_ref, y1_ref, y2_ref, y3_ref)):
            w = y_ref[pl.ds(s, tm, stride=nsl), :]
            tl, th = gk[k] * _unpack_lo(w), gk[k] * _unpack_hi(w)
            lo = tl if lo is None else lo + tl
            hi = th if hi is None else hi + th
        los.append(lo)
        his.append(hi)
    z = DEEPNORM_ALPHA * x_ref[...] + jnp.concatenate(los + his, axis=1)
    out = _ln_rows(z, g_ref[...], b_ref[...])
    o_ref[...] = out
    ob_ref[...] = out.astype(ob_ref.dtype)


def _combine_ln(x1, yp, gates, g, b):
    t, d = x1.shape
    tm = min(COMBINE_TM, t)
    nrb = t // tm
    nsl = d // 2 // LANES
    row = lambda i: (i, 0)
    const = lambda i: (0, 0)
    y_specs = [pl.BlockSpec((tm * nsl, LANES), functools.partial(lambda i, k: (k * nrb + i, 0), k=k))
               for k in range(TOP_K)]
    return pl.pallas_call(
        _combine_ln_kernel,
        grid=(nrb,),
        in_specs=[pl.BlockSpec((tm, d), row)] + y_specs +
                 [pl.BlockSpec((tm, ROUTER_PAD), row),
                  pl.BlockSpec((1, d), const),
                  pl.BlockSpec((1, d), const)],
        out_specs=[pl.BlockSpec((tm, d), row), pl.BlockSpec((tm, d), row)],
        out_shape=[jax.ShapeDtypeStruct((t, d), F32), jax.ShapeDtypeStruct((t, d), BF16)],
        compiler_params=_params("arbitrary"),
        name="combine_ln",
    )(x1, yp, yp, yp, yp, gates, g.reshape(1, d), b.reshape(1, d))


def _ln_kernel(z_ref, g_ref, b_ref, o_ref):
    o_ref[...] = _ln_rows(z_ref[...], g_ref[...], b_ref[...])


def _ln(z, g, b):
    t, d = z.shape
    tm = min(LN_TM, t)
    return pl.pallas_call(
        _ln_kernel,
        grid=(t // tm,),
        in_specs=[pl.BlockSpec((tm, d), lambda i: (i, 0)),
                  pl.BlockSpec((1, d), lambda i: (0, 0)),
                  pl.BlockSpec((1, d), lambda i: (0, 0))],
        out_specs=pl.BlockSpec((tm, d), lambda i: (i, 0)),
        out_shape=jax.ShapeDtypeStruct((t, d), F32),
        compiler_params=_params("arbitrary"),
        name="layer_norm",
    )(z, g.reshape(1, d), b.reshape(1, d))


def _moe_kernel(be_ref, nu_ref, rows_ref, idx_hbm, x_hbm, wg_ref, wu_ref, bg_ref, bu_ref,
                wdl_ref, wdh_ref, bdl_ref, bdh_ref, y_hbm, idx_smem, xg, xb, act, ybuf, sem_idx, sem_g, sem_s, *, bm, ngu, ntok):
    del be_ref
    i = pl.program_id(0)
    j = pl.program_id(1)
    nu = nu_ref[0]
    active = i < nu
    d = xb.shape[1]

    def with_rows(item, fn):
        item_rows = rows_ref[item]
        for part in range(1, MOE_PARTS + 1):
            rows = part * (bm // MOE_PARTS)

            @pl.when(item_rows == rows)
            def _(rows=rows):
                fn(rows)

    half = d // 2
    nsl = half // LANES
    td = wdl_ref.shape[2]

    def idx_copy(item, slot):
        src = idx_hbm.at[pl.ds(pl.multiple_of(item * (2 * bm), 2 * bm), 2 * bm)]
        dst = idx_smem.at[pl.ds(pl.multiple_of(slot * (2 * bm), 2 * bm), 2 * bm)]
        return pltpu.make_async_copy(src, dst, sem_idx)

    def for_each_row(slot, table, rows, fn):
        base = slot * (2 * bm) + table * bm

        def body(r, c):
            fn(idx_smem[base + r], r)
            return c
        lax.fori_loop(0, rows, body, 0, unroll=8)

    def vmem_slab(buf, r):
        return buf.at[pl.ds(pl.multiple_of(r * SLAB_PITCH, 8), nsl), :]

    def hbm_slab(ref, row):
        return ref.at[pl.ds(pl.multiple_of(row, 8), nsl), :]

    def start_gather(slot, rows):
        def one(src, r):
            pltpu.make_async_copy(hbm_slab(x_hbm, src), vmem_slab(xg, r), sem_g).start()
        for_each_row(slot, 0, rows, one)

    def wait_gather(rows):
        pltpu.make_async_copy(x_hbm.at[pl.ds(0, rows * nsl), :], xg.at[pl.ds(0, rows * nsl), :], sem_g).wait()

    def start_scatter(slot, rows):
        def one(dst, r):
            pltpu.make_async_copy(vmem_slab(ybuf, r), hbm_slab(y_hbm, dst), sem_s).start()
        for_each_row(slot, 1, rows, one)

    def wait_scatter(rows):
        pltpu.make_async_copy(ybuf.at[pl.ds(0, rows * nsl), :], y_hbm.at[pl.ds(0, rows * nsl), :], sem_s).wait()

    def unpack(rows):
        wait_gather(rows)
        for s in range(nsl):
            w = xg[pl.ds(s, rows, stride=SLAB_PITCH), :]
            xb[:rows, s * LANES:(s + 1) * LANES] = _unpack_lo(w).astype(BF16)
            xb[:rows, half + s * LANES:half + (s + 1) * LANES] = _unpack_hi(w).astype(BF16)

    @pl.when(active & (j == 0))
    def _():
        @pl.when(i == 0)
        def _():
            idx_copy(0, 0).start()
            idx_copy(0, 0).wait()
            with_rows(0, lambda rows: start_gather(0, rows))
            ybuf[...] = jnp.zeros_like(ybuf)
            pad = pltpu.make_async_copy(ybuf.at[pl.ds(0, bm * nsl), :],
                                        y_hbm.at[pl.ds(TOP_K * ntok * nsl, bm * nsl), :], sem_s)
            pad.start()
            pad.wait()
        nxt = lax.rem(i + 1, 2)

        @pl.when(i + 1 < nu)
        def _():
            idx_copy(i + 1, nxt).start()

        with_rows(i, unpack)

        @pl.when(i + 1 < nu)
        def _():
            idx_copy(i + 1, nxt).wait()
            with_rows(i + 1, lambda rows: start_gather(nxt, rows))

    def gate_up(rows):
        x = xb[:rows, :]
        g = jnp.dot(x, wg_ref[0], preferred_element_type=F32) + bg_ref[0]
        u = jnp.dot(x, wu_ref[0], preferred_element_type=F32) + bu_ref[0]
        g = jnp.minimum(g, SWIGLU_LIMIT)
        u = jnp.clip(u, -SWIGLU_LIMIT, SWIGLU_LIMIT)
        act[j, :rows, :] = (g * _sigmoid(SWIGLU_ALPHA * g) * (u + 1.0)).astype(BF16)

    @pl.when(active & (j < ngu))
    def _():
        with_rows(i, gate_up)

    def down(rows):
        c = j - ngu
        a = jnp.concatenate([act[q, :rows, :] for q in range(ngu)], axis=1)
        for p in range(td // MOE_DC):
            cs = slice(p * MOE_DC, (p + 1) * MOE_DC)
            lo = jnp.dot(a, wdl_ref[0, :, cs], preferred_element_type=F32) + bdl_ref[0, :, cs]
            hi = jnp.dot(a, wdh_ref[0, :, cs], preferred_element_type=F32) + bdh_ref[0, :, cs]
            words = _pack_words(lo, hi)
            for q in range(MOE_DC // LANES):
                s = c * (td // LANES) + p * (MOE_DC // LANES) + q
                ybuf[pl.ds(s, rows, stride=SLAB_PITCH), :] = words[:, q * LANES:(q + 1) * LANES]

    @pl.when(active & (j == ngu))
    def _():
        @pl.when(i > 0)
        def _():
            with_rows(i - 1, wait_scatter)

    @pl.when(active & (j >= ngu))
    def _():
        with_rows(i, down)

    @pl.when(active & (j == ngu + half // td - 1))
    def _():
        def finish(rows):
            start_scatter(lax.rem(i, 2), rows)

            @pl.when(i == nu - 1)
            def _():
                wait_scatter(rows)
        with_rows(i, finish)


def _moe(xp, item_dst, block_expert, item_rows, n_used, wgu, bgu, wd, bd, t, d):
    bm, tf, td = MOE_BM, MOE_TF, MOE_TD
    half = d // 2
    nsl = half // LANES
    assert half % td == 0 and td % MOE_DC == 0 and EXPERT_FF % tf == 0
    assert nsl % 8 == 0 and SLAB_PITCH % 8 == 0 and SLAB_PITCH >= nsl
    n_items = item_dst.shape[0]
    ngu = EXPERT_FF // tf
    nd = half // td
    nj = ngu + nd
    idx3 = jnp.stack([(item_dst % t) * nsl, item_dst * nsl], axis=1).reshape(n_items * 2 * bm)
    bgu3 = bgu.reshape(N_EXPERTS, 1, 2 * EXPERT_FF)
    bd3 = bd.reshape(N_EXPERTS, 1, d)

    def step(i, j, nu):
        return jnp.where(i < nu[0], j, nj - 1)

    def gu(i, j, nu):
        return jnp.minimum(step(i, j, nu), ngu - 1)

    def dn(i, j, nu):
        return jnp.maximum(step(i, j, nu) - ngu, 0)

    def wd_block(i, j, be, nu, base):
        in_down = step(i, j, nu) >= ngu
        e = jnp.where(in_down, be[i], be[jnp.maximum(i - 1, 0)])
        return (e, 0, base + jnp.where(in_down, dn(i, j, nu), nd - 1))

    grid_spec = pltpu.PrefetchScalarGridSpec(
        num_scalar_prefetch=3,
        grid=(n_items, nj),
        in_specs=[
            pl.BlockSpec(memory_space=pl.ANY),
            pl.BlockSpec(memory_space=pl.ANY),
            pl.BlockSpec((1, d, tf), lambda i, j, be, nu, hf: (be[i], 0, gu(i, j, nu))),
            pl.BlockSpec((1, d, tf), lambda i, j, be, nu, hf: (be[i], 0, gu(i, j, nu) + ngu)),
            pl.BlockSpec((1, 1, tf), lambda i, j, be, nu, hf: (be[i], 0, gu(i, j, nu))),
            pl.BlockSpec((1, 1, tf), lambda i, j, be, nu, hf: (be[i], 0, gu(i, j, nu) + ngu)),
            pl.BlockSpec((1, EXPERT_FF, td), lambda i, j, be, nu, hf: wd_block(i, j, be, nu, 0)),
            pl.BlockSpec((1, EXPERT_FF, td), lambda i, j, be, nu, hf: wd_block(i, j, be, nu, nd)),
            pl.BlockSpec((1, 1, td), lambda i, j, be, nu, hf: (be[i], 0, dn(i, j, nu))),
            pl.BlockSpec((1, 1, td), lambda i, j, be, nu, hf: (be[i], 0, nd + dn(i, j, nu))),
        ],
        out_specs=pl.BlockSpec(memory_space=pl.ANY),
        scratch_shapes=[
            pltpu.SMEM((2 * 2 * bm,), jnp.int32),
            pltpu.VMEM((bm * SLAB_PITCH, LANES), jnp.uint32),
            pltpu.VMEM((bm, d), BF16),
            pltpu.VMEM((ngu, bm, tf), BF16),
            pltpu.VMEM((bm * SLAB_PITCH, LANES), jnp.uint32),
            pltpu.SemaphoreType.DMA(()),
            pltpu.SemaphoreType.DMA(()),
            pltpu.SemaphoreType.DMA(()),
        ],
    )
    return pl.pallas_call(
        functools.partial(_moe_kernel, bm=bm, ngu=ngu, ntok=t),
        grid_spec=grid_spec,
        out_shape=jax.ShapeDtypeStruct(((TOP_K * t + bm) * nsl, LANES), jnp.uint32),
        compiler_params=_params("arbitrary", "arbitrary"),
        name="moe_experts",
    )(block_expert, n_used, item_rows, idx3, xp, wgu, wgu, bgu3, bgu3, wd, wd, bd3, bd3)


def _routing_tables(top_idx, t):
    bm = MOE_BM
    pb = bm // MOE_PARTS
    tk = t * TOP_K
    n_items = tk // bm + N_EXPERTS
    n_rows = (tk // pb + N_EXPERTS) * pb + bm
    flat_e = top_idx.T.reshape(tk)
    experts = jnp.arange(N_EXPERTS, dtype=jnp.int32)
    order = jnp.argsort(flat_e, stable=True).astype(jnp.int32)
    counts = jnp.sum((flat_e[:, None] == experts[None, :]).astype(jnp.int32), axis=0)
    parts = (counts + pb - 1) // pb
    padded = parts * pb
    group_end = jnp.cumsum(counts)
    group_start = group_end - counts
    padded_end = jnp.cumsum(padded)
    padded_start = padded_end - padded
    items_per = (parts + MOE_PARTS - 1) // MOE_PARTS
    item_end = jnp.cumsum(items_per)
    item_first = item_end - items_per

    def bucket(ends, v):
        e = jnp.sum((ends[None, :] <= v[:, None]).astype(jnp.int32), axis=1)
        return jnp.minimum(e, N_EXPERTS - 1)

    def lookup(table, e):
        return jnp.sum(jnp.where(e[:, None] == experts[None, :], table[None, :], 0), axis=1)

    rows = jnp.arange(n_rows, dtype=jnp.int32)
    row_e = bucket(padded_end, rows)
    pos = rows - lookup(padded_start - group_start, row_e)
    valid = pos < lookup(group_end, row_e)
    row_dst = jnp.where(valid, order[jnp.clip(pos, 0, tk - 1)], tk + rows % bm)

    n_used = item_end[-1].astype(jnp.int32)
    items = jnp.minimum(jnp.arange(n_items, dtype=jnp.int32), n_used - 1)
    item_e = bucket(item_end, items)
    local = items - lookup(item_first, item_e)
    item_rows = pb * jnp.clip(lookup(parts, item_e) - MOE_PARTS * local, 1, MOE_PARTS)
    first_row = lookup(padded_start, item_e) + local * bm
    item_dst = row_dst[first_row[:, None] + jnp.arange(bm, dtype=jnp.int32)[None, :]]
    return item_dst, item_e, item_rows.astype(jnp.int32), n_used.reshape(1)


def kernel(x, p, w_in, attn_sinks, ret_norm_g, w_att_out, w_ret_out, w_out, ln1_g, ln1_b,
           w_router, b_router, w_gate_up, b_gate_up, w_down, b_down, ln2_g, ln2_b,
           w_ple, w_ple_gate, ln3_g, ln3_b):
    bsz, seq, d = x.shape
    t = bsz * seq
    assert seq % ATT_BLOCK == 0 and seq % RET_CHUNK == 0
    xf = x.reshape(t, d)
    for i in range(DEPTH):
        w_in_i = w_in[i].astype(F32)
        widths = (("aq", ATT_Q_WIDTH), ("akv", 2 * ATT_KV_WIDTH), ("rqk", 2 * RET_QK_WIDTH),
                  ("rv", RET_V_WIDTH), ("rg", RET_V_WIDTH), ("ga", d), ("gr", d))
        offs, off = {}, 0
        for name, width in widths:
            offs[name] = off
            off += width
        assert 2 * ATT_KV_WIDTH == MM_TN
        segs = {}
        segs["akv"], xb = _proj_cast(xf.astype(F32), w_in_i, offs["akv"], "proj_akv")
        for name, width in widths:
            if name != "akv":
                segs[name] = _proj(xb, w_in_i, offs[name], width, "proj_" + name)
        y_att = _attention(segs["aq"], segs["akv"], attn_sinks[i].astype(F32), seq // ATT_BLOCK)
        y_ret = _retention(segs["rqk"], segs["rv"], segs["rg"], ret_norm_g[i].astype(F32),
                           bsz, seq // RET_CHUNK)
        merged = _merge(y_att, y_ret, w_att_out[i].astype(F32), w_ret_out[i].astype(F32),
                        segs["ga"], segs["gr"])
        z1 = _resid_mm(merged, w_out[i].astype(F32), xf)
        x1, x1p, gates, top_idx = _ln_router(z1, ln1_g[i], ln1_b[i], w_router[i], b_router[i])
        item_dst, block_expert, item_rows, n_used = _routing_tables(top_idx[:, :TOP_K], t)
        yp = _moe(x1p, item_dst, block_expert, item_rows, n_used, w_gate_up[i].astype(BF16),
                  b_gate_up[i].astype(F32), w_down[i].astype(BF16), b_down[i].astype(F32), t, d)
        x2, x2b = _combine_ln(x1, yp, gates, ln2_g[i], ln2_b[i])
        z3 = _ple(x2b, w_ple_gate[i].astype(F32), p[i].reshape(t, -1), w_ple[i].astype(F32), x2)
        xf = _ln(z3, ln3_g[i], ln3_b[i])
    return xf.reshape(bsz, seq, d)
```

```python
import functools
import math

import jax
import jax.numpy as jnp
import numpy as np
from jax import lax
from jax.experimental import pallas as pl
from jax.experimental.pallas import tpu as pltpu

F32 = jnp.float32
BF16 = jnp.bfloat16

ATT_HEADS = 32
ATT_KV_HEADS = 4
ATT_GROUP = ATT_HEADS // ATT_KV_HEADS
ATT_HEAD_DIM = 64
ATT_BLOCK = 128
RET_HEADS = 8
RET_KEY_DIM = 128
RET_VAL_DIM = 256
RET_CHUNK = 128
N_EXPERTS = 32
TOP_K = 4
EXPERT_FF = 1536
SWIGLU_LIMIT = 7.0
SWIGLU_ALPHA = 1.702
LN_EPS = 1e-5
DEPTH = 1
DEEPNORM_ALPHA = float((2 * DEPTH) ** 0.25)

ATT_Q_WIDTH = ATT_HEADS * ATT_HEAD_DIM
ATT_KV_WIDTH = ATT_KV_HEADS * ATT_HEAD_DIM
RET_QK_WIDTH = RET_HEADS * RET_KEY_DIM
RET_V_WIDTH = RET_HEADS * RET_VAL_DIM

LANES = 128
VMEM_LIMIT_BYTES = 56 * 1024 * 1024
MM_TM = 1024
MM_TN = 512
LN_TM = 512
COMBINE_TM = 256
MOE_BM = 512
MOE_PARTS = 4
MOE_TF = 512
MOE_TD = 1024
MOE_DC = 512
ROUTER_PAD = LANES
NEG_BIG = -1e30
SLAB_PITCH = 24
HI_MASK = 0xFFFF0000


def _params(*sem):
    return pltpu.CompilerParams(dimension_semantics=sem, vmem_limit_bytes=VMEM_LIMIT_BYTES)


def _sigmoid(x):
    return 1.0 / (1.0 + jnp.exp(-x))


def _bf16_bits(x):
    return lax.bitcast_convert_type(x.astype(BF16).astype(F32), jnp.uint32)


def _pack_words(lo, hi):
    return (_bf16_bits(lo) >> 16) | _bf16_bits(hi)


def _unpack_lo(w):
    return lax.bitcast_convert_type(w << 16, F32)


def _unpack_hi(w):
    return lax.bitcast_convert_type(w & jnp.uint32(HI_MASK), F32)


def _cast_at_first_row_tile(w_ref, wb_ref):
    @pl.when(pl.program_id(1) == 0)
    def _():
        wb_ref[...] = w_ref[...].astype(BF16)


def _proj_kernel(x_ref, w_ref, o_ref, wb_ref):
    _cast_at_first_row_tile(w_ref, wb_ref)
    o_ref[...] = jnp.dot(x_ref[...], wb_ref[...], preferred_element_type=F32).astype(o_ref.dtype)


def _proj(xb, w, col_off, ncols, name):
    m, k = xb.shape
    tm, tn = min(MM_TM, m), MM_TN
    assert m % tm == 0 and ncols % tn == 0 and col_off % tn == 0
    off = col_off // tn
    return pl.pallas_call(
        _proj_kernel,
        grid=(ncols // tn, m // tm),
        in_specs=[pl.BlockSpec((tm, k), lambda j, i: (i, 0)),
                  pl.BlockSpec((k, tn), lambda j, i: (0, j + off))],
        out_specs=pl.BlockSpec((tm, tn), lambda j, i: (i, j)),
        out_shape=jax.ShapeDtypeStruct((m, ncols), BF16),
        scratch_shapes=[pltpu.VMEM((k, tn), BF16)],
        compiler_params=_params("arbitrary", "arbitrary"),
        name=name,
    )(xb, w)


def _proj_cast_kernel(x_ref, w_ref, o_ref, xb_ref, wb_ref):
    _cast_at_first_row_tile(w_ref, wb_ref)
    xb = x_ref[...].astype(BF16)
    xb_ref[...] = xb
    o_ref[...] = jnp.dot(xb, wb_ref[...], preferred_element_type=F32).astype(o_ref.dtype)


def _proj_cast(x, w, col_off, name):
    m, k = x.shape
    tm, tn = min(MM_TM // 2, m), MM_TN
    assert m % tm == 0 and col_off % tn == 0
    off = col_off // tn
    return pl.pallas_call(
        _proj_cast_kernel,
        grid=(1, m // tm),
        in_specs=[pl.BlockSpec((tm, k), lambda j, i: (i, 0)),
                  pl.BlockSpec((k, tn), lambda j, i: (0, off))],
        out_specs=[pl.BlockSpec((tm, tn), lambda j, i: (i, 0)),
                   pl.BlockSpec((tm, k), lambda j, i: (i, 0))],
        out_shape=[jax.ShapeDtypeStruct((m, tn), BF16), jax.ShapeDtypeStruct((m, k), BF16)],
        scratch_shapes=[pltpu.VMEM((k, tn), BF16)],
        compiler_params=_params("arbitrary", "arbitrary"),
        name=name,
    )(x, w)


def _merge_kernel(ya_ref, yr_ref, wa_ref, wr_ref, ga_ref, gr_ref, o_ref, wab_ref, wrb_ref):
    _cast_at_first_row_tile(wa_ref, wab_ref)
    _cast_at_first_row_tile(wr_ref, wrb_ref)
    a = jnp.dot(ya_ref[...], wab_ref[...], preferred_element_type=F32)
    r = jnp.dot(yr_ref[...], wrb_ref[...], preferred_element_type=F32)
    ga = _sigmoid(ga_ref[...].astype(F32))
    gr = _sigmoid(gr_ref[...].astype(F32))
    o_ref[...] = (ga * a + gr * r).astype(o_ref.dtype)


def _merge(ya, yr, wa, wr, ga, gr):
    m, ka = ya.shape
    kr = yr.shape[1]
    n = wa.shape[1]
    tm, tn = min(MM_TM, m), MM_TN
    return pl.pallas_call(
        _merge_kernel,
        grid=(n // tn, m // tm),
        in_specs=[pl.BlockSpec((tm, ka), lambda j, i: (i, 0)),
                  pl.BlockSpec((tm, kr), lambda j, i: (i, 0)),
                  pl.BlockSpec((ka, tn), lambda j, i: (0, j)),
                  pl.BlockSpec((kr, tn), lambda j, i: (0, j)),
                  pl.BlockSpec((tm, tn), lambda j, i: (i, j)),
                  pl.BlockSpec((tm, tn), lambda j, i: (i, j))],
        out_specs=pl.BlockSpec((tm, tn), lambda j, i: (i, j)),
        out_shape=jax.ShapeDtypeStruct((m, n), BF16),
        scratch_shapes=[pltpu.VMEM((ka, tn), BF16), pltpu.VMEM((kr, tn), BF16)],
        compiler_params=_params("arbitrary", "arbitrary"),
        name="merge",
    )(ya, yr, wa, wr, ga, gr)


def _resid_mm_kernel(a_ref, w_ref, x_ref, o_ref, wb_ref):
    _cast_at_first_row_tile(w_ref, wb_ref)
    acc = jnp.dot(a_ref[...], wb_ref[...], preferred_element_type=F32)
    o_ref[...] = DEEPNORM_ALPHA * x_ref[...] + acc


def _resid_mm(a, w, x):
    m, k = a.shape
    n = w.shape[1]
    tm, tn = min(MM_TM, m), MM_TN
    return pl.pallas_call(
        _resid_mm_kernel,
        grid=(n // tn, m // tm),
        in_specs=[pl.BlockSpec((tm, k), lambda j, i: (i, 0)),
                  pl.BlockSpec((k, tn), lambda j, i: (0, j)),
                  pl.BlockSpec((tm, tn), lambda j, i: (i, j))],
        out_specs=pl.BlockSpec((tm, tn), lambda j, i: (i, j)),
        out_shape=jax.ShapeDtypeStruct((m, n), F32),
        scratch_shapes=[pltpu.VMEM((k, tn), BF16)],
        compiler_params=_params("arbitrary", "arbitrary"),
        name="resid_mm",
    )(a, w, x)


def _ple_kernel(xb_ref, wg_ref, p_ref, wp_ref, x_ref, o_ref, wgb_ref, wpb_ref):
    _cast_at_first_row_tile(wg_ref, wgb_ref)
    _cast_at_first_row_tile(wp_ref, wpb_ref)
    gate = jnp.dot(xb_ref[...], wgb_ref[...], preferred_element_type=F32)
    emb = jnp.dot(p_ref[...].astype(BF16), wpb_ref[...], preferred_element_type=F32)
    o_ref[...] = DEEPNORM_ALPHA * x_ref[...] + emb * _sigmoid(gate)


def _ple(xb, wg, p, wp, x):
    m, k = xb.shape
    n = wg.shape[1]
    kp = p.shape[1]
    tm, tn = min(MM_TM, m), MM_TN
    return pl.pallas_call(
        _ple_kernel,
        grid=(n // tn, m // tm),
        in_specs=[pl.BlockSpec((tm, k), lambda j, i: (i, 0)),
                  pl.BlockSpec((k, tn), lambda j, i: (0, j)),
                  pl.BlockSpec((tm, kp), lambda j, i: (i, 0)),
                  pl.BlockSpec((kp, tn), lambda j, i: (0, j)),
                  pl.BlockSpec((tm, tn), lambda j, i: (i, j))],
        out_specs=pl.BlockSpec((tm, tn), lambda j, i: (i, j)),
        out_shape=jax.ShapeDtypeStruct((m, n), F32),
        scratch_shapes=[pltpu.VMEM((k, tn), BF16), pltpu.VMEM((kp, tn), BF16)],
        compiler_params=_params("arbitrary", "arbitrary"),
        name="ple",
    )(xb, wg, p, wp, x)


def _alibi_slope(h):
    return float(2.0 ** (-8.0 * (h + 1) / ATT_HEADS))


def _attn_kernel(sink_ref, q_ref, kvc_ref, kvp_ref, o_ref, *, nblk):
    blk = ATT_BLOCK
    n = lax.rem(pl.program_id(0), nblk)
    has_prev = n > 0
    row = lax.broadcasted_iota(jnp.int32, (blk, 2 * blk), 0)
    col = lax.broadcasted_iota(jnp.int32, (blk, 2 * blk), 1)
    dist = blk + row - col
    valid = (dist >= 0) & (dist < blk) & ((col >= blk) | has_prev)
    distf = dist.astype(F32)
    lane = lax.broadcasted_iota(jnp.int32, (2 * blk, LANES), 1)
    lo = lane < ATT_HEAD_DIM
    scale = ATT_HEAD_DIM ** -0.5

    def halves(t, half):
        swapped = jnp.concatenate([t[:, ATT_HEAD_DIM:], t[:, :ATT_HEAD_DIM]], axis=1)
        zero = jnp.zeros_like(t)
        if half == 0:
            return jnp.where(lo, t, zero), jnp.where(lo, zero, swapped)
        return jnp.where(lo, swapped, zero), jnp.where(lo, zero, t)

    for kh in range(ATT_KV_HEADS):
        tile, half = kh // 2, kh % 2
        ks = slice(tile * LANES, (tile + 1) * LANES)
        vs = slice(ATT_KV_WIDTH + tile * LANES, ATT_KV_WIDTH + (tile + 1) * LANES)
        kcat = jnp.concatenate([kvp_ref[:, ks], kvc_ref[:, ks]], axis=0)
        vcat = jnp.concatenate([kvp_ref[:, vs], kvc_ref[:, vs]], axis=0)
        k_lo, k_hi = halves(kcat, half)
        v_lo, v_hi = halves(vcat, half)
        for jp in range(ATT_GROUP // 2):
            h0 = kh * ATT_GROUP + 2 * jp
            qs = slice(h0 * ATT_HEAD_DIM, h0 * ATT_HEAD_DIM + LANES)
            qp = q_ref[:, qs]
            acc = jnp.zeros((blk, LANES), F32)
            for par, (kx, vx) in enumerate(((k_lo, v_lo), (k_hi, v_hi))):
                h = h0 + par
                s = lax.dot_general(qp, kx, (((1,), (1,)), ((), ())), preferred_element_type=F32)
                s = s * scale - _alibi_slope(h) * distf
                s = jnp.where(valid, s, -jnp.inf)
                sink = sink_ref[h]
                m = jnp.maximum(jnp.max(s, axis=-1, keepdims=True), sink)
                e = jnp.exp(s - m)
                denom = jnp.sum(e, axis=-1, keepdims=True) + jnp.exp(sink - m)
                pv = jnp.dot(e.astype(BF16), vx, preferred_element_type=F32)
                acc = acc + pv * (1.0 / denom)
            o_ref[:, qs] = acc.astype(o_ref.dtype)


def _attention(aq, akv, sinks, nblk):
    t = aq.shape[0]
    blk = ATT_BLOCK
    return pl.pallas_call(
        functools.partial(_attn_kernel, nblk=nblk),
        grid=(t // blk,),
        in_specs=[pl.BlockSpec(memory_space=pltpu.SMEM),
                  pl.BlockSpec((blk, ATT_Q_WIDTH), lambda i: (i, 0)),
                  pl.BlockSpec((blk, 2 * ATT_KV_WIDTH), lambda i: (i, 0)),
                  pl.BlockSpec((blk, 2 * ATT_KV_WIDTH), lambda i: (jnp.maximum(i - 1, 0), 0))],
        out_specs=pl.BlockSpec((blk, ATT_Q_WIDTH), lambda i: (i, 0)),
        out_shape=jax.ShapeDtypeStruct((t, ATT_Q_WIDTH), BF16),
        compiler_params=_params("arbitrary"),
        name="swa_attention",
    )(sinks, aq, akv, akv)


def _ret_constants():
    c = RET_CHUNK
    gam = 1.0 - 2.0 ** (-5.0 - np.arange(RET_HEADS, dtype=np.float64))
    lg = np.log(gam)
    pos = np.arange(c, dtype=np.float64)
    diff = pos[:, None] - pos[None, :]
    kscale = RET_KEY_DIM ** -0.5
    inner = np.where(diff >= 0, np.exp(np.maximum(diff, 0.0) * lg[:, None, None]), 0.0) * kscale
    qdec = np.exp((pos + 1.0) * lg[:, None])
    kdec = np.exp((c - 1.0 - pos) * lg[:, None]) * kscale
    cdec = np.exp(c * lg)
    qdec_b = np.broadcast_to(qdec[:, :, None], (RET_HEADS, c, RET_VAL_DIM))
    kdec_b = np.broadcast_to(kdec[:, :, None], (RET_HEADS, c, RET_KEY_DIM))
    return (jnp.asarray(inner, F32), jnp.asarray(qdec_b, F32), jnp.asarray(kdec_b, F32),
            [float(v) for v in cdec])


def _ret_kernel(qk_ref, v_ref, g_ref, ng_ref, inner_ref, qdec_ref, kdec_ref, o_ref, state_ref, *, cdec):
    @pl.when(pl.program_id(1) == 0)
    def _():
        state_ref[...] = jnp.zeros_like(state_ref)

    dk, dv = RET_KEY_DIM, RET_VAL_DIM
    for h in range(RET_HEADS):
        q = qk_ref[:, h * dk:(h + 1) * dk]
        k = qk_ref[:, RET_QK_WIDTH + h * dk:RET_QK_WIDTH + (h + 1) * dk]
        v = v_ref[:, h * dv:(h + 1) * dv]
        st = state_ref[h]
        qk = lax.dot_general(q, k, (((1,), (1,)), ((), ())), preferred_element_type=F32)
        a = (qk * inner_ref[h]).astype(BF16)
        inner = jnp.dot(a, v, preferred_element_type=F32)
        cross = jnp.dot(q, st.astype(BF16), preferred_element_type=F32) * qdec_ref[h]
        kd = (k.astype(F32) * kdec_ref[h]).astype(BF16)
        upd = lax.dot_general(kd, v, (((0,), (0,)), ((), ())), preferred_element_type=F32)
        state_ref[h] = st * cdec[h] + upd
        o = inner + cross
        mu = jnp.mean(o, axis=-1, keepdims=True)
        d = o - mu
        var = jnp.mean(d * d, axis=-1, keepdims=True)
        y = d * lax.rsqrt(var + LN_EPS) * ng_ref[:, h * dv:(h + 1) * dv]
        g = g_ref[:, h * dv:(h + 1) * dv].astype(F32)
        o_ref[:, h * dv:(h + 1) * dv] = (g * _sigmoid(g) * y).astype(o_ref.dtype)


def _retention(rqk, rv, rg, norm_g, batch, nchunk):
    t = rqk.shape[0]
    c = RET_CHUNK
    inner, qdec, kdec, cdec = _ret_constants()
    row = lambda b, n: (b * nchunk + n, 0)
    const3 = lambda b, n: (0, 0, 0)
    return pl.pallas_call(
        functools.partial(_ret_kernel, cdec=cdec),
        grid=(batch, nchunk),
        in_specs=[pl.BlockSpec((c, 2 * RET_QK_WIDTH), row),
                  pl.BlockSpec((c, RET_V_WIDTH), row),
                  pl.BlockSpec((c, RET_V_WIDTH), row),
                  pl.BlockSpec((1, RET_V_WIDTH), lambda b, n: (0, 0)),
                  pl.BlockSpec((RET_HEADS, c, c), const3),
                  pl.BlockSpec((RET_HEADS, c, RET_VAL_DIM), const3),
                  pl.BlockSpec((RET_HEADS, c, RET_KEY_DIM), const3)],
        out_specs=pl.BlockSpec((c, RET_V_WIDTH), row),
        out_shape=jax.ShapeDtypeStruct((t, RET_V_WIDTH), BF16),
        scratch_shapes=[pltpu.VMEM((RET_HEADS, RET_KEY_DIM, RET_VAL_DIM), F32)],
        compiler_params=_params("arbitrary", "arbitrary"),
        name="retention",
    )(rqk, rv, rg, norm_g.reshape(1, RET_V_WIDTH), inner, qdec, kdec)


def _ln_rows(z, g, b):
    mu = jnp.mean(z, axis=-1, keepdims=True)
    d = z - mu
    var = jnp.mean(d * d, axis=-1, keepdims=True)
    return d * lax.rsqrt(var + LN_EPS) * g + b


def _ln_router_kernel(z_ref, g_ref, b_ref, wr_ref, br_ref, x_ref, xp_ref, gate_ref, idx_ref):
    x = _ln_rows(z_ref[...], g_ref[...], b_ref[...])
    x_ref[...] = x
    tm, d = x.shape
    half = d // 2
    nsl = half // LANES
    for s in range(nsl):
        cs = slice(s * LANES, (s + 1) * LANES)
        hs = slice(half + s * LANES, half + (s + 1) * LANES)
        xp_ref[pl.ds(s, tm, stride=nsl), :] = _pack_words(x[:, cs], x[:, hs])
    x_hi = x.astype(BF16)
    x_lo = (x - x_hi.astype(F32)).astype(BF16)
    p_hi = jnp.dot(x_hi, wr_ref[...], preferred_element_type=F32)
    p_lo = jnp.dot(x_lo, wr_ref[...], preferred_element_type=F32)
    logits = p_hi + pltpu.roll(p_hi, ROUTER_PAD - N_EXPERTS, axis=1) + p_lo + br_ref[...]
    lane = lax.broadcasted_iota(jnp.int32, logits.shape, 1)
    lane_f = lane.astype(F32)
    vals, idxs = [], []
    cur = logits
    for _ in range(TOP_K):
        m = jnp.max(cur, axis=-1, keepdims=True)
        idx_f = jnp.min(jnp.where(cur == m, lane_f, float(ROUTER_PAD)), axis=-1, keepdims=True)
        idx = idx_f.astype(jnp.int32)
        vals.append(m)
        idxs.append(idx)
        cur = jnp.where(lane == idx, -jnp.inf, cur)
    es = [jnp.exp(v - vals[0]) for v in vals]
    inv = 1.0 / (es[0] + es[1] + es[2] + es[3])
    gates = jnp.zeros(logits.shape, F32)
    ids = jnp.zeros(logits.shape, jnp.int32)
    for k in range(TOP_K):
        gates = jnp.where(lane == k, es[k] * inv, gates)
        ids = jnp.where(lane == k, idxs[k], ids)
    gate_ref[...] = gates
    idx_ref[...] = ids


def _ln_router(z, g, b, w_router, b_router):
    t, d = z.shape
    tm = min(LN_TM, t)
    w32 = w_router.astype(F32)
    w_hi = w32.astype(BF16)
    w_lo = (w32 - w_hi.astype(F32)).astype(BF16)
    wr = jnp.zeros((d, ROUTER_PAD), BF16).at[:, :N_EXPERTS].set(w_hi).at[:, N_EXPERTS:2 * N_EXPERTS].set(w_lo)
    br = jnp.full((1, ROUTER_PAD), NEG_BIG, F32).at[0, :N_EXPERTS].set(b_router.astype(F32))
    row = lambda i: (i, 0)
    const = lambda i: (0, 0)
    nsl = d // 2 // LANES
    return pl.pallas_call(
        _ln_router_kernel,
        grid=(t // tm,),
        in_specs=[pl.BlockSpec((tm, d), row),
                  pl.BlockSpec((1, d), const),
                  pl.BlockSpec((1, d), const),
                  pl.BlockSpec((d, ROUTER_PAD), const),
                  pl.BlockSpec((1, ROUTER_PAD), const)],
        out_specs=[pl.BlockSpec((tm, d), row),
                   pl.BlockSpec((tm * nsl, LANES), row),
                   pl.BlockSpec((tm, ROUTER_PAD), row),
                   pl.BlockSpec((tm, ROUTER_PAD), row)],
        out_shape=[jax.ShapeDtypeStruct((t, d), F32),
                   jax.ShapeDtypeStruct((t * nsl, LANES), jnp.uint32),
                   jax.ShapeDtypeStruct((t, ROUTER_PAD), F32),
                   jax.ShapeDtypeStruct((t, ROUTER_PAD), jnp.int32)],
        compiler_params=_params("arbitrary"),
        name="ln_router",
    )(z, g.reshape(1, d), b.reshape(1, d), wr, br)


def _combine_ln_kernel(x_ref, y0_ref, y1_ref, y2_ref, y3_ref, gate_ref, g_ref, b_ref, o_ref, ob_ref):
    tm, d = x_ref.shape
    nsl = d // 2 // LANES
    gt = gate_ref[...]
    gk = [jnp.broadcast_to(gt[:, k:k + 1], (tm, LANES)) for k in range(TOP_K)]
    los, his = [], []
    for s in range(nsl):
        lo = hi = None
        for k, y_ref in enumerate((y0_ref, y1_ref, y2_ref, y3_ref)):
            w = y_ref[pl.ds(s, tm, stride=nsl), :]
            tl, th = gk[k] * _unpack_lo(w), gk[k] * _unpack_hi(w)
            lo = tl if lo is None else lo + tl
            hi = th if hi is None else hi + th
        los.append(lo)
        his.append(hi)
    z = DEEPNORM_ALPHA * x_ref[...] + jnp.concatenate(los + his, axis=1)
    out = _ln_rows(z, g_ref[...], b_ref[...])
    o_ref[...] = out
    ob_ref[...] = out.astype(ob_ref.dtype)


def _combine_ln(x1, yp, gates, g, b):
    t, d = x1.shape
    tm = min(COMBINE_TM, t)
    nrb = t // tm
    nsl = d // 2 // LANES
    row = lambda i: (i, 0)
    const = lambda i: (0, 0)
    y_specs = [pl.BlockSpec((tm * nsl, LANES), functools.partial(lambda i, k: (k * nrb + i, 0), k=k))
               for k in range(TOP_K)]
    return pl.pallas_call(
        _combine_ln_kernel,
        grid=(nrb,),
        in_specs=[pl.BlockSpec((tm, d), row)] + y_specs +
                 [pl.BlockSpec((tm, ROUTER_PAD), row),
                  pl.BlockSpec((1, d), const),
                  pl.BlockSpec((1, d), const)],
        out_specs=[pl.BlockSpec((tm, d), row), pl.BlockSpec((tm, d), row)],
        out_shape=[jax.ShapeDtypeStruct((t, d), F32), jax.ShapeDtypeStruct((t, d), BF16)],
        compiler_params=_params("arbitrary"),
        name="combine_ln",
    )(x1, yp, yp, yp, yp, gates, g.reshape(1, d), b.reshape(1, d))


def _ln_kernel(z_ref, g_ref, b_ref, o_ref):
    o_ref[...] = _ln_rows(z_ref[...], g_ref[...], b_ref[...])


def _ln(z, g, b):
    t, d = z.shape
    tm = min(LN_TM, t)
    return pl.pallas_call(
        _ln_kernel,
        grid=(t // tm,),
        in_specs=[pl.BlockSpec((tm, d), lambda i: (i, 0)),
                  pl.BlockSpec((1, d), lambda i: (0, 0)),
                  pl.BlockSpec((1, d), lambda i: (0, 0))],
        out_specs=pl.BlockSpec((tm, d), lambda i: (i, 0)),
        out_shape=jax.ShapeDtypeStruct((t, d), F32),
        compiler_params=_params("arbitrary"),
        name="layer_norm",
    )(z, g.reshape(1, d), b.reshape(1, d))


def _moe_kernel(be_ref, nu_ref, rows_ref, idx_hbm, x_hbm, wg_ref, wu_ref, bg_ref, bu_ref,
                wdl_ref, wdh_ref, bdl_ref, bdh_ref, y_hbm, idx_smem, xg, xb, act, ybuf, sem_idx, sem_g, sem_s, *, bm, ngu, ntok):
    del be_ref
    i = pl.program_id(0)
    j = pl.program_id(1)
    nu = nu_ref[0]
    active = i < nu
    d = xb.shape[1]

    def with_rows(item, fn):
        item_rows = rows_ref[item]
        for part in range(1, MOE_PARTS + 1):
            rows = part * (bm // MOE_PARTS)

            @pl.when(item_rows == rows)
            def _(rows=rows):
                fn(rows)

    half = d // 2
    nsl = half // LANES
    td = wdl_ref.shape[2]

    def idx_copy(item, slot):
        src = idx_hbm.at[pl.ds(pl.multiple_of(item * (2 * bm), 2 * bm), 2 * bm)]
        dst = idx_smem.at[pl.ds(pl.multiple_of(slot * (2 * bm), 2 * bm), 2 * bm)]
        return pltpu.make_async_copy(src, dst, sem_idx)

    def for_each_row(slot, table, rows, fn):
        base = slot * (2 * bm) + table * bm

        def body(r, c):
            fn(idx_smem[base + r], r)
            return c
        lax.fori_loop(0, rows, body, 0, unroll=8)

    def vmem_slab(buf, r):
        return buf.at[pl.ds(pl.multiple_of(r * SLAB_PITCH, 8), nsl), :]

    def hbm_slab(ref, row):
        return ref.at[pl.ds(pl.multiple_of(row, 8), nsl), :]

    def start_gather(slot, rows):
        def one(src, r):
            pltpu.make_async_copy(hbm_slab(x_hbm, src), vmem_slab(xg, r), sem_g).start()
        for_each_row(slot, 0, rows, one)

    def wait_gather(rows):
        pltpu.make_async_copy(x_hbm.at[pl.ds(0, rows * nsl), :], xg.at[pl.ds(0, rows * nsl), :], sem_g).wait()

    def start_scatter(slot, rows):
        def one(dst, r):
            pltpu.make_async_copy(vmem_slab(ybuf, r), hbm_slab(y_hbm, dst), sem_s).start()
        for_each_row(slot, 1, rows, one)

    def wait_scatter(rows):
        pltpu.make_async_copy(ybuf.at[pl.ds(0, rows * nsl), :], y_hbm.at[pl.ds(0, rows * nsl), :], sem_s).wait()

    def unpack(rows):
        wait_gather(rows)
        for s in range(nsl):
            w = xg[pl.ds(s, rows, stride=SLAB_PITCH), :]
            xb[:rows, s * LANES:(s + 1) * LANES] = _unpack_lo(w).astype(BF16)
            xb[:rows, half + s * LANES:half + (s + 1) * LANES] = _unpack_hi(w).astype(BF16)

    @pl.when(active & (j == 0))
    def _():
        @pl.when(i == 0)
        def _():
            idx_copy(0, 0).start()
            idx_copy(0, 0).wait()
            with_rows(0, lambda rows: start_gather(0, rows))
            ybuf[...] = jnp.zeros_like(ybuf)
            pad = pltpu.make_async_copy(ybuf.at[pl.ds(0, bm * nsl), :],
                                        y_hbm.at[pl.ds(TOP_K * ntok * nsl, bm * nsl), :], sem_s)
            pad.start()
            pad.wait()
        nxt = lax.rem(i + 1, 2)

        @pl.when(i + 1 < nu)
        def _():
            idx_copy(i + 1, nxt).start()

        with_rows(i, unpack)

        @pl.when(i + 1 < nu)
        def _():
            idx_copy(i + 1, nxt).wait()
            with_rows(i + 1, lambda rows: start_gather(nxt, rows))

    def gate_up(rows):
        x = xb[:rows, :]
        g = jnp.dot(x, wg_ref[0], preferred_element_type=F32) + bg_ref[0]
        u = jnp.dot(x, wu_ref[0], preferred_element_type=F32) + bu_ref[0]
        g = jnp.minimum(g, SWIGLU_LIMIT)
        u = jnp.clip(u, -SWIGLU_LIMIT, SWIGLU_LIMIT)
        act[j, :rows, :] = (g * _sigmoid(SWIGLU_ALPHA * g) * (u + 1.0)).astype(BF16)

    @pl.when(active & (j < ngu))
    def _():
        with_rows(i, gate_up)

    def down(rows):
        c = j - ngu
        a = jnp.concatenate([act[q, :rows, :] for q in range(ngu)], axis=1)
        for p in range(td // MOE_DC):
            cs = slice(p * MOE_DC, (p + 1) * MOE_DC)
            lo = jnp.dot(a, wdl_ref[0, :, cs], preferred_element_type=F32) + bdl_ref[0, :, cs]
            hi = jnp.dot(a, wdh_ref[0, :, cs], preferred_element_type=F32) + bdh_ref[0, :, cs]
            words = _pack_words(lo, hi)
            for q in range(MOE_DC // LANES):
                s = c * (td // LANES) + p * (MOE_DC // LANES) + q
                ybuf[pl.ds(s, rows, stride=SLAB_PITCH), :] = words[:, q * LANES:(q + 1) * LANES]

    @pl.when(active & (j == ngu))
    def _():
        @pl.when(i > 0)
        def _():
            with_rows(i - 1, wait_scatter)

    @pl.when(active & (j >= ngu))
    def _():
        with_rows(i, down)

    @pl.when(active & (j == ngu + half // td - 1))
    def _():
        def finish(rows):
            start_scatter(lax.rem(i, 2), rows)

            @pl.when(i == nu - 1)
            def _():
                wait_scatter(rows)
        with_rows(i, finish)


def _moe(xp, item_dst, block_expert, item_rows, n_used, wgu, bgu, wd, bd, t, d):
    bm, tf, td = MOE_BM, MOE_TF, MOE_TD
    half = d // 2
    nsl = half // LANES
    assert half % td == 0 and td % MOE_DC == 0 and EXPERT_FF % tf == 0
    assert nsl % 8 == 0 and SLAB_PITCH % 8 == 0 and SLAB_PITCH >= nsl
    n_items = item_dst.shape[0]
    ngu = EXPERT_FF // tf
    nd = half // td
    nj = ngu + nd
    idx3 = jnp.stack([(item_dst % t) * nsl, item_dst * nsl], axis=1).reshape(n_items * 2 * bm)
    bgu3 = bgu.reshape(N_EXPERTS, 1, 2 * EXPERT_FF)
    bd3 = bd.reshape(N_EXPERTS, 1, d)

    def step(i, j, nu):
        return jnp.where(i < nu[0], j, nj - 1)

    def gu(i, j, nu):
        return jnp.minimum(step(i, j, nu), ngu - 1)

    def dn(i, j, nu):
        return jnp.maximum(step(i, j, nu) - ngu, 0)

    def wd_block(i, j, be, nu, base):
        in_down = step(i, j, nu) >= ngu
        e = jnp.where(in_down, be[i], be[jnp.maximum(i - 1, 0)])
        return (e, 0, base + jnp.where(in_down, dn(i, j, nu), nd - 1))

    grid_spec = pltpu.PrefetchScalarGridSpec(
        num_scalar_prefetch=3,
        grid=(n_items, nj),
        in_specs=[
            pl.BlockSpec(memory_space=pl.ANY),
            pl.BlockSpec(memory_space=pl.ANY),
            pl.BlockSpec((1, d, tf), lambda i, j, be, nu, hf: (be[i], 0, gu(i, j, nu))),
            pl.BlockSpec((1, d, tf), lambda i, j, be, nu, hf: (be[i], 0, gu(i, j, nu) + ngu)),
            pl.BlockSpec((1, 1, tf), lambda i, j, be, nu, hf: (be[i], 0, gu(i, j, nu))),
            pl.BlockSpec((1, 1, tf), lambda i, j, be, nu, hf: (be[i], 0, gu(i, j, nu) + ngu)),
            pl.BlockSpec((1, EXPERT_FF, td), lambda i, j, be, nu, hf: wd_block(i, j, be, nu, 0)),
            pl.BlockSpec((1, EXPERT_FF, td), lambda i, j, be, nu, hf: wd_block(i, j, be, nu, nd)),
            pl.BlockSpec((1, 1, td), lambda i, j, be, nu, hf: (be[i], 0, dn(i, j, nu))),
            pl.BlockSpec((1, 1, td), lambda i, j, be, nu, hf: (be[i], 0, nd + dn(i, j, nu))),
        ],
        out_specs=pl.BlockSpec(memory_space=pl.ANY),
        scratch_shapes=[
            pltpu.SMEM((2 * 2 * bm,), jnp.int32),
            pltpu.VMEM((bm * SLAB_PITCH, LANES), jnp.uint32),
            pltpu.VMEM((bm, d), BF16),
            pltpu.VMEM((ngu, bm, tf), BF16),
            pltpu.VMEM((bm * SLAB_PITCH, LANES), jnp.uint32),
            pltpu.SemaphoreType.DMA(()),
            pltpu.SemaphoreType.DMA(()),
            pltpu.SemaphoreType.DMA(()),
        ],
    )
    return pl.pallas_call(
        functools.partial(_moe_kernel, bm=bm, ngu=ngu, ntok=t),
        grid_spec=grid_spec,
        out_shape=jax.ShapeDtypeStruct(((TOP_K * t + bm) * nsl, LANES), jnp.uint32),
        compiler_params=_params("arbitrary", "arbitrary"),
        name="moe_experts",
    )(block_expert, n_used, item_rows, idx3, xp, wgu, wgu, bgu3, bgu3, wd, wd, bd3, bd3)


def _routing_tables(top_idx, t):
    bm = MOE_BM
    pb = bm // MOE_PARTS
    tk = t * TOP_K
    n_items = tk // bm + N_EXPERTS
    n_rows = (tk // pb + N_EXPERTS) * pb + bm
    flat_e = top_idx.T.reshape(tk)
    experts = jnp.arange(N_EXPERTS, dtype=jnp.int32)
    order = jnp.argsort(flat_e, stable=True).astype(jnp.int32)
    counts = jnp.sum((flat_e[:, None] == experts[None, :]).astype(jnp.int32), axis=0)
    parts = (counts + pb - 1) // pb
    padded = parts * pb
    group_end = jnp.cumsum(counts)
    group_start = group_end - counts
    padded_end = jnp.cumsum(padded)
    padded_start = padded_end - padded
    items_per = (parts + MOE_PARTS - 1) // MOE_PARTS
    item_end = jnp.cumsum(items_per)
    item_first = item_end - items_per

    def bucket(ends, v):
        e = jnp.sum((ends[None, :] <= v[:, None]).astype(jnp.int32), axis=1)
        return jnp.minimum(e, N_EXPERTS - 1)

    def lookup(table, e):
        return jnp.sum(jnp.where(e[:, None] == experts[None, :], table[None, :], 0), axis=1)

    rows = jnp.arange(n_rows, dtype=jnp.int32)
    row_e = bucket(padded_end, rows)
    pos = rows - lookup(padded_start - group_start, row_e)
    valid = pos < lookup(group_end, row_e)
    row_dst = jnp.where(valid, order[jnp.clip(pos, 0, tk - 1)], tk + rows % bm)

    n_used = item_end[-1].astype(jnp.int32)
    items = jnp.minimum(jnp.arange(n_items, dtype=jnp.int32), n_used - 1)
    item_e = bucket(item_end, items)
    local = items - lookup(item_first, item_e)
    item_rows = pb * jnp.clip(lookup(parts, item_e) - MOE_PARTS * local, 1, MOE_PARTS)
    first_row = lookup(padded_start, item_e) + local * bm
    item_dst = row_dst[first_row[:, None] + jnp.arange(bm, dtype=jnp.int32)[None, :]]
    return item_dst, item_e, item_rows.astype(jnp.int32), n_used.reshape(1)


def kernel(x, p, w_in, attn_sinks, ret_norm_g, w_att_out, w_ret_out, w_out, ln1_g, ln1_b,
           w_router, b_router, w_gate_up, b_gate_up, w_down, b_down, ln2_g, ln2_b,
           w_ple, w_ple_gate, ln3_g, ln3_b):
    bsz, seq, d = x.shape
    t = bsz * seq
    assert seq % ATT_BLOCK == 0 and seq % RET_CHUNK == 0
    xf = x.reshape(t, d)
    for i in range(DEPTH):
        w_in_i = w_in[i].astype(F32)
        widths = (("aq", ATT_Q_WIDTH), ("akv", 2 * ATT_KV_WIDTH), ("rqk", 2 * RET_QK_WIDTH),
                  ("rv", RET_V_WIDTH), ("rg", RET_V_WIDTH), ("ga", d), ("gr", d))
        offs, off = {}, 0
        for name, width in widths:
            offs[name] = off
            off += width
        assert 2 * ATT_KV_WIDTH == MM_TN
        segs = {}
        segs["akv"], xb = _proj_cast(xf.astype(F32), w_in_i, offs["akv"], "proj_akv")
        for name, width in widths:
            if name != "akv":
                segs[name] = _proj(xb, w_in_i, offs[name], width, "proj_" + name)
        y_att = _attention(segs["aq"], segs["akv"], attn_sinks[i].astype(F32), seq // ATT_BLOCK)
        y_ret = _retention(segs["rqk"], segs["rv"], segs["rg"], ret_norm_g[i].astype(F32),
                           bsz, seq // RET_CHUNK)
        merged = _merge(y_att, y_ret, w_att_out[i].astype(F32), w_ret_out[i].astype(F32),
                        segs["ga"], segs["gr"])
        z1 = _resid_mm(merged, w_out[i].astype(F32), xf)
        x1, x1p, gates, top_idx = _ln_router(z1, ln1_g[i], ln1_b[i], w_router[i], b_router[i])
        item_dst, block_expert, item_rows, n_used = _routing_tables(top_idx[:, :TOP_K], t)
        yp = _moe(x1p, item_dst, block_expert, item_rows, n_used, w_gate_up[i].astype(BF16),
                  b_gate_up[i].astype(F32), w_down[i].astype(BF16), b_down[i].astype(F32), t, d)
        x2, x2b = _combine_ln(x1, yp, gates, ln2_g[i], ln2_b[i])
        z3 = _ple(x2b, w_ple_gate[i].astype(F32), p[i].reshape(t, -1), w_ple[i].astype(F32), x2)
        xf = _ln(z3, ln3_g[i], ln3_b[i])
    return xf.reshape(bsz, seq, d)
```
